```python
import jax
import jax.numpy as jnp
from jax import lax
import numpy as np

D_MODEL = 1024
BATCH = 16
SEQ = 2048
DEPTH = 2

CTX_LEN = 256
GRID_W = 64
EPS = 1e-6
F32 = jnp.float32

MLSTM_HEADS = 4
MLSTM_HEAD_DIM = 64
MLSTM_WIDTH = MLSTM_HEADS * MLSTM_HEAD_DIM
MLSTM_CHUNK = 128
MLA_HEADS = 8
MLA_Q_RANK = 256
MLA_KV_RANK = 128
MLA_NOPE_DIM = 64
MLA_ROPE_DIM = 32
MLA_V_DIM = 64
MLA_QK_DIM = MLA_NOPE_DIM + MLA_ROPE_DIM
MLA_WIDTH = MLA_HEADS * MLA_V_DIM
ROPE_BASE = 10000.0
ATTN_BLOCK = 128
GMLP_GROUPS = 4
GMLP_GROUP_DIM = 64
GMLP_WIDTH = GMLP_GROUPS * GMLP_GROUP_DIM
GMLP_CHUNK = 128
MIX_WIDTH = MLSTM_WIDTH + MLA_WIDTH + GMLP_WIDTH

OFF_QA = 0
OFF_KA = OFF_QA + MLSTM_WIDTH
OFF_VA = OFF_KA + MLSTM_WIDTH
OFF_OA = OFF_VA + MLSTM_WIDTH
OFF_GA = OFF_OA + MLSTM_WIDTH
OFF_CQ = OFF_GA + 4 * MLSTM_HEADS
OFF_CKV = OFF_CQ + MLA_Q_RANK
OFF_KR = OFF_CKV + MLA_KV_RANK
OFF_GM = OFF_KR + MLA_ROPE_DIM
IN_COLS = OFF_GM + 2 * GMLP_WIDTH

D_FF = 2816
N_EXPERTS = 8
TOP_K = 2
D_FF_EXPERT = 3584
MOE_BLOCK = 512
N_DENSE = (DEPTH + 1) // 2
N_MOE = DEPTH // 2

kernel_name = 'hybrid_mlstm_mla_gmlp_moe_block'


def rms_norm(x, g):
    xf = x.astype(F32)
    y = xf * lax.rsqrt(jnp.mean(xf * xf, axis=-1, keepdims=True) + EPS)
    return y.astype(x.dtype) * g


def layer_norm(x, g, b):
    xf = x.astype(F32)
    mu = jnp.mean(xf, axis=-1, keepdims=True)
    var = jnp.mean(jnp.square(xf - mu), axis=-1, keepdims=True)
    return ((xf - mu) * lax.rsqrt(var + EPS)).astype(x.dtype) * g + b


def axial_rope(rows):
    row = jnp.repeat(jnp.arange(rows), GRID_W).astype(F32)
    col = jnp.tile(jnp.arange(GRID_W), rows).astype(F32)
    n_freq = MLA_ROPE_DIM // 4
    inv = ROPE_BASE ** (-jnp.arange(n_freq, dtype=F32) / n_freq)
    ang = jnp.concatenate([row[:, None] * inv, col[:, None] * inv], axis=-1)
    return jnp.cos(ang), jnp.sin(ang)


def apply_rope(t, cos, sin):
    half = t.shape[-1] // 2
    t1, t2 = t[..., :half], t[..., half:]
    cos = cos[None, :, None, :].astype(t.dtype)
    sin = sin[None, :, None, :].astype(t.dtype)
    return jnp.concatenate([t1 * cos - t2 * sin, t1 * sin + t2 * cos], axis=-1)


def mlstm_scan(q, k, v, log_i, log_f, state, with_h):
    bsz, t_len, heads, e = q.shape
    L = MLSTM_CHUNK
    nc = t_len // L

    def chunks(a):
        return jnp.moveaxis(a.reshape((bsz, nc, L) + a.shape[2:]), 1, 0)

    scan_order = jnp.tril(jnp.ones((L, L), bool))

    def step(carry, inp):
        C, n, m = carry
        qc, kc, vc, ic, fc = inp
        b = jnp.cumsum(fc, axis=1).transpose(0, 2, 1)
        ig = ic.transpose(0, 2, 1)
        b_last = b[:, :, -1]
        g = b_last[..., None] - b + ig
        m_new = jnp.maximum(b_last + m, jnp.max(g, axis=-1))
        w = jnp.exp(g - m_new[..., None])
        decay = jnp.exp(b_last + m - m_new)
        C_new = decay[..., None, None] * C + jnp.einsum('bhs,bshd,bshe->bhde', w, kc, vc)
        n_new = decay[..., None] * n + jnp.einsum('bhs,bshd->bhd', w, kc)
        if not with_h:
            return (C_new, n_new, m_new), None
        dmat = b[..., :, None] - b[..., None, :] + ig[..., None, :]
        dmat = jnp.where(scan_order, dmat, -jnp.inf)
        inter = b + m[..., None]
        m_t = jnp.maximum(inter, jnp.max(dmat, axis=-1))
        s = jnp.einsum('blhd,bshd->bhls', qc, kc) * jnp.exp(dmat - m_t[..., None])
        a = jnp.exp(inter - m_t)
        num = jnp.einsum('bhls,bshe->bhle', s, vc) + a[..., None] * jnp.einsum('blhd,bhde->bhle', qc, C)
        den = jnp.sum(s, axis=-1) + a * jnp.einsum('blhd,bhd->bhl', qc, n)
        h = num / jnp.maximum(jnp.abs(den), jnp.exp(-m_t))[..., None]
        return (C_new, n_new, m_new), h.transpose(0, 2, 1, 3)

    state, hs = lax.scan(step, state, (chunks(q), chunks(k), chunks(v), chunks(log_i), chunks(log_f)))
    if not with_h:
        return state, None
    return state, jnp.moveaxis(hs, 0, 1).reshape(bsz, t_len, heads, e)


def mlstm_split(z, gate_b):
    bsz, t_len, _ = z.shape

    def heads(a):
        return a.reshape(bsz, t_len, MLSTM_HEADS, MLSTM_HEAD_DIM).astype(F32)

    q = heads(z[..., OFF_QA:OFF_KA])
    k = heads(z[..., OFF_KA:OFF_VA]) * (MLSTM_HEAD_DIM ** -0.5)
    v = heads(z[..., OFF_VA:OFF_OA])
    o = jax.nn.sigmoid(z[..., OFF_OA:OFF_GA])
    gates = (z[..., OFF_GA:OFF_CQ] + gate_b).astype(F32).reshape(bsz, t_len, 4, MLSTM_HEADS)
    log_i = gates[:, :, 0:2]
    log_f = jax.nn.log_sigmoid(gates[:, :, 2:4])
    return q, k, v, o, log_i, log_f


def mlstm_mixer(z_lat, z_ctx, gate_b, norm_g, with_ctx_out):
    ql, kl, vl, ol, il, fl = mlstm_split(z_lat, gate_b)
    qc, kc, vc, oc, ic, fc = mlstm_split(z_ctx, gate_b)
    bsz = z_lat.shape[0]
    zero_state = (jnp.zeros((bsz, MLSTM_HEADS, MLSTM_HEAD_DIM, MLSTM_HEAD_DIM), F32),
                  jnp.zeros((bsz, MLSTM_HEADS, MLSTM_HEAD_DIM), F32),
                  jnp.zeros((bsz, MLSTM_HEADS), F32))
    g = norm_g.reshape(MLSTM_HEADS, MLSTM_HEAD_DIM)
    h_lat = 0.0
    h_ctx = 0.0
    for d in range(2):
        rev = d == 1
        fl_ = (lambda a: jnp.flip(a, axis=1)) if rev else (lambda a: a)
        ctx_state, hc = mlstm_scan(fl_(qc), fl_(kc), fl_(vc), fl_(ic[:, :, d]), fl_(fc[:, :, d]),
                                   zero_state, with_ctx_out)
        _, hl = mlstm_scan(fl_(ql), fl_(kl), fl_(vl), fl_(il[:, :, d]), fl_(fl[:, :, d]), ctx_state, True)
        h_lat = h_lat + fl_(hl)
        if with_ctx_out:
            h_ctx = h_ctx + fl_(hc)
    bl, tl = z_lat.shape[:2]
    out_lat = (ol * rms_norm(h_lat, g).reshape(bl, tl, MLSTM_WIDTH)).astype(z_lat.dtype)
    if not with_ctx_out:
        return out_lat, None
    tc = z_ctx.shape[1]
    out_ctx = (oc * rms_norm(h_ctx, g).reshape(bl, tc, MLSTM_WIDTH)).astype(z_ctx.dtype)
    return out_lat, out_ctx


def with_rope(t, rope):
    return jnp.concatenate([t[..., :MLA_NOPE_DIM], apply_rope(t[..., MLA_NOPE_DIM:], rope[0], rope[1])], axis=-1)


def mla_queries(z, cq_g, w_uq, q_g, rope):
    bsz, t_len, _ = z.shape
    q = (rms_norm(z[..., OFF_CQ:OFF_CKV], cq_g) @ w_uq).reshape(bsz, t_len, MLA_HEADS, MLA_QK_DIM)
    q = rms_norm(q, q_g)
    return q if rope is None else with_rope(q, rope)


def mla_keys_values(z, ckv_g, w_ukv, k_g, rope):
    bsz, t_len, _ = z.shape
    kv = (rms_norm(z[..., OFF_CKV:OFF_KR], ckv_g) @ w_ukv).reshape(bsz, t_len, MLA_HEADS, MLA_NOPE_DIM + MLA_V_DIM)
    k_rope = jnp.broadcast_to(z[..., None, OFF_KR:OFF_GM], (bsz, t_len, MLA_HEADS, MLA_ROPE_DIM))
    k = rms_norm(jnp.concatenate([kv[..., :MLA_NOPE_DIM], k_rope], axis=-1), k_g)
    v = kv[..., MLA_NOPE_DIM:]
    k = k if rope is None else with_rope(k, rope)
    return k, v


def attention(q, k, v):
    s = jnp.einsum('bqhd,bkhd->bhqk', q, k).astype(F32) * (MLA_QK_DIM ** -0.5)
    p = jax.nn.softmax(s, axis=-1).astype(v.dtype)
    return jnp.einsum('bhqk,bkhd->bqhd', p, v)


def blocked_attention(q, k, v):
    bsz, t_len, heads, dq = q.shape
    nb = t_len // ATTN_BLOCK
    qb = jnp.moveaxis(q.reshape(bsz, nb, ATTN_BLOCK, heads, dq), 1, 0)
    ob = lax.map(lambda qq: attention(qq, k, v), qb)
    return jnp.moveaxis(ob, 0, 1).reshape(bsz, t_len, heads * v.shape[-1])


def gmlp_mixer(z, ln_g, ln_b, w_s, b_s):
    bsz, t_len, _ = z.shape
    a = jax.nn.gelu(z[..., OFF_GM:IN_COLS])
    u, v = a[..., :GMLP_WIDTH], a[..., GMLP_WIDTH:]
    v = layer_norm(v, ln_g, ln_b).reshape(bsz, t_len // GMLP_CHUNK, GMLP_CHUNK, GMLP_GROUPS, GMLP_GROUP_DIM)
    sv = jnp.einsum('gpq,bcqgd->bcpgd', w_s, v) + b_s.T[None, None, :, :, None]
    return u * sv.reshape(bsz, t_len, GMLP_WIDTH)


def swiglu(h, w1, w3, w2):
    return (jax.nn.silu(h @ w1) * (h @ w3)) @ w2


def moe_swiglu(h, router_w, router_b, w1, w3, w2):
    d = h.shape[-1]
    hf = h.reshape(-1, d)
    n = hf.shape[0]
    logits = (hf @ router_w).astype(F32) + router_b.astype(F32)
    top_logit, top_idx = lax.top_k(logits, TOP_K)
    gates = jax.nn.softmax(top_logit, axis=-1).astype(h.dtype)
    flat_e = top_idx.reshape(-1)
    flat_tok = jnp.repeat(jnp.arange(n, dtype=jnp.int32), TOP_K)
    order = jnp.argsort(flat_e)
    sorted_e = flat_e[order]
    counts = jnp.bincount(flat_e, length=N_EXPERTS)
    padded = (counts + MOE_BLOCK - 1) // MOE_BLOCK * MOE_BLOCK
    padded_end = jnp.cumsum(padded)
    rank = jnp.arange(n * TOP_K) - (jnp.cumsum(counts) - counts)[sorted_e]
    dest = (padded_end - padded)[sorted_e] + rank
    cap = (n * TOP_K + MOE_BLOCK - 1) // MOE_BLOCK * MOE_BLOCK + N_EXPERTS * MOE_BLOCK
    slot_tok = jnp.zeros((cap,), jnp.int32).at[dest].set(flat_tok[order])
    slot_gate = jnp.zeros((cap,), h.dtype).at[dest].set(gates.reshape(-1)[order])
    n_blocks = cap // MOE_BLOCK
    block_expert = jnp.minimum(
        jnp.searchsorted(padded_end, jnp.arange(n_blocks) * MOE_BLOCK, side='right'), N_EXPERTS - 1)
    xb = hf[slot_tok].reshape(n_blocks, MOE_BLOCK, d)

    def expert_block(args):
        xe, e = args
        return swiglu(xe, w1[e], w3[e], w2[e])

    yb = lax.map(expert_block, (xb, block_expert))
    y = jax.ops.segment_sum(yb.reshape(cap, d) * slot_gate[:, None], slot_tok, num_segments=n)
    return y.reshape(h.shape)


def channel_mixer(layer, h, ffn_w1, ffn_w3, ffn_w2, moe_router_w, moe_router_b, moe_w1, moe_w3, moe_w2):
    j = layer // 2
    if layer % 2 == 0:
        return swiglu(h, ffn_w1[j], ffn_w3[j], ffn_w2[j])
    return moe_swiglu(h, moe_router_w[j], moe_router_b[j], moe_w1[j], moe_w3[j], moe_w2[j])


def setup_inputs(seed: int = 0) -> dict:
    key = jax.random.key(seed)
    ks = iter(jax.random.split(key, 40))
    D = D_MODEL

    def nrm(shape, scale):
        return jax.random.normal(next(ks), shape, F32) * scale

    def gain(shape):
        return 1.0 + nrm(shape, 0.02)

    f_bias = jnp.tile(jnp.linspace(3.0, 6.0, MLSTM_HEADS), 2)[None, :] + nrm((DEPTH, 2 * MLSTM_HEADS), 0.1)
    i_bias = nrm((DEPTH, 2 * MLSTM_HEADS), 0.1)
    return {
        'x': nrm((BATCH, SEQ, D), 1.0),
        'c': nrm((BATCH, D), 1.0),
        'ctx': nrm((BATCH, CTX_LEN, D), 1.0),
        'c_ctx': nrm((D,), 1.0),
        'ada_w': nrm((DEPTH, D, 6 * D), 0.5 * D ** -0.5),
        'ada_b': nrm((DEPTH, 6 * D), 0.02),
        'norm1_g': gain((DEPTH, D)),
        'norm2_g': gain((DEPTH, D)),
        'w_in': nrm((DEPTH, D, IN_COLS), D ** -0.5),
        'w_out': nrm((DEPTH, MIX_WIDTH, D), MIX_WIDTH ** -0.5),
        'mlstm_gate_b': jnp.concatenate([i_bias, f_bias], axis=-1),
        'mlstm_norm_g': gain((DEPTH, MLSTM_WIDTH)),
        'mla_cq_g': gain((DEPTH, MLA_Q_RANK)),
        'mla_ckv_g': gain((DEPTH, MLA_KV_RANK)),
        'mla_w_uq': nrm((DEPTH, MLA_Q_RANK, MLA_HEADS * MLA_QK_DIM), MLA_Q_RANK ** -0.5),
        'mla_w_ukv': nrm((DEPTH, MLA_KV_RANK, MLA_HEADS * (MLA_NOPE_DIM + MLA_V_DIM)), MLA_KV_RANK ** -0.5),
        'mla_q_g': gain((DEPTH, MLA_QK_DIM)),
        'mla_k_g': gain((DEPTH, MLA_QK_DIM)),
        'gmlp_ln_g': gain((DEPTH, GMLP_WIDTH)),
        'gmlp_ln_b': nrm((DEPTH, GMLP_WIDTH), 0.02),
        'gmlp_w_s': nrm((DEPTH, GMLP_GROUPS, GMLP_CHUNK, GMLP_CHUNK), GMLP_CHUNK ** -0.5),
        'gmlp_b_s': gain((DEPTH, GMLP_GROUPS, GMLP_CHUNK)),
        'ffn_w1': nrm((N_DENSE, D, D_FF), D ** -0.5),
        'ffn_w3': nrm((N_DENSE, D, D_FF), D ** -0.5),
        'ffn_w2': nrm((N_DENSE, D_FF, D), D_FF ** -0.5),
        'moe_router_w': nrm((N_MOE, D, N_EXPERTS), D ** -0.5),
        'moe_router_b': nrm((N_MOE, N_EXPERTS), 0.01),
        'moe_w1': nrm((N_MOE, N_EXPERTS, D, D_FF_EXPERT), D ** -0.5),
        'moe_w3': nrm((N_MOE, N_EXPERTS, D, D_FF_EXPERT), D ** -0.5),
        'moe_w2': nrm((N_MOE, N_EXPERTS, D_FF_EXPERT, D), D_FF_EXPERT ** -0.5),
    }


def reference(x, c, ctx, c_ctx, ada_w, ada_b, norm1_g, norm2_g, w_in, w_out, mlstm_gate_b, mlstm_norm_g,
              mla_cq_g, mla_ckv_g, mla_w_uq, mla_w_ukv, mla_q_g, mla_k_g, gmlp_ln_g, gmlp_ln_b, gmlp_w_s,
              gmlp_b_s, ffn_w1, ffn_w3, ffn_w2, moe_router_w, moe_router_b, moe_w1, moe_w3, moe_w2):
    rows = x.shape[1] // GRID_W
    rope = axial_rope(rows)
    ffn = (ffn_w1, ffn_w3, ffn_w2, moe_router_w, moe_router_b, moe_w1, moe_w3, moe_w2)
    xc = ctx
    for l in range(DEPTH):
        last = l == DEPTH - 1
        sh1, sc1, g1, sh2, sc2, g2 = jnp.split((jax.nn.silu(c) @ ada_w[l] + ada_b[l])[:, None, :], 6, axis=-1)
        sh1c, sc1c, g1c, sh2c, sc2c, g2c = jnp.split(jax.nn.silu(c_ctx) @ ada_w[l] + ada_b[l], 6, axis=-1)
        z = (rms_norm(x, norm1_g[l]) * (1.0 + sc1) + sh1) @ w_in[l]
        zc = (rms_norm(xc, norm1_g[l]) * (1.0 + sc1c) + sh1c) @ w_in[l]
        a_lat, a_ctx = mlstm_mixer(z, zc, mlstm_gate_b[l], mlstm_norm_g[l], not last)
        k_lat, v_lat = mla_keys_values(z, mla_ckv_g[l], mla_w_ukv[l], mla_k_g[l], rope)
        k_ctx, v_ctx = mla_keys_values(zc, mla_ckv_g[l], mla_w_ukv[l], mla_k_g[l], None)
        q_lat = mla_queries(z, mla_cq_g[l], mla_w_uq[l], mla_q_g[l], rope)
        b_lat = blocked_attention(q_lat, jnp.concatenate([k_lat, k_ctx], axis=1),
                                  jnp.concatenate([v_lat, v_ctx], axis=1))
        c_lat = gmlp_mixer(z, gmlp_ln_g[l], gmlp_ln_b[l], gmlp_w_s[l], gmlp_b_s[l])
        x = x + g1 * (jnp.concatenate([a_lat, b_lat, c_lat], axis=-1) @ w_out[l])
        x = x + g2 * channel_mixer(l, rms_norm(x, norm2_g[l]) * (1.0 + sc2) + sh2, *ffn)
        if not last:
            q_ctx = mla_queries(zc, mla_cq_g[l], mla_w_uq[l], mla_q_g[l], None)
            b_ctx = attention(q_ctx, k_ctx, v_ctx).reshape(xc.shape[0], xc.shape[1], MLA_WIDTH)
            c_ctx_out = gmlp_mixer(zc, gmlp_ln_g[l], gmlp_ln_b[l], gmlp_w_s[l], gmlp_b_s[l])
            xc = xc + g1c * (jnp.concatenate([a_ctx, b_ctx, c_ctx_out], axis=-1) @ w_out[l])
            xc = xc + g2c * channel_mixer(l, rms_norm(xc, norm2_g[l]) * (1.0 + sc2c) + sh2c, *ffn)
    return x
```

```python
import functools

import jax
import jax.numpy as jnp
from jax import lax
from jax.experimental import pallas as pl
from jax.experimental.pallas import tpu as pltpu

F32 = jnp.float32
BF16 = jnp.bfloat16
I32 = jnp.int32
HIGHEST = lax.Precision.HIGHEST

D_MODEL = 1024
SEQ = 2048
CTX_LEN = 256
T_ALL = SEQ + CTX_LEN
DEPTH = 2
GRID_W = 64
EPS = 1e-6
MLSTM_HEADS = 4
MLSTM_HEAD_DIM = 64
MLSTM_WIDTH = 256
CHUNK = 128
MLA_HEADS = 8
MLA_Q_RANK = 256
MLA_KV_RANK = 128
MLA_NOPE = 64
MLA_ROPE = 32
MLA_V = 64
MLA_QK = 96
MLA_WIDTH = 512
ROPE_BASE = 10000.0
GMLP_GROUPS = 4
GMLP_GROUP_DIM = 64
GMLP_WIDTH = 256
D_FF = 2816
N_EXPERTS = 8
TOP_K = 2
D_FF_EXPERT = 3584
MOE_BLOCK = 512
OFF_GA = 4 * MLSTM_WIDTH
OFF_CQ = OFF_GA + 4 * MLSTM_HEADS
OFF_CKV = OFF_CQ + MLA_Q_RANK
OFF_KR = OFF_CKV + MLA_KV_RANK
OFF_GM = OFF_KR + MLA_ROPE
IN_COLS = OFF_GM + 2 * GMLP_WIDTH

LANES = 128
HEAD_PAD = LANES
ZC_A = 0
ZC_B = 1024
ZC_GM = 1536
ZC_G = 2048
Z_COLS = ZC_G + 2 * LANES
VMEM_LIMIT = 56 * 1024 * 1024

N_CHUNKS = T_ALL // CHUNK
N_LAT_CHUNKS = SEQ // CHUNK


def _cparams(sem, vmem=VMEM_LIMIT):
    return pltpu.CompilerParams(dimension_semantics=sem, vmem_limit_bytes=vmem)


def _rms(x, g):
    return x * lax.rsqrt(jnp.mean(x * x, axis=-1, keepdims=True) + EPS) * g


def _silu(x):
    return x * jax.nn.sigmoid(x)


def _modulated_norm(x, g, shift, scale):
    return _rms(x, g) * (1.0 + scale) + shift


def _row_mods(ml_ref, mc_ref, first, tile_start, rows):
    row = tile_start + lax.broadcasted_iota(I32, (rows, 1), 0)
    is_ctx = row >= SEQ
    return tuple(jnp.where(is_ctx, mc_ref[0, first + k:first + k + 1, :], ml_ref[0, first + k:first + k + 1, :])
                 for k in range(3))


def _ada_kernel(c_ref, w_ref, b_ref, o_ref):
    s = _silu(c_ref[...]).astype(BF16)
    o_ref[0] = jnp.dot(s, w_ref[0], preferred_element_type=F32) + b_ref[0]


def _ada_call(cvec, ada_w, ada_b):
    rows = cvec.shape[0]
    return pl.pallas_call(
        _ada_kernel,
        grid=(DEPTH, 6),
        in_specs=[pl.BlockSpec((rows, D_MODEL), lambda l, j: (0, 0)),
                  pl.BlockSpec((1, D_MODEL, D_MODEL), lambda l, j: (l, 0, j)),
                  pl.BlockSpec((1, 1, D_MODEL), lambda l, j: (l, 0, j))],
        out_specs=pl.BlockSpec((1, rows, D_MODEL), lambda l, j: (l, 0, j)),
        out_shape=jax.ShapeDtypeStruct((DEPTH, rows, 6 * D_MODEL), F32),
        compiler_params=_cparams(("arbitrary", "arbitrary")),
        name="adaln",
    )(cvec, ada_w, ada_b)


def _inproj_kernel(x_ref, ml_ref, mc_ref, g_ref, w_ref, z_ref, *, tr):
    shift, scale, _ = _row_mods(ml_ref, mc_ref, 0, pl.program_id(1) * tr, tr)
    xn = _modulated_norm(x_ref[0], g_ref[...], shift, scale)
    z_ref[0] = jnp.dot(xn.astype(BF16), w_ref[...], preferred_element_type=F32)


def _inproj_call(xall, mods, norm_g, w_in_r, n_ctx_row):
    bsz = xall.shape[0]
    tr = 768
    return pl.pallas_call(
        functools.partial(_inproj_kernel, tr=tr),
        grid=(bsz, T_ALL // tr),
        in_specs=[pl.BlockSpec((1, tr, D_MODEL), lambda b, i: (b, i, 0)),
                  pl.BlockSpec((1, 6, D_MODEL), lambda b, i: (b, 0, 0)),
                  pl.BlockSpec((1, 6, D_MODEL), lambda b, i: (n_ctx_row, 0, 0)),
                  pl.BlockSpec((1, D_MODEL), lambda b, i: (0, 0)),
                  pl.BlockSpec((D_MODEL, Z_COLS), lambda b, i: (0, 0))],
        out_specs=pl.BlockSpec((1, tr, Z_COLS), lambda b, i: (b, i, 0)),
        out_shape=jax.ShapeDtypeStruct((bsz, T_ALL, Z_COLS), F32),
        compiler_params=_cparams(("parallel", "parallel")),
        name="in_proj",
    )(xall, mods, mods, norm_g, w_in_r)


def _scan_chunk(d, j):
    fwd_chunk = (j + N_LAT_CHUNKS) % N_CHUNKS
    return jnp.where(d == 0, fwd_chunk, N_CHUNKS - 1 - j)


def _mlstm_kernel(za_ref, zg_ref, gb_ref, ng_ref, o_ref, cbd_ref, n_ref, m_ref, hf_ref):
    L = CHUNK
    E = MLSTM_HEAD_DIM
    W = MLSTM_WIDTH
    d = pl.program_id(1)
    j = pl.program_id(2)
    fwd = d == 0
    row0 = pl.multiple_of(_scan_chunk(d, j) * L, L)

    @pl.when(j == 0)
    def _():
        cbd_ref[...] = jnp.zeros_like(cbd_ref)
        n_ref[...] = jnp.zeros_like(n_ref)
        m_ref[...] = jnp.zeros_like(m_ref)

    q = za_ref[0, :, 0:W]
    k = za_ref[0, :, W:2 * W] * (E ** -0.5)
    v = za_ref[0, :, 2 * W:3 * W]
    qb = q.astype(BF16)
    kb = k.astype(BF16)
    vb = v.astype(BF16)

    gates = zg_ref[0] + gb_ref[0]
    logf = jnp.minimum(gates, 0.0) - jnp.log1p(jnp.exp(-jnp.abs(gates)))

    r_i = lax.broadcasted_iota(I32, (L, L), 0)
    c_i = lax.broadcasted_iota(I32, (L, L), 1)
    prec = jnp.logical_or(jnp.logical_and(fwd, c_i <= r_i), jnp.logical_and(jnp.logical_not(fwd), c_i >= r_i))
    tri_col = prec.astype(F32)
    tri_row = jnp.logical_or(jnp.logical_and(fwd, r_i <= c_i),
                             jnp.logical_and(jnp.logical_not(fwd), r_i >= c_i)).astype(F32)
    b_cols = jnp.dot(tri_col, logf, precision=HIGHEST, preferred_element_type=F32)
    gates_t = gates.T
    b_rows = jnp.dot(logf.T, tri_row, precision=HIGHEST, preferred_element_type=F32)
    b_last_all = jnp.where(fwd, b_cols[L - 1:L, :], b_cols[0:1, :])

    cbd_old = cbd_ref[...]
    n_old = n_ref[...]
    qc_all = jnp.dot(qb, cbd_old.astype(BF16), preferred_element_type=F32)
    qn_all = q * n_old

    lane_head = lax.broadcasted_iota(I32, (1, W), 1) // E
    w_full = jnp.zeros((L, W), F32)
    dec_row = jnp.zeros((1, W), F32)
    h_heads = []
    for h in range(MLSTM_HEADS):
        sl = slice(h * E, (h + 1) * E)
        b_c = b_cols[:, 4 + h:5 + h]
        b_r = b_rows[4 + h:5 + h, :]
        i_c = gates[:, h:h + 1]
        i_r = gates_t[h:h + 1, :]
        b_last = b_last_all[:, 4 + h:5 + h]
        m_old = m_ref[0:1, h:h + 1]
        g_r = b_last - b_r + i_r
        g_c = b_last - b_c + i_c
        m_new = jnp.maximum(b_last + m_old, jnp.max(g_r, axis=-1, keepdims=True))
        w_c = jnp.exp(g_c - m_new)
        decay = jnp.exp(b_last + m_old - m_new)
        w_full = jnp.where(lane_head == h, w_c, w_full)
        dec_row = jnp.where(lane_head == h, decay, dec_row)
        m_ref[0:1, h:h + 1] = m_new

        dmat = jnp.where(prec, b_c - b_r + i_r, -jnp.inf)
        inter = b_c + m_old
        m_t = jnp.maximum(inter, jnp.max(dmat, axis=-1, keepdims=True))
        sqk = lax.dot_general(qb[:, sl], kb[:, sl], (((1,), (1,)), ((), ())), preferred_element_type=F32)
        s = sqk * jnp.exp(dmat - m_t)
        a = jnp.exp(inter - m_t)
        num = jnp.dot(s.astype(BF16), vb[:, sl], preferred_element_type=F32) + a * qc_all[:, sl]
        den = jnp.sum(s, axis=-1, keepdims=True) + a * jnp.sum(qn_all[:, sl], axis=-1, keepdims=True)
        h_heads.append(num / jnp.maximum(jnp.abs(den), jnp.exp(-m_t)))

    kw = k * w_full
    upd = jnp.dot(kw.T.astype(BF16), vb, preferred_element_type=F32)
    rh = lax.broadcasted_iota(I32, (W, W), 0) // E
    ch = lax.broadcasted_iota(I32, (W, W), 1) // E
    dec_col = jnp.zeros((W, 1), F32)
    row_head = lax.broadcasted_iota(I32, (W, 1), 0) // E
    for h in range(MLSTM_HEADS):
        dec_col = jnp.where(row_head == h, dec_row[:, h * E:h * E + 1], dec_col)
    cbd_ref[...] = dec_col * cbd_old + jnp.where(rh == ch, upd, 0.0)
    n_ref[...] = dec_row * n_old + jnp.sum(kw, axis=0, keepdims=True)

    @pl.when(fwd)
    def _():
        for h in range(MLSTM_HEADS):
            hf_ref[pl.ds(row0, L), h * E:(h + 1) * E] = h_heads[h]

    @pl.when(jnp.logical_not(fwd))
    def _():
        for h in range(MLSTM_HEADS):
            sl = slice(h * E, (h + 1) * E)
            hs = hf_ref[pl.ds(row0, L), sl] + h_heads[h]
            y = _rms(hs, ng_ref[:, sl])
            o_ref[0, :, sl] = jax.nn.sigmoid(za_ref[0, :, 3 * W + h * E:3 * W + (h + 1) * E]) * y


def _mlstm_call(z, gate_b, norm_g):
    bsz = z.shape[0]
    g_blk = ZC_G // LANES
    last = N_CHUNKS - 1
    return pl.pallas_call(
        _mlstm_kernel,
        grid=(bsz, 2, N_CHUNKS),
        in_specs=[pl.BlockSpec((1, CHUNK, 4 * MLSTM_WIDTH), lambda b, d, j: (b, _scan_chunk(d, j), 0)),
                  pl.BlockSpec((1, CHUNK, LANES), lambda b, d, j: (b, _scan_chunk(d, j), g_blk + d)),
                  pl.BlockSpec((1, 1, LANES), lambda b, d, j: (d, 0, 0)),
                  pl.BlockSpec((1, MLSTM_WIDTH), lambda b, d, j: (0, 0))],
        out_specs=pl.BlockSpec((1, CHUNK, MLSTM_WIDTH),
                               lambda b, d, j: (b, jnp.where(d == 0, last, last - j), 0)),
        out_shape=jax.ShapeDtypeStruct((bsz, T_ALL, MLSTM_WIDTH), F32),
        scratch_shapes=[pltpu.VMEM((MLSTM_WIDTH, MLSTM_WIDTH), F32),
                        pltpu.VMEM((1, MLSTM_WIDTH), F32),
                        pltpu.VMEM((1, LANES), F32),
                        pltpu.VMEM((T_ALL, MLSTM_WIDTH), F32)],
        compiler_params=_cparams(("parallel", "arbitrary", "arbitrary")),
        name="mlstm",
    )(z, z, gate_b, norm_g)


def _rope(t, cos_t, sin_a, sin_b):
    return t * cos_t + pltpu.roll(t, LANES - 16, 1) * sin_a + pltpu.roll(t, 16, 1) * sin_b


def _head_rms(t, g):
    return t * lax.rsqrt(jnp.sum(t * t, axis=-1, keepdims=True) * (1.0 / MLA_QK) + EPS) * g


def _qkv_kernel(z_ref, cqg_ref, ckvg_ref, wq_ref, wk_ref, wv_ref, qg_ref, kg_ref, ct_ref, sa_ref, sb_ref,
                q_ref, k_ref, v_ref, *, tr):
    cqn = _rms(z_ref[0, :, 0:MLA_Q_RANK], cqg_ref[...]).astype(BF16)
    ckvn = _rms(z_ref[0, :, MLA_Q_RANK:MLA_Q_RANK + MLA_KV_RANK], ckvg_ref[...]).astype(BF16)
    k_rope = z_ref[0, :, MLA_Q_RANK + MLA_KV_RANK:MLA_Q_RANK + MLA_KV_RANK + LANES]
    q_raw = jnp.dot(cqn, wq_ref[...], preferred_element_type=F32)
    k_raw = jnp.dot(ckvn, wk_ref[...], preferred_element_type=F32)
    v_raw = jnp.dot(ckvn, wv_ref[...], preferred_element_type=F32)
    cos_t = ct_ref[...]
    sin_a = sa_ref[...]
    sin_b = sb_ref[...]
    lane = lax.broadcasted_iota(I32, (tr, LANES), 1)
    for h in range(MLA_HEADS):
        sl = slice(h * HEAD_PAD, (h + 1) * HEAD_PAD)
        qh = _rope(_head_rms(q_raw[:, sl], qg_ref[...]), cos_t, sin_a, sin_b)
        q_ref[0, :, sl] = (qh * (MLA_QK ** -0.5)).astype(BF16)
        kh = _rope(_head_rms(k_raw[:, sl] + k_rope, kg_ref[...]), cos_t, sin_a, sin_b)
        k_ref[0, :, sl] = kh.astype(BF16)
        v_ref[0, :, sl] = jnp.where(lane == MLA_V, 1.0, v_raw[:, sl]).astype(BF16)


def _qkv_call(z, p, rope_tabs):
    bsz = z.shape[0]
    tr = 768
    hw = MLA_HEADS * HEAD_PAD
    const = lambda b, i: (0, 0)
    out = jax.ShapeDtypeStruct((bsz, T_ALL, hw), BF16)
    tab = pl.BlockSpec((tr, LANES), lambda b, i: (i, 0))
    ospec = pl.BlockSpec((1, tr, hw), lambda b, i: (b, i, 0))
    return pl.pallas_call(
        functools.partial(_qkv_kernel, tr=tr),
        grid=(bsz, T_ALL // tr),
        in_specs=[pl.BlockSpec((1, tr, 512), lambda b, i: (b, i, ZC_B // 512)),
                  pl.BlockSpec((1, MLA_Q_RANK), const), pl.BlockSpec((1, MLA_KV_RANK), const),
                  pl.BlockSpec((MLA_Q_RANK, hw), const), pl.BlockSpec((MLA_KV_RANK, hw), const),
                  pl.BlockSpec((MLA_KV_RANK, hw), const),
                  pl.BlockSpec((1, LANES), const), pl.BlockSpec((1, LANES), const),
                  tab, tab, tab],
        out_specs=[ospec, ospec, ospec],
        out_shape=[out, out, out],
        compiler_params=_cparams(("parallel", "parallel")),
        name="mla_qkv",
    )(z, p["cq_g"], p["ckv_g"], p["w_uq"], p["w_k"], p["w_v"], p["q_g"], p["k_g"], *rope_tabs)


ATT_TQ = 256
ATT_HEADS_PER_STEP = 2


def _attn_kernel(q_ref, k_ref, v_ref, o_ref):
    i = pl.program_id(2)

    def run(k0):
        for hh in range(ATT_HEADS_PER_STEP):
            sl = slice(hh * HEAD_PAD, (hh + 1) * HEAD_PAD)
            s = lax.dot_general(q_ref[0, :, sl], k_ref[0, k0:T_ALL, sl], (((1,), (1,)), ((), ())),
                                preferred_element_type=F32)
            p = jnp.exp(s - jnp.max(s, axis=-1, keepdims=True)).astype(BF16)
            oe = jnp.dot(p, v_ref[0, k0:T_ALL, sl], preferred_element_type=F32)
            o_ref[0, :, hh * MLA_V:(hh + 1) * MLA_V] = oe[:, 0:MLA_V] / oe[:, MLA_V:MLA_V + 1]

    @pl.when(i < SEQ // ATT_TQ)
    def _():
        run(0)

    @pl.when(i >= SEQ // ATT_TQ)
    def _():
        run(SEQ)


def _attn_call(q, k, v, t_out):
    bsz = q.shape[0]
    w = ATT_HEADS_PER_STEP * HEAD_PAD
    return pl.pallas_call(
        _attn_kernel,
        grid=(bsz, MLA_HEADS // ATT_HEADS_PER_STEP, t_out // ATT_TQ),
        in_specs=[pl.BlockSpec((1, ATT_TQ, w), lambda b, h, i: (b, i, h)),
                  pl.BlockSpec((1, T_ALL, w), lambda b, h, i: (b, 0, h)),
                  pl.BlockSpec((1, T_ALL, w), lambda b, h, i: (b, 0, h))],
        out_specs=pl.BlockSpec((1, ATT_TQ, ATT_HEADS_PER_STEP * MLA_V), lambda b, h, i: (b, i, h)),
        out_shape=jax.ShapeDtypeStruct((bsz, t_out, MLA_WIDTH), F32),
        compiler_params=_cparams(("parallel", "parallel", "arbitrary")),
        name="mla_attn",
    )(q, k, v)


def _mixout_kernel(x_ref, zg_ref, a_ref, b_ref, ml_ref, mc_ref, lng_ref, lnb_ref, ws_ref, bs_ref, wo_ref,
                   o_ref, cm_ref, *, tr):
    gd = GMLP_GROUP_DIM
    act = jax.nn.gelu(zg_ref[0])
    u = act[:, 0:GMLP_WIDTH]
    vv = act[:, GMLP_WIDTH:2 * GMLP_WIDTH]
    mu = jnp.mean(vv, axis=-1, keepdims=True)
    var = jnp.mean(jnp.square(vv - mu), axis=-1, keepdims=True)
    vn = ((vv - mu) * lax.rsqrt(var + EPS) * lng_ref[...] + lnb_ref[...]).astype(BF16)
    for c in range(tr // CHUNK):
        rows = slice(c * CHUNK, (c + 1) * CHUNK)
        for g in range(GMLP_GROUPS):
            cols = slice(g * gd, (g + 1) * gd)
            sv = jnp.dot(ws_ref[g], vn[rows, cols], preferred_element_type=F32) + bs_ref[:, cols]
            cm_ref[rows, cols] = u[rows, cols] * sv
    y = jnp.dot(a_ref[0].astype(BF16), wo_ref[0:MLSTM_WIDTH, :], preferred_element_type=F32)
    y += jnp.dot(b_ref[0].astype(BF16), wo_ref[MLSTM_WIDTH:MLSTM_WIDTH + MLA_WIDTH, :],
                 preferred_element_type=F32)
    y += jnp.dot(cm_ref[...].astype(BF16), wo_ref[MLSTM_WIDTH + MLA_WIDTH:, :], preferred_element_type=F32)
    _, _, gate = _row_mods(ml_ref, mc_ref, 0, pl.program_id(1) * tr, tr)
    o_ref[0] = x_ref[0] + gate * y


def _mixout_call(xall, z, a_mix, b_mix, mods, p, n_ctx_row, t_out, tr):
    bsz = xall.shape[0]
    const = lambda b, i: (0, 0)
    return pl.pallas_call(
        functools.partial(_mixout_kernel, tr=tr),
        grid=(bsz, t_out // tr),
        in_specs=[pl.BlockSpec((1, tr, D_MODEL), lambda b, i: (b, i, 0)),
                  pl.BlockSpec((1, tr, 2 * GMLP_WIDTH), lambda b, i: (b, i, ZC_GM // (2 * GMLP_WIDTH))),
                  pl.BlockSpec((1, tr, MLSTM_WIDTH), lambda b, i: (b, i, 0)),
                  pl.BlockSpec((1, tr, MLA_WIDTH), lambda b, i: (b, i, 0)),
                  pl.BlockSpec((1, 6, D_MODEL), lambda b, i: (b, 0, 0)),
                  pl.BlockSpec((1, 6, D_MODEL), lambda b, i: (n_ctx_row, 0, 0)),
                  pl.BlockSpec((1, GMLP_WIDTH), const), pl.BlockSpec((1, GMLP_WIDTH), const),
                  pl.BlockSpec((GMLP_GROUPS, CHUNK, CHUNK), lambda b, i: (0, 0, 0)),
                  pl.BlockSpec((CHUNK, GMLP_WIDTH), const),
                  pl.BlockSpec((D_MODEL, D_MODEL), const)],
        out_specs=pl.BlockSpec((1, tr, D_MODEL), lambda b, i: (b, i, 0)),
        out_shape=jax.ShapeDtypeStruct((bsz, t_out, D_MODEL), F32),
        scratch_shapes=[pltpu.VMEM((tr, GMLP_WIDTH), F32)],
        compiler_params=_cparams(("parallel", "parallel")),
        name="mix_out",
    )(xall, z, a_mix, b_mix, mods, mods, p["ln_g"], p["ln_b"], p["w_s"], p["b_s"], p["w_out"])


FFN_SPLIT = 2


def _ffn_kernel(x_ref, ml_ref, mc_ref, g_ref, w1_ref, w3_ref, w2_ref, o_ref, *, tr):
    shift, scale, gate = _row_mods(ml_ref, mc_ref, 3, pl.program_id(1) * tr, tr)
    x = x_ref[0]
    h = _modulated_norm(x, g_ref[...], shift, scale).astype(BF16)
    fc = D_FF // FFN_SPLIT
    y = jnp.zeros((tr, D_MODEL), F32)
    for f in range(FFN_SPLIT):
        cols = slice(f * fc, (f + 1) * fc)
        h1 = jnp.dot(h, w1_ref[:, cols], preferred_element_type=F32)
        h3 = jnp.dot(h, w3_ref[:, cols], preferred_element_type=F32)
        y += jnp.dot((_silu(h1) * h3).astype(BF16), w2_ref[cols, :], preferred_element_type=F32)
    o_ref[0] = x + gate * y


def _ffn_call(x1, mods, norm_g, w1, w3, w2, n_ctx_row):
    bsz = x1.shape[0]
    tr = 768
    const = lambda b, i: (0, 0)
    resident = pl.Buffered(1)
    return pl.pallas_call(
        functools.partial(_ffn_kernel, tr=tr),
        grid=(bsz, T_ALL // tr),
        in_specs=[pl.BlockSpec((1, tr, D_MODEL), lambda b, i: (b, i, 0)),
                  pl.BlockSpec((1, 6, D_MODEL), lambda b, i: (b, 0, 0)),
                  pl.BlockSpec((1, 6, D_MODEL), lambda b, i: (n_ctx_row, 0, 0)),
                  pl.BlockSpec((1, D_MODEL), const),
                  pl.BlockSpec((D_MODEL, D_FF), const, pipeline_mode=resident),
                  pl.BlockSpec((D_MODEL, D_FF), const, pipeline_mode=resident),
                  pl.BlockSpec((D_FF, D_MODEL), const, pipeline_mode=resident)],
        out_specs=pl.BlockSpec((1, tr, D_MODEL), lambda b, i: (b, i, 0)),
        out_shape=jax.ShapeDtypeStruct((bsz, T_ALL, D_MODEL), F32),
        compiler_params=_cparams(("parallel", "parallel")),
        name="dense_ffn",
    )(x1, mods, mods, norm_g, w1, w3, w2)


ROUTE_TR = 1024


def _router_kernel(x_ref, ml_ref, g_ref, rw_ref, rb_ref, h_ref, e_ref, r_ref, gt_ref, cnt_ref, run_ref, *, tr):
    step = pl.program_id(0) * pl.num_programs(1) + pl.program_id(1)

    @pl.when(step == 0)
    def _():
        run_ref[...] = jnp.zeros_like(run_ref)

    h = _modulated_norm(x_ref[0], g_ref[...], ml_ref[0, 3:4, :], ml_ref[0, 4:5, :])
    h_ref[...] = h
    lane = lax.broadcasted_iota(I32, (tr, LANES), 1)
    logits = jnp.dot(h, rw_ref[...], precision=HIGHEST, preferred_element_type=F32) + rb_ref[...]
    logits = jnp.where(lane < N_EXPERTS, logits, -jnp.inf)
    m1 = jnp.max(logits, axis=-1, keepdims=True)
    e1 = jnp.min(jnp.where(logits == m1, lane, LANES), axis=-1, keepdims=True)
    rest = jnp.where(lane == e1, -jnp.inf, logits)
    m2 = jnp.max(rest, axis=-1, keepdims=True)
    e2 = jnp.min(jnp.where(rest == m2, lane, LANES), axis=-1, keepdims=True)
    ex = jnp.exp(m2 - m1)
    g1 = 1.0 / (1.0 + ex)
    g2 = ex / (1.0 + ex)
    onehot = jnp.logical_or(lane == e1, lane == e2 + N_EXPERTS)
    oh = onehot.astype(F32)
    r_i = lax.broadcasted_iota(I32, (tr, tr), 0)
    c_i = lax.broadcasted_iota(I32, (tr, tr), 1)
    before = jnp.dot((c_i < r_i).astype(BF16), oh.astype(BF16), preferred_element_type=F32)
    tot = jnp.sum(oh, axis=0, keepdims=True)
    tot0_shift = pltpu.roll(tot, N_EXPERTS, 1)
    run = run_ref[...]
    first_half = lax.broadcasted_iota(I32, (1, LANES), 1) < N_EXPERTS
    offs = run + jnp.where(first_half, 0.0, tot0_shift)
    ranks = oh * (before + offs)
    rank1 = jnp.sum(jnp.where(lane < N_EXPERTS, ranks, 0.0), axis=-1, keepdims=True)
    rank2 = jnp.sum(jnp.where(lane >= N_EXPERTS, ranks, 0.0), axis=-1, keepdims=True)
    col = lax.broadcasted_iota(I32, (tr, TOP_K), 1)
    e_ref[...] = jnp.where(col == 0, e1, e2)
    r_ref[...] = jnp.where(col == 0, rank1, rank2).astype(I32)
    gt_ref[...] = jnp.where(col == 0, g1, g2)
    both = tot + jnp.where(first_half, pltpu.roll(tot, LANES - N_EXPERTS, 1), tot0_shift)
    new_run = run + both
    run_ref[...] = new_run
    cnt_ref[...] = new_run.astype(I32)


def _router_call(x1, mods, norm_g, rw_p, rb_p):
    bsz = x1.shape[0]
    tr = ROUTE_TR
    n = bsz * SEQ
    nt = SEQ // tr
    const = lambda b, i: (0, 0)
    tok = lambda b, i: (b * nt + i, 0)
    return pl.pallas_call(
        functools.partial(_router_kernel, tr=tr),
        grid=(bsz, nt),
        in_specs=[pl.BlockSpec((1, tr, D_MODEL), lambda b, i: (b, i, 0)),
                  pl.BlockSpec((1, 6, D_MODEL), lambda b, i: (b, 0, 0)),
                  pl.BlockSpec((1, D_MODEL), const),
                  pl.BlockSpec((D_MODEL, LANES), const),
                  pl.BlockSpec((1, LANES), const)],
        out_specs=[pl.BlockSpec((tr, D_MODEL), tok),
                   pl.BlockSpec((tr, TOP_K), tok), pl.BlockSpec((tr, TOP_K), tok), pl.BlockSpec((tr, TOP_K), tok),
                   pl.BlockSpec((1, LANES), const)],
        out_shape=[jax.ShapeDtypeStruct((n, D_MODEL), F32),
                   jax.ShapeDtypeStruct((n, TOP_K), I32), jax.ShapeDtypeStruct((n, TOP_K), I32),
                   jax.ShapeDtypeStruct((n, TOP_K), F32),
                   jax.ShapeDtypeStruct((1, LANES), I32)],
        scratch_shapes=[pltpu.VMEM((1, LANES), F32)],
        compiler_params=_cparams(("arbitrary", "arbitrary")),
        name="moe_router",
    )(x1, mods, norm_g, rw_p, rb_p)


DISPATCH_TD = 512


def _row_copy(src, src_row, dst, dst_row, sem):
    return pltpu.make_async_copy(src.at[pl.ds(src_row, 1)], dst.at[pl.ds(dst_row, 1)], sem)


def _dispatch_kernel(dest_ref, h_hbm, xs_in, xs_out, sem):
    del xs_in
    base = pl.program_id(0) * DISPATCH_TD

    def issue(r, carry):
        for kk in range(TOP_K):
            _row_copy(h_hbm, base + r, xs_out, dest_ref[kk, r], sem).start()
        return carry

    lax.fori_loop(0, DISPATCH_TD, issue, 0)

    def drain(r, carry):
        for kk in range(TOP_K):
            _row_copy(h_hbm, 0, xs_out, 0, sem).wait()
        return carry

    lax.fori_loop(0, DISPATCH_TD, drain, 0)


def _dispatch_call(dest_t, h2, cap):
    n = h2.shape[0]
    xs0 = jnp.zeros((cap, D_MODEL), F32)
    return pl.pallas_call(
        _dispatch_kernel,
        grid=(n // DISPATCH_TD,),
        in_specs=[pl.BlockSpec((TOP_K, DISPATCH_TD), lambda i: (0, i), memory_space=pltpu.SMEM),
                  pl.BlockSpec(memory_space=pl.ANY),
                  pl.BlockSpec(memory_space=pl.ANY)],
        out_specs=pl.BlockSpec(memory_space=pl.ANY),
        out_shape=jax.ShapeDtypeStruct((cap, D_MODEL), F32),
        scratch_shapes=[pltpu.SemaphoreType.DMA(())],
        input_output_aliases={2: 0},
        compiler_params=_cparams(("arbitrary",)),
        name="moe_dispatch",
    )(dest_t, h2, xs0)


EXPERT_FSPLIT = 2


def _expert_kernel(be_ref, na_ref, xs_ref, w1_ref, w3_ref, w2_ref, ys_ref):
    i = pl.program_id(0)
    f = pl.program_id(1)
    active = i < na_ref[0]

    @pl.when(jnp.logical_and(active, f == 0))
    def _():
        ys_ref[...] = jnp.zeros_like(ys_ref)

    @pl.when(active)
    def _():
        x = xs_ref[...].astype(BF16)
        h1 = jnp.dot(x, w1_ref[0], preferred_element_type=F32)
        h3 = jnp.dot(x, w3_ref[0], preferred_element_type=F32)
        ys_ref[...] += jnp.dot((_silu(h1) * h3).astype(BF16), w2_ref[0], preferred_element_type=F32)

    @pl.when(jnp.logical_not(active))
    def _():
        ys_ref[...] = jnp.zeros_like(ys_ref)


def _expert_call(block_expert, n_active, xs, w1, w3, w2):
    cap = xs.shape[0]
    nb = cap // MOE_BLOCK
    fc = D_FF_EXPERT // EXPERT_FSPLIT
    last_f = EXPERT_FSPLIT - 1

    def blk(i, na):
        return jnp.minimum(i, na[0] - 1)

    def fidx(i, f, na):
        return jnp.where(i < na[0], f, last_f)

    grid_spec = pltpu.PrefetchScalarGridSpec(
        num_scalar_prefetch=2,
        grid=(nb, EXPERT_FSPLIT),
        in_specs=[pl.BlockSpec((MOE_BLOCK, D_MODEL), lambda i, f, be, na: (blk(i, na), 0)),
                  pl.BlockSpec((1, D_MODEL, fc), lambda i, f, be, na: (be[blk(i, na)], 0, fidx(i, f, na))),
                  pl.BlockSpec((1, D_MODEL, fc), lambda i, f, be, na: (be[blk(i, na)], 0, fidx(i, f, na))),
                  pl.BlockSpec((1, fc, D_MODEL), lambda i, f, be, na: (be[blk(i, na)], fidx(i, f, na), 0))],
        out_specs=pl.BlockSpec((MOE_BLOCK, D_MODEL), lambda i, f, be, na: (i, 0)),
    )
    return pl.pallas_call(
        _expert_kernel,
        grid_spec=grid_spec,
        out_shape=jax.ShapeDtypeStruct((cap, D_MODEL), F32),
        compiler_params=_cparams(("arbitrary", "arbitrary")),
        name="moe_experts",
    )(block_expert, n_active, xs, w1, w3, w2)


COMBINE_TC = 256


def _combine_kernel(dest_ref, x_ref, gt_ref, ml_ref, ys_hbm, o_ref, buf_ref, sem):
    def issue(r, carry):
        for kk in range(TOP_K):
            _row_copy(ys_hbm, dest_ref[kk, r], buf_ref.at[kk], r, sem).start()
        return carry

    lax.fori_loop(0, COMBINE_TC, issue, 0)

    def drain(r, carry):
        for kk in range(TOP_K):
            _row_copy(ys_hbm, 0, buf_ref.at[kk], 0, sem).wait()
        return carry

    lax.fori_loop(0, COMBINE_TC, drain, 0)
    g = gt_ref[...]
    y = buf_ref[0] * g[:, 0:1] + buf_ref[1] * g[:, 1:2]
    o_ref[...] = x_ref[...] + ml_ref[0, 5:6, :] * y


def _combine_call(dest_t, x1_flat, gates, mods, ys):
    n = x1_flat.shape[0]
    tc = COMBINE_TC
    per_batch = SEQ // tc
    return pl.pallas_call(
        _combine_kernel,
        grid=(n // tc,),
        in_specs=[pl.BlockSpec((TOP_K, tc), lambda i: (0, i), memory_space=pltpu.SMEM),
                  pl.BlockSpec((tc, D_MODEL), lambda i: (i, 0)),
                  pl.BlockSpec((tc, TOP_K), lambda i: (i, 0)),
                  pl.BlockSpec((1, 6, D_MODEL), lambda i: (i // per_batch, 0, 0)),
                  pl.BlockSpec(memory_space=pl.ANY)],
        out_specs=pl.BlockSpec((tc, D_MODEL), lambda i: (i, 0)),
        out_shape=jax.ShapeDtypeStruct((n, D_MODEL), F32),
        scratch_shapes=[pltpu.VMEM((TOP_K, tc, D_MODEL), F32), pltpu.SemaphoreType.DMA(())],
        compiler_params=_cparams(("arbitrary",)),
        name="moe_combine",
    )(dest_t, x1_flat, gates, mods, ys)


def _moe_layer(x1, mods, norm_g, router_w, router_b, w1, w3, w2):
    bsz = x1.shape[0]
    n = bsz * SEQ
    cap = (n * TOP_K + MOE_BLOCK - 1) // MOE_BLOCK * MOE_BLOCK + N_EXPERTS * MOE_BLOCK
    rw_p = jnp.zeros((D_MODEL, LANES), F32).at[:, :N_EXPERTS].set(router_w)
    rb_p = jnp.zeros((1, LANES), F32).at[0, :N_EXPERTS].set(router_b)
    h2, e_idx, rank, gates, counts = _router_call(x1, mods, norm_g, rw_p, rb_p)
    counts = counts[0, :N_EXPERTS]
    padded = (counts + MOE_BLOCK - 1) // MOE_BLOCK * MOE_BLOCK
    padded_end = jnp.cumsum(padded)
    base = padded_end - padded
    dest_t = (base[e_idx] + rank).T.astype(I32)
    n_blocks = cap // MOE_BLOCK
    block_expert = jnp.minimum(
        jnp.searchsorted(padded_end, jnp.arange(n_blocks, dtype=I32) * MOE_BLOCK, side="right"),
        N_EXPERTS - 1).astype(I32)
    n_active = (padded_end[-1:] // MOE_BLOCK).astype(I32)
    xs = _dispatch_call(dest_t, h2, cap)
    ys = _expert_call(block_expert, n_active, xs, w1, w3, w2)
    out = _combine_call(dest_t, x1.reshape(n, D_MODEL), gates, mods, ys)
    return out.reshape(bsz, SEQ, D_MODEL)


def _rope_tables():
    rows = SEQ // GRID_W
    row = jnp.repeat(jnp.arange(rows), GRID_W).astype(F32)
    col = jnp.tile(jnp.arange(GRID_W), rows).astype(F32)
    n_freq = MLA_ROPE // 4
    inv = ROPE_BASE ** (-jnp.arange(n_freq, dtype=F32) / n_freq)
    ang = jnp.concatenate([row[:, None] * inv, col[:, None] * inv], axis=-1)
    cos, sin = jnp.cos(ang), jnp.sin(ang)
    half = MLA_ROPE // 2
    ones = jnp.ones((SEQ, MLA_NOPE), F32)
    pad = jnp.zeros((SEQ, LANES - MLA_QK), F32)
    zn = jnp.zeros((SEQ, MLA_NOPE), F32)
    zh = jnp.zeros((SEQ, half), F32)
    cos_t = jnp.concatenate([ones, cos, cos, pad], axis=-1)
    sin_a = jnp.concatenate([zn, -sin, zh, pad], axis=-1)
    sin_b = jnp.concatenate([zn, zh, sin, pad], axis=-1)
    ident = jnp.concatenate([jnp.ones((CTX_LEN, MLA_QK), F32), jnp.zeros((CTX_LEN, LANES - MLA_QK), F32)], axis=-1)
    zero = jnp.zeros((CTX_LEN, LANES), F32)
    return (jnp.concatenate([cos_t, ident], axis=0), jnp.concatenate([sin_a, zero], axis=0),
            jnp.concatenate([sin_b, zero], axis=0))


def _relayout_w_in(w_in):
    zeros = lambda n: jnp.zeros((D_MODEL, n), F32)
    ga = w_in[:, OFF_GA:OFF_CQ]
    h = MLSTM_HEADS
    gate_tile = lambda d: jnp.concatenate([ga[:, d * h:(d + 1) * h], ga[:, (2 + d) * h:(3 + d) * h],
                                           zeros(LANES - 2 * h)], axis=-1)
    return jnp.concatenate([
        w_in[:, 0:OFF_GA],
        w_in[:, OFF_CQ:OFF_KR],
        zeros(MLA_NOPE), w_in[:, OFF_KR:OFF_GM], zeros(LANES - MLA_QK),
        w_in[:, OFF_GM:IN_COLS],
        gate_tile(0), gate_tile(1)], axis=-1).astype(BF16)


def _relayout_gate_b(gb):
    h = MLSTM_HEADS
    tile = lambda d: jnp.concatenate([gb[d * h:(d + 1) * h], gb[(2 + d) * h:(3 + d) * h],
                                      jnp.zeros((LANES - 2 * h,), F32)])
    return jnp.stack([tile(0), tile(1)])[:, None, :]


def _pad_heads(w, width):
    kdim = w.shape[0]
    w = w.reshape(kdim, MLA_HEADS, width)
    return jnp.pad(w, ((0, 0), (0, 0), (0, HEAD_PAD - width))).reshape(kdim, MLA_HEADS * HEAD_PAD)


def _layer_params(l, w_in, w_out, mlstm_gate_b, mlstm_norm_g, mla_cq_g, mla_ckv_g, mla_w_uq, mla_w_ukv,
                  mla_q_g, mla_k_g, gmlp_ln_g, gmlp_ln_b, gmlp_w_s, gmlp_b_s):
    ukv = mla_w_ukv[l].reshape(MLA_KV_RANK, MLA_HEADS, MLA_NOPE + MLA_V)
    pad1 = lambda g: jnp.pad(g, (0, LANES - MLA_QK))[None, :]
    return dict(
        w_in=_relayout_w_in(w_in[l]),
        gate_b=_relayout_gate_b(mlstm_gate_b[l]),
        mlstm_g=mlstm_norm_g[l][None, :],
        cq_g=mla_cq_g[l][None, :], ckv_g=mla_ckv_g[l][None, :],
        w_uq=_pad_heads(mla_w_uq[l], MLA_QK).astype(BF16),
        w_k=_pad_heads(ukv[:, :, :MLA_NOPE].reshape(MLA_KV_RANK, -1), MLA_NOPE).astype(BF16),
        w_v=_pad_heads(ukv[:, :, MLA_NOPE:].reshape(MLA_KV_RANK, -1), MLA_V).astype(BF16),
        q_g=pad1(mla_q_g[l]), k_g=pad1(mla_k_g[l]),
        ln_g=gmlp_ln_g[l][None, :], ln_b=gmlp_ln_b[l][None, :],
        w_s=gmlp_w_s[l].astype(BF16),
        b_s=jnp.repeat(gmlp_b_s[l].T, GMLP_GROUP_DIM, axis=1),
        w_out=w_out[l].astype(BF16),
    )


def kernel(x, c, ctx, c_ctx, ada_w, ada_b, norm1_g, norm2_g, w_in, w_out, mlstm_gate_b, mlstm_norm_g, mla_cq_g, mla_ckv_g, mla_w_uq, mla_w_ukv, mla_q_g, mla_k_g, gmlp_ln_g, gmlp_ln_b, gmlp_w_s, gmlp_b_s, ffn_w1, ffn_w3, ffn_w2, moe_router_w, moe_router_b, moe_w1, moe_w3, moe_w2):
    bsz = x.shape[0]
    assert x.shape[1:] == (SEQ, D_MODEL) and ctx.shape[1:] == (CTX_LEN, D_MODEL)
    mod_rows = -(-(bsz + 1) // 8) * 8
    cvec = jnp.zeros((mod_rows, D_MODEL), F32).at[:bsz].set(c).at[bsz].set(c_ctx)
    mods_all = _ada_call(cvec, ada_w.astype(BF16), ada_b[:, None, :]).reshape(DEPTH, mod_rows, 6, D_MODEL)
    rope_tabs = _rope_tables()
    xall = jnp.concatenate([x, ctx], axis=1)
    for l in range(DEPTH):
        last = l == DEPTH - 1
        p = _layer_params(l, w_in, w_out, mlstm_gate_b, mlstm_norm_g, mla_cq_g, mla_ckv_g, mla_w_uq, mla_w_ukv,
                          mla_q_g, mla_k_g, gmlp_ln_g, gmlp_ln_b, gmlp_w_s, gmlp_b_s)
        mods = mods_all[l]
        z = _inproj_call(xall, mods, norm1_g[l][None, :], p["w_in"], bsz)
        a_mix = _mlstm_call(z, p["gate_b"], p["mlstm_g"])
        q, k, v = _qkv_call(z, p, rope_tabs)
        t_out = SEQ if last else T_ALL
        b_mix = _attn_call(q, k, v, t_out)
        x1 = _mixout_call(xall, z, a_mix, b_mix, mods, p, bsz, t_out, 1024 if last else 768)
        j = l // 2
        if l % 2 == 0:
            assert not last
            xall = _ffn_call(x1, mods, norm2_g[l][None, :], ffn_w1[j].astype(BF16), ffn_w3[j].astype(BF16),
                             ffn_w2[j].astype(BF16), bsz)
        else:
            assert last
            xall = _moe_layer(x1, mods, norm2_g[l][None, :], moe_router_w[j], moe_router_b[j],
                              moe_w1[j].astype(BF16), moe_w3[j].astype(BF16), moe_w2[j].astype(BF16))
    return xall
```

```python
import functools

import jax
import jax.numpy as jnp
from jax import lax
from jax.experimental import pallas as pl
from jax.experimental.pallas import tpu as pltpu

F32 = jnp.float32
BF16 = jnp.bfloat16
I32 = jnp.int32
HIGHEST = lax.Precision.HIGHEST

D_MODEL = 1024
SEQ = 2048
CTX_LEN = 256
T_ALL = SEQ + CTX_LEN
DEPTH = 2
GRID_W = 64
EPS = 1e-6
MLSTM_HEADS = 4
MLSTM_HEAD_DIM = 64
MLSTM_WIDTH = 256
CHUNK = 128
MLA_HEADS = 8
MLA_Q_RANK = 256
MLA_KV_RANK = 128
MLA_NOPE = 64
MLA_ROPE = 32
MLA_V = 64
MLA_QK = 96
MLA_WIDTH = 512
ROPE_BASE = 10000.0
GMLP_GROUPS = 4
GMLP_GROUP_DIM = 64
GMLP_WIDTH = 256
D_FF = 2816
N_EXPERTS = 8
TOP_K = 2
D_FF_EXPERT = 3584
MOE_BLOCK = 512
OFF_GA = 4 * MLSTM_WIDTH
OFF_CQ = OFF_GA + 4 * MLSTM_HEADS
OFF_CKV = OFF_CQ + MLA_Q_RANK
OFF_KR = OFF_CKV + MLA_KV_RANK
OFF_GM = OFF_KR + MLA_ROPE
IN_COLS = OFF_GM + 2 * GMLP_WIDTH

LANES = 128
HEAD_PAD = LANES
ZC_A = 0
ZC_B = 1024
ZC_GM = 1536
ZC_G = 2048
Z_COLS = ZC_G + 2 * LANES
VMEM_LIMIT = 56 * 1024 * 1024

N_CHUNKS = T_ALL // CHUNK
N_LAT_CHUNKS = SEQ // CHUNK


def _cparams(sem, vmem=VMEM_LIMIT):
    return pltpu.CompilerParams(dimension_semantics=sem, vmem_limit_bytes=vmem)


def _rms(x, g):
    return x * lax.rsqrt(jnp.mean(x * x, axis=-1, keepdims=True) + EPS) * g


def _silu(x):
    return x * jax.nn.sigmoid(x)


def _modulated_norm(x, g, shift, scale):
    return _rms(x, g) * (1.0 + scale) + shift


def _row_mods(ml_ref, mc_ref, first, tile_start, rows):
    row = tile_start + lax.broadcasted_iota(I32, (rows, 1), 0)
    is_ctx = row >= SEQ
    return tuple(jnp.where(is_ctx, mc_ref[0, first + k:first + k + 1, :], ml_ref[0, first + k:first + k + 1, :])
                 for k in range(3))


def _ada_kernel(c_ref, w_ref, b_ref, o_ref):
    s = _silu(c_ref[...]).astype(BF16)
    o_ref[0] = jnp.dot(s, w_ref[0], preferred_element_type=F32) + b_ref[0]


def _ada_call(cvec, ada_w, ada_b):
    rows = cvec.shape[0]
    return pl.pallas_call(
        _ada_kernel,
        grid=(DEPTH, 6),
        in_specs=[pl.BlockSpec((rows, D_MODEL), lambda l, j: (0, 0)),
                  pl.BlockSpec((1, D_MODEL, D_MODEL), lambda l, j: (l, 0, j)),
                  pl.BlockSpec((1, 1, D_MODEL), lambda l, j: (l, 0, j))],
        out_specs=pl.BlockSpec((1, rows, D_MODEL), lambda l, j: (l, 0, j)),
        out_shape=jax.ShapeDtypeStruct((DEPTH, rows, 6 * D_MODEL), F32),
        compiler_params=_cparams(("arbitrary", "arbitrary")),
        name="adaln",
    )(cvec, ada_w, ada_b)


def _inproj_kernel(x_ref, ml_ref, mc_ref, g_ref, w_ref, z_ref, *, tr):
    shift, scale, _ = _row_mods(ml_ref, mc_ref, 0, pl.program_id(1) * tr, tr)
    xn = _modulated_norm(x_ref[0], g_ref[...], shift, scale)
    z_ref[0] = jnp.dot(xn.astype(BF16), w_ref[...], preferred_element_type=F32)


def _inproj_call(xall, mods, norm_g, w_in_r, n_ctx_row):
    bsz = xall.shape[0]
    tr = 768
    return pl.pallas_call(
        functools.partial(_inproj_kernel, tr=tr),
        grid=(bsz, T_ALL // tr),
        in_specs=[pl.BlockSpec((1, tr, D_MODEL), lambda b, i: (b, i, 0)),
                  pl.BlockSpec((1, 6, D_MODEL), lambda b, i: (b, 0, 0)),
                  pl.BlockSpec((1, 6, D_MODEL), lambda b, i: (n_ctx_row, 0, 0)),
                  pl.BlockSpec((1, D_MODEL), lambda b, i: (0, 0)),
                  pl.BlockSpec((D_MODEL, Z_COLS), lambda b, i: (0, 0))],
        out_specs=pl.BlockSpec((1, tr, Z_COLS), lambda b, i: (b, i, 0)),
        out_shape=jax.ShapeDtypeStruct((bsz, T_ALL, Z_COLS), F32),
        compiler_params=_cparams(("parallel", "parallel")),
        name="in_proj",
    )(xall, mods, mods, norm_g, w_in_r)


def _scan_chunk(d, j):
    fwd_chunk = (j + N_LAT_CHUNKS) % N_CHUNKS
    return jnp.where(d == 0, fwd_chunk, N_CHUNKS - 1 - j)


MLSTM_ROWS_PER_STEP = 1


def _mlstm_kernel(za_ref, zg_ref, gb_ref, ng_ref, o_ref, cbd_ref, n_ref, m_ref, hf_ref):
    L = CHUNK
    E = MLSTM_HEAD_DIM
    W = MLSTM_WIDTH
    d = pl.program_id(1)
    j = pl.program_id(2)
    fwd = d == 0
    row0 = pl.multiple_of(_scan_chunk(d, j) * L, L)

    @pl.when(j == 0)
    def _():
        cbd_ref[...] = jnp.zeros_like(cbd_ref)
        n_ref[...] = jnp.zeros_like(n_ref)
        m_ref[...] = jnp.zeros_like(m_ref)

    r_i = lax.broadcasted_iota(I32, (L, L), 0)
    c_i = lax.broadcasted_iota(I32, (L, L), 1)
    prec = jnp.logical_or(jnp.logical_and(fwd, c_i <= r_i), jnp.logical_and(jnp.logical_not(fwd), c_i >= r_i))
    succ = jnp.logical_or(jnp.logical_and(fwd, r_i <= c_i), jnp.logical_and(jnp.logical_not(fwd), r_i >= c_i))

    rows = range(MLSTM_ROWS_PER_STEP)
    st = [_mlstm_first_matmuls(fwd, prec, succ, za_ref[bb, :, 0:3 * W], zg_ref[bb] + gb_ref[0],
                               cbd_ref[bb], n_ref[bb], m_ref[bb]) for bb in rows]
    for s in st:
        _mlstm_decay_weights(prec, s)
    for s in st:
        _mlstm_second_matmuls(s)
    for bb in rows:
        _mlstm_finalise(st[bb])
        cbd_ref[bb] = st[bb]["cbd_new"]
        n_ref[bb] = st[bb]["n_new"]
        m_ref[bb] = st[bb]["m_new_row"]

    @pl.when(fwd)
    def _():
        for bb in rows:
            for h in range(MLSTM_HEADS):
                hf_ref[bb, pl.ds(row0, L), h * E:(h + 1) * E] = st[bb]["h"][h]

    @pl.when(jnp.logical_not(fwd))
    def _():
        for bb in rows:
            for h in range(MLSTM_HEADS):
                sl = slice(h * E, (h + 1) * E)
                hs = hf_ref[bb, pl.ds(row0, L), sl] + st[bb]["h"][h]
                y = _rms(hs, ng_ref[:, sl])
                o_ref[bb, :, sl] = jax.nn.sigmoid(za_ref[bb, :, 3 * W + h * E:3 * W + (h + 1) * E]) * y


def _head_slices():
    return [slice(h * MLSTM_HEAD_DIM, (h + 1) * MLSTM_HEAD_DIM) for h in range(MLSTM_HEADS)]


def _mlstm_first_matmuls(fwd, prec, succ, qkv, gates, cbd_old, n_old, m_row):
    L = CHUNK
    W = MLSTM_WIDTH
    q = qkv[:, 0:W]
    k = qkv[:, W:2 * W] * (MLSTM_HEAD_DIM ** -0.5)
    qb = q.astype(BF16)
    kb = k.astype(BF16)
    vb = qkv[:, 2 * W:3 * W].astype(BF16)
    logf = jnp.minimum(gates, 0.0) - jnp.log1p(jnp.exp(-jnp.abs(gates)))
    b_cols = jnp.dot(prec.astype(F32), logf, precision=HIGHEST, preferred_element_type=F32)
    b_rows = jnp.dot(logf.T, succ.astype(F32), precision=HIGHEST, preferred_element_type=F32)
    sqk = [lax.dot_general(qb[:, sl], kb[:, sl], (((1,), (1,)), ((), ())), preferred_element_type=F32)
           for sl in _head_slices()]
    return dict(
        q=q, k=k, qb=qb, vb=vb, gates=gates, gates_t=gates.T, b_cols=b_cols, b_rows=b_rows,
        b_last_all=jnp.where(fwd, b_cols[L - 1:L, :], b_cols[0:1, :]),
        cbd_old=cbd_old, n_old=n_old, m_row=m_row, sqk=sqk,
        qc_all=jnp.dot(qb, cbd_old.astype(BF16), preferred_element_type=F32))


def _mlstm_decay_weights(prec, s):
    L = CHUNK
    W = MLSTM_WIDTH
    lane_head = lax.broadcasted_iota(I32, (1, W), 1) // MLSTM_HEAD_DIM
    m_lane = lax.broadcasted_iota(I32, (1, LANES), 1)
    w_full = jnp.zeros((L, W), F32)
    dec_row = jnp.zeros((1, W), F32)
    m_new_row = s["m_row"]
    s["p"], s["a"], s["m_t"], s["den_s"] = [], [], [], []
    for h in range(MLSTM_HEADS):
        b_c = s["b_cols"][:, 4 + h:5 + h]
        b_r = s["b_rows"][4 + h:5 + h, :]
        i_c = s["gates"][:, h:h + 1]
        i_r = s["gates_t"][h:h + 1, :]
        b_last = s["b_last_all"][:, 4 + h:5 + h]
        m_old = s["m_row"][:, h:h + 1]
        g_r = b_last - b_r + i_r
        g_c = b_last - b_c + i_c
        m_new = jnp.maximum(b_last + m_old, jnp.max(g_r, axis=-1, keepdims=True))
        w_full = jnp.where(lane_head == h, jnp.exp(g_c - m_new), w_full)
        dec_row = jnp.where(lane_head == h, jnp.exp(b_last + m_old - m_new), dec_row)
        m_new_row = jnp.where(m_lane == h, m_new, m_new_row)
        dmat = jnp.where(prec, b_c - b_r + i_r, -jnp.inf)
        inter = b_c + m_old
        m_t = jnp.maximum(inter, jnp.max(dmat, axis=-1, keepdims=True))
        p = s["sqk"][h] * jnp.exp(dmat - m_t)
        s["p"].append(p.astype(BF16))
        s["den_s"].append(jnp.sum(p, axis=-1, keepdims=True))
        s["a"].append(jnp.exp(inter - m_t))
        s["m_t"].append(m_t)
    s["kw"] = s["k"] * w_full
    s["dec_row"] = dec_row
    s["m_new_row"] = m_new_row


def _mlstm_second_matmuls(s):
    s["pv"] = [jnp.dot(s["p"][h], s["vb"][:, sl], preferred_element_type=F32)
               for h, sl in enumerate(_head_slices())]
    s["upd"] = jnp.dot(s["kw"].T.astype(BF16), s["vb"], preferred_element_type=F32)


def _mlstm_finalise(s):
    W = MLSTM_WIDTH
    E = MLSTM_HEAD_DIM
    qn_all = s["q"] * s["n_old"]
    s["h"] = []
    for h, sl in enumerate(_head_slices()):
        a = s["a"][h]
        num = s["pv"][h] + a * s["qc_all"][:, sl]
        den = s["den_s"][h] + a * jnp.sum(qn_all[:, sl], axis=-1, keepdims=True)
        s["h"].append(num / jnp.maximum(jnp.abs(den), jnp.exp(-s["m_t"][h])))
    rh = lax.broadcasted_iota(I32, (W, W), 0) // E
    ch = lax.broadcasted_iota(I32, (W, W), 1) // E
    row_head = lax.broadcasted_iota(I32, (W, 1), 0) // E
    dec_col = jnp.zeros((W, 1), F32)
    for h in range(MLSTM_HEADS):
        dec_col = jnp.where(row_head == h, s["dec_row"][:, h * E:h * E + 1], dec_col)
    s["cbd_new"] = dec_col * s["cbd_old"] + jnp.where(rh == ch, s["upd"], 0.0)
    s["n_new"] = s["dec_row"] * s["n_old"] + jnp.sum(s["kw"], axis=0, keepdims=True)


def _mlstm_call(z, gate_b, norm_g):
    bsz = z.shape[0]
    nb = MLSTM_ROWS_PER_STEP
    assert bsz % nb == 0
    g_blk = ZC_G // LANES
    last = N_CHUNKS - 1
    return pl.pallas_call(
        _mlstm_kernel,
        grid=(bsz // nb, 2, N_CHUNKS),
        in_specs=[pl.BlockSpec((nb, CHUNK, 4 * MLSTM_WIDTH), lambda b, d, j: (b, _scan_chunk(d, j), 0)),
                  pl.BlockSpec((nb, CHUNK, LANES), lambda b, d, j: (b, _scan_chunk(d, j), g_blk + d)),
                  pl.BlockSpec((1, 1, LANES), lambda b, d, j: (d, 0, 0)),
                  pl.BlockSpec((1, MLSTM_WIDTH), lambda b, d, j: (0, 0))],
        out_specs=pl.BlockSpec((nb, CHUNK, MLSTM_WIDTH),
                               lambda b, d, j: (b, jnp.where(d == 0, last, last - j), 0)),
        out_shape=jax.ShapeDtypeStruct((bsz, T_ALL, MLSTM_WIDTH), F32),
        scratch_shapes=[pltpu.VMEM((nb, MLSTM_WIDTH, MLSTM_WIDTH), F32),
                        pltpu.VMEM((nb, 1, MLSTM_WIDTH), F32),
                        pltpu.VMEM((nb, 1, LANES), F32),
                        pltpu.VMEM((nb, T_ALL, MLSTM_WIDTH), F32)],
        compiler_params=_cparams(("parallel", "arbitrary", "arbitrary")),
        name="mlstm",
    )(z, z, gate_b, norm_g)


def _rope(t, cos_t, sin_a, sin_b):
    return t * cos_t + pltpu.roll(t, LANES - 16, 1) * sin_a + pltpu.roll(t, 16, 1) * sin_b


def _head_rms(t, g):
    return t * lax.rsqrt(jnp.sum(t * t, axis=-1, keepdims=True) * (1.0 / MLA_QK) + EPS) * g


def _qkv_kernel(z_ref, cqg_ref, ckvg_ref, wq_ref, wk_ref, wv_ref, qg_ref, kg_ref, ct_ref, sa_ref, sb_ref,
                q_ref, k_ref, v_ref, *, tr):
    cqn = _rms(z_ref[0, :, 0:MLA_Q_RANK], cqg_ref[...]).astype(BF16)
    ckvn = _rms(z_ref[0, :, MLA_Q_RANK:MLA_Q_RANK + MLA_KV_RANK], ckvg_ref[...]).astype(BF16)
    k_rope = z_ref[0, :, MLA_Q_RANK + MLA_KV_RANK:MLA_Q_RANK + MLA_KV_RANK + LANES]
    q_raw = jnp.dot(cqn, wq_ref[...], preferred_element_type=F32)
    k_raw = jnp.dot(ckvn, wk_ref[...], preferred_element_type=F32)
    v_raw = jnp.dot(ckvn, wv_ref[...], preferred_element_type=F32)
    cos_t = ct_ref[...]
    sin_a = sa_ref[...]
    sin_b = sb_ref[...]
    lane = lax.broadcasted_iota(I32, (tr, LANES), 1)
    for h in range(MLA_HEADS):
        sl = slice(h * HEAD_PAD, (h + 1) * HEAD_PAD)
        qh = _rope(_head_rms(q_raw[:, sl], qg_ref[...]), cos_t, sin_a, sin_b)
        q_ref[0, :, sl] = (qh * (MLA_QK ** -0.5)).astype(BF16)
        kh = _rope(_head_rms(k_raw[:, sl] + k_rope, kg_ref[...]), cos_t, sin_a, sin_b)
        k_ref[0, :, sl] = kh.astype(BF16)
        v_ref[0, :, sl] = jnp.where(lane == MLA_V, 1.0, v_raw[:, sl]).astype(BF16)


def _qkv_call(z, p, rope_tabs):
    bsz = z.shape[0]
    tr = 768
    hw = MLA_HEADS * HEAD_PAD
    const = lambda b, i: (0, 0)
    out = jax.ShapeDtypeStruct((bsz, T_ALL, hw), BF16)
    tab = pl.BlockSpec((tr, LANES), lambda b, i: (i, 0))
    ospec = pl.BlockSpec((1, tr, hw), lambda b, i: (b, i, 0))
    return pl.pallas_call(
        functools.partial(_qkv_kernel, tr=tr),
        grid=(bsz, T_ALL // tr),
        in_specs=[pl.BlockSpec((1, tr, 512), lambda b, i: (b, i, ZC_B // 512)),
                  pl.BlockSpec((1, MLA_Q_RANK), const), pl.BlockSpec((1, MLA_KV_RANK), const),
                  pl.BlockSpec((MLA_Q_RANK, hw), const), pl.BlockSpec((MLA_KV_RANK, hw), const),
                  pl.BlockSpec((MLA_KV_RANK, hw), const),
                  pl.BlockSpec((1, LANES), const), pl.BlockSpec((1, LANES), const),
                  tab, tab, tab],
        out_specs=[ospec, ospec, ospec],
        out_shape=[out, out, out],
        compiler_params=_cparams(("parallel", "parallel")),
        name="mla_qkv",
    )(z, p["cq_g"], p["ckv_g"], p["w_uq"], p["w_k"], p["w_v"], p["q_g"], p["k_g"], *rope_tabs)


ATT_TQ = 256
ATT_HEADS_PER_STEP = 2


def _attn_kernel(q_ref, k_ref, v_ref, o_ref):
    i = pl.program_id(2)

    def run(k0):
        for hh in range(ATT_HEADS_PER_STEP):
            sl = slice(hh * HEAD_PAD, (hh + 1) * HEAD_PAD)
            s = lax.dot_general(q_ref[0, :, sl], k_ref[0, k0:T_ALL, sl], (((1,), (1,)), ((), ())),
                                preferred_element_type=F32)
            p = jnp.exp(s - jnp.max(s, axis=-1, keepdims=True)).astype(BF16)
            oe = jnp.dot(p, v_ref[0, k0:T_ALL, sl], preferred_element_type=F32)
            o_ref[0, :, hh * MLA_V:(hh + 1) * MLA_V] = oe[:, 0:MLA_V] / oe[:, MLA_V:MLA_V + 1]

    @pl.when(i < SEQ // ATT_TQ)
    def _():
        run(0)

    @pl.when(i >= SEQ // ATT_TQ)
    def _():
        run(SEQ)


def _attn_call(q, k, v, t_out):
    bsz = q.shape[0]
    w = ATT_HEADS_PER_STEP * HEAD_PAD
    return pl.pallas_call(
        _attn_kernel,
        grid=(bsz, MLA_HEADS // ATT_HEADS_PER_STEP, t_out // ATT_TQ),
        in_specs=[pl.BlockSpec((1, ATT_TQ, w), lambda b, h, i: (b, i, h)),
                  pl.BlockSpec((1, T_ALL, w), lambda b, h, i: (b, 0, h)),
                  pl.BlockSpec((1, T_ALL, w), lambda b, h, i: (b, 0, h))],
        out_specs=pl.BlockSpec((1, ATT_TQ, ATT_HEADS_PER_STEP * MLA_V), lambda b, h, i: (b, i, h)),
        out_shape=jax.ShapeDtypeStruct((bsz, t_out, MLA_WIDTH), F32),
        compiler_params=_cparams(("parallel", "parallel", "arbitrary")),
        name="mla_attn",
    )(q, k, v)


def _mixout_kernel(x_ref, zg_ref, a_ref, b_ref, ml_ref, mc_ref, lng_ref, lnb_ref, ws_ref, bs_ref, wo_ref,
                   o_ref, cm_ref, *, tr):
    gd = GMLP_GROUP_DIM
    act = jax.nn.gelu(zg_ref[0])
    u = act[:, 0:GMLP_WIDTH]
    vv = act[:, GMLP_WIDTH:2 * GMLP_WIDTH]
    mu = jnp.mean(vv, axis=-1, keepdims=True)
    var = jnp.mean(jnp.square(vv - mu), axis=-1, keepdims=True)
    vn = ((vv - mu) * lax.rsqrt(var + EPS) * lng_ref[...] + lnb_ref[...]).astype(BF16)
    for c in range(tr // CHUNK):
        rows = slice(c * CHUNK, (c + 1) * CHUNK)
        for g in range(GMLP_GROUPS):
            cols = slice(g * gd, (g + 1) * gd)
            sv = jnp.dot(ws_ref[g], vn[rows, cols], preferred_element_type=F32) + bs_ref[:, cols]
            cm_ref[rows, cols] = u[rows, cols] * sv
    y = jnp.dot(a_ref[0].astype(BF16), wo_ref[0:MLSTM_WIDTH, :], preferred_element_type=F32)
    y += jnp.dot(b_ref[0].astype(BF16), wo_ref[MLSTM_WIDTH:MLSTM_WIDTH + MLA_WIDTH, :],
                 preferred_element_type=F32)
    y += jnp.dot(cm_ref[...].astype(BF16), wo_ref[MLSTM_WIDTH + MLA_WIDTH:, :], preferred_element_type=F32)
    _, _, gate = _row_mods(ml_ref, mc_ref, 0, pl.program_id(1) * tr, tr)
    o_ref[0] = x_ref[0] + gate * y


def _mixout_call(xall, z, a_mix, b_mix, mods, p, n_ctx_row, t_out, tr):
    bsz = xall.shape[0]
    const = lambda b, i: (0, 0)
    return pl.pallas_call(
        functools.partial(_mixout_kernel, tr=tr),
        grid=(bsz, t_out // tr),
        in_specs=[pl.BlockSpec((1, tr, D_MODEL), lambda b, i: (b, i, 0)),
                  pl.BlockSpec((1, tr, 2 * GMLP_WIDTH), lambda b, i: (b, i, ZC_GM // (2 * GMLP_WIDTH))),
                  pl.BlockSpec((1, tr, MLSTM_WIDTH), lambda b, i: (b, i, 0)),
                  pl.BlockSpec((1, tr, MLA_WIDTH), lambda b, i: (b, i, 0)),
                  pl.BlockSpec((1, 6, D_MODEL), lambda b, i: (b, 0, 0)),
                  pl.BlockSpec((1, 6, D_MODEL), lambda b, i: (n_ctx_row, 0, 0)),
                  pl.BlockSpec((1, GMLP_WIDTH), const), pl.BlockSpec((1, GMLP_WIDTH), const),
                  pl.BlockSpec((GMLP_GROUPS, CHUNK, CHUNK), lambda b, i: (0, 0, 0)),
                  pl.BlockSpec((CHUNK, GMLP_WIDTH), const),
                  pl.BlockSpec((D_MODEL, D_MODEL), const)],
        out_specs=pl.BlockSpec((1, tr, D_MODEL), lambda b, i: (b, i, 0)),
        out_shape=jax.ShapeDtypeStruct((bsz, t_out, D_MODEL), F32),
        scratch_shapes=[pltpu.VMEM((tr, GMLP_WIDTH), F32)],
        compiler_params=_cparams(("parallel", "parallel")),
        name="mix_out",
    )(xall, z, a_mix, b_mix, mods, mods, p["ln_g"], p["ln_b"], p["w_s"], p["b_s"], p["w_out"])


FFN_SPLIT = 2


def _ffn_kernel(x_ref, ml_ref, mc_ref, g_ref, w1_ref, w3_ref, w2_ref, o_ref, *, tr):
    shift, scale, gate = _row_mods(ml_ref, mc_ref, 3, pl.program_id(1) * tr, tr)
    x = x_ref[0]
    h = _modulated_norm(x, g_ref[...], shift, scale).astype(BF16)
    fc = D_FF // FFN_SPLIT
    y = jnp.zeros((tr, D_MODEL), F32)
    for f in range(FFN_SPLIT):
        cols = slice(f * fc, (f + 1) * fc)
        h1 = jnp.dot(h, w1_ref[:, cols], preferred_element_type=F32)
        h3 = jnp.dot(h, w3_ref[:, cols], preferred_element_type=F32)
        y += jnp.dot((_silu(h1) * h3).astype(BF16), w2_ref[cols, :], preferred_element_type=F32)
    o_ref[0] = x + gate * y


def _ffn_call(x1, mods, norm_g, w1, w3, w2, n_ctx_row):
    bsz = x1.shape[0]
    tr = 768
    const = lambda b, i: (0, 0)
    resident = pl.Buffered(1)
    return pl.pallas_call(
        functools.partial(_ffn_kernel, tr=tr),
        grid=(bsz, T_ALL // tr),
        in_specs=[pl.BlockSpec((1, tr, D_MODEL), lambda b, i: (b, i, 0)),
                  pl.BlockSpec((1, 6, D_MODEL), lambda b, i: (b, 0, 0)),
                  pl.BlockSpec((1, 6, D_MODEL), lambda b, i: (n_ctx_row, 0, 0)),
                  pl.BlockSpec((1, D_MODEL), const),
                  pl.BlockSpec((D_MODEL, D_FF), const, pipeline_mode=resident),
                  pl.BlockSpec((D_MODEL, D_FF), const, pipeline_mode=resident),
                  pl.BlockSpec((D_FF, D_MODEL), const, pipeline_mode=resident)],
        out_specs=pl.BlockSpec((1, tr, D_MODEL), lambda b, i: (b, i, 0)),
        out_shape=jax.ShapeDtypeStruct((bsz, T_ALL, D_MODEL), F32),
        compiler_params=_cparams(("parallel", "parallel")),
        name="dense_ffn",
    )(x1, mods, mods, norm_g, w1, w3, w2)


ROUTE_TR = 1024


def _router_kernel(x_ref, ml_ref, g_ref, rw_ref, rb_ref, h_ref, e_ref, r_ref, gt_ref, cnt_ref, run_ref, *, tr):
    step = pl.program_id(0) * pl.num_programs(1) + pl.program_id(1)

    @pl.when(step == 0)
    def _():
        run_ref[...] = jnp.zeros_like(run_ref)

    h = _modulated_norm(x_ref[0], g_ref[...], ml_ref[0, 3:4, :], ml_ref[0, 4:5, :])
    h_ref[...] = h
    lane = lax.broadcasted_iota(I32, (tr, LANES), 1)
    logits = jnp.dot(h, rw_ref[...], precision=HIGHEST, preferred_element_type=F32) + rb_ref[...]
    logits = jnp.where(lane < N_EXPERTS, logits, -jnp.inf)
    m1 = jnp.max(logits, axis=-1, keepdims=True)
    e1 = jnp.min(jnp.where(logits == m1, lane, LANES), axis=-1, keepdims=True)
    rest = jnp.where(lane == e1, -jnp.inf, logits)
    m2 = jnp.max(rest, axis=-1, keepdims=True)
    e2 = jnp.min(jnp.where(rest == m2, lane, LANES), axis=-1, keepdims=True)
    ex = jnp.exp(m2 - m1)
    g1 = 1.0 / (1.0 + ex)
    g2 = ex / (1.0 + ex)
    onehot = jnp.logical_or(lane == e1, lane == e2 + N_EXPERTS)
    oh = onehot.astype(F32)
    r_i = lax.broadcasted_iota(I32, (tr, tr), 0)
    c_i = lax.broadcasted_iota(I32, (tr, tr), 1)
    before = jnp.dot((c_i < r_i).astype(BF16), oh.astype(BF16), preferred_element_type=F32)
    tot = jnp.sum(oh, axis=0, keepdims=True)
    tot0_shift = pltpu.roll(tot, N_EXPERTS, 1)
    run = run_ref[...]
    first_half = lax.broadcasted_iota(I32, (1, LANES), 1) < N_EXPERTS
    offs = run + jnp.where(first_half, 0.0, tot0_shift)
    ranks = oh * (before + offs)
    rank1 = jnp.sum(jnp.where(lane < N_EXPERTS, ranks, 0.0), axis=-1, keepdims=True)
    rank2 = jnp.sum(jnp.where(lane >= N_EXPERTS, ranks, 0.0), axis=-1, keepdims=True)
    col = lax.broadcasted_iota(I32, (tr, TOP_K), 1)
    e_ref[...] = jnp.where(col == 0, e1, e2)
    r_ref[...] = jnp.where(col == 0, rank1, rank2).astype(I32)
    gt_ref[...] = jnp.where(col == 0, g1, g2)
    both = tot + jnp.where(first_half, pltpu.roll(tot, LANES - N_EXPERTS, 1), tot0_shift)
    new_run = run + both
    run_ref[...] = new_run
    cnt_ref[...] = new_run.astype(I32)


def _router_call(x1, mods, norm_g, rw_p, rb_p):
    bsz = x1.shape[0]
    tr = ROUTE_TR
    n = bsz * SEQ
    nt = SEQ // tr
    const = lambda b, i: (0, 0)
    tok = lambda b, i: (b * nt + i, 0)
    return pl.pallas_call(
        functools.partial(_router_kernel, tr=tr),
        grid=(bsz, nt),
        in_specs=[pl.BlockSpec((1, tr, D_MODEL), lambda b, i: (b, i, 0)),
                  pl.BlockSpec((1, 6, D_MODEL), lambda b, i: (b, 0, 0)),
                  pl.BlockSpec((1, D_MODEL), const),
                  pl.BlockSpec((D_MODEL, LANES), const),
                  pl.BlockSpec((1, LANES), const)],
        out_specs=[pl.BlockSpec((tr, D_MODEL), tok),
                   pl.BlockSpec((tr, TOP_K), tok), pl.BlockSpec((tr, TOP_K), tok), pl.BlockSpec((tr, TOP_K), tok),
                   pl.BlockSpec((1, LANES), const)],
        out_shape=[jax.ShapeDtypeStruct((n, D_MODEL), F32),
                   jax.ShapeDtypeStruct((n, TOP_K), I32), jax.ShapeDtypeStruct((n, TOP_K), I32),
                   jax.ShapeDtypeStruct((n, TOP_K), F32),
                   jax.ShapeDtypeStruct((1, LANES), I32)],
        scratch_shapes=[pltpu.VMEM((1, LANES), F32)],
        compiler_params=_cparams(("arbitrary", "arbitrary")),
        name="moe_router",
    )(x1, mods, norm_g, rw_p, rb_p)


DISPATCH_TD = 512


def _row_copy(src, src_row, dst, dst_row, sem):
    return pltpu.make_async_copy(src.at[pl.ds(src_row, 1)], dst.at[pl.ds(dst_row, 1)], sem)


def _dispatch_kernel(dest_ref, h_ref, xs_in, xs_out, sem):
    del xs_in

    def issue(r, carry):
        for kk in range(TOP_K):
            _row_copy(h_ref, r, xs_out, dest_ref[kk, r], sem).start()
        return carry

    lax.fori_loop(0, DISPATCH_TD, issue, 0)

    def drain(r, carry):
        for kk in range(TOP_K):
            _row_copy(h_ref, 0, xs_out, 0, sem).wait()
        return carry

    lax.fori_loop(0, DISPATCH_TD, drain, 0)


def _dispatch_call(dest_t, h2, cap):
    n = h2.shape[0]
    xs0 = jnp.zeros((cap, D_MODEL), F32)
    return pl.pallas_call(
        _dispatch_kernel,
        grid=(n // DISPATCH_TD,),
        in_specs=[pl.BlockSpec((TOP_K, DISPATCH_TD), lambda i: (0, i), memory_space=pltpu.SMEM),
                  pl.BlockSpec((DISPATCH_TD, D_MODEL), lambda i: (i, 0)),
                  pl.BlockSpec(memory_space=pl.ANY)],
        out_specs=pl.BlockSpec(memory_space=pl.ANY),
        out_shape=jax.ShapeDtypeStruct((cap, D_MODEL), F32),
        scratch_shapes=[pltpu.SemaphoreType.DMA(())],
        input_output_aliases={2: 0},
        compiler_params=_cparams(("arbitrary",)),
        name="moe_dispatch",
    )(dest_t, h2, xs0)


EXPERT_FSPLIT = 2


def _expert_kernel(be_ref, na_ref, xs_ref, w1_ref, w3_ref, w2_ref, ys_ref):
    i = pl.program_id(0)
    f = pl.program_id(1)
    active = i < na_ref[0]

    @pl.when(jnp.logical_and(active, f == 0))
    def _():
        ys_ref[...] = jnp.zeros_like(ys_ref)

    @pl.when(active)
    def _():
        x = xs_ref[...].astype(BF16)
        h1 = jnp.dot(x, w1_ref[0], preferred_element_type=F32)
        h3 = jnp.dot(x, w3_ref[0], preferred_element_type=F32)
        ys_ref[...] += jnp.dot((_silu(h1) * h3).astype(BF16), w2_ref[0], preferred_element_type=F32)

    @pl.when(jnp.logical_not(active))
    def _():
        ys_ref[...] = jnp.zeros_like(ys_ref)


def _expert_call(block_expert, n_active, xs, w1, w3, w2):
    cap = xs.shape[0]
    nb = cap // MOE_BLOCK
    fc = D_FF_EXPERT // EXPERT_FSPLIT
    last_f = EXPERT_FSPLIT - 1

    def blk(i, na):
        return jnp.minimum(i, na[0] - 1)

    def fidx(i, f, na):
        return jnp.where(i < na[0], f, last_f)

    grid_spec = pltpu.PrefetchScalarGridSpec(
        num_scalar_prefetch=2,
        grid=(nb, EXPERT_FSPLIT),
        in_specs=[pl.BlockSpec((MOE_BLOCK, D_MODEL), lambda i, f, be, na: (blk(i, na), 0)),
                  pl.BlockSpec((1, D_MODEL, fc), lambda i, f, be, na: (be[blk(i, na)], 0, fidx(i, f, na))),
                  pl.BlockSpec((1, D_MODEL, fc), lambda i, f, be, na: (be[blk(i, na)], 0, fidx(i, f, na))),
                  pl.BlockSpec((1, fc, D_MODEL), lambda i, f, be, na: (be[blk(i, na)], fidx(i, f, na), 0))],
        out_specs=pl.BlockSpec((MOE_BLOCK, D_MODEL), lambda i, f, be, na: (i, 0)),
    )
    return pl.pallas_call(
        _expert_kernel,
        grid_spec=grid_spec,
        out_shape=jax.ShapeDtypeStruct((cap, D_MODEL), F32),
        compiler_params=_cparams(("arbitrary", "arbitrary")),
        name="moe_experts",
    )(block_expert, n_active, xs, w1, w3, w2)


COMBINE_TC = 256


def _combine_kernel(dest_ref, dest_next_ref, x_ref, gt_ref, ml_ref, ys_hbm, o_ref, buf_ref, sem):
    i = pl.program_id(0)
    slot = i % 2

    def gather(d_ref, s):
        def issue(r, carry):
            for kk in range(TOP_K):
                _row_copy(ys_hbm, d_ref[kk, r], buf_ref.at[s, kk], r, sem.at[s]).start()
            return carry

        lax.fori_loop(0, COMBINE_TC, issue, 0)

    @pl.when(i == 0)
    def _():
        gather(dest_ref, 0)

    @pl.when(i + 1 < pl.num_programs(0))
    def _():
        gather(dest_next_ref, 1 - slot)

    def drain(r, carry):
        for kk in range(TOP_K):
            _row_copy(ys_hbm, 0, buf_ref.at[slot, kk], 0, sem.at[slot]).wait()
        return carry

    lax.fori_loop(0, COMBINE_TC, drain, 0)
    g = gt_ref[...]
    y = buf_ref[slot, 0] * g[:, 0:1] + buf_ref[slot, 1] * g[:, 1:2]
    o_ref[...] = x_ref[...] + ml_ref[0, 5:6, :] * y


def _combine_call(dest_t, x1_flat, gates, mods, ys):
    n = x1_flat.shape[0]
    tc = COMBINE_TC
    per_batch = SEQ // tc
    last = n // tc - 1
    return pl.pallas_call(
        _combine_kernel,
        grid=(n // tc,),
        in_specs=[pl.BlockSpec((TOP_K, tc), lambda i: (0, i), memory_space=pltpu.SMEM),
                  pl.BlockSpec((TOP_K, tc), lambda i: (0, jnp.minimum(i + 1, last)), memory_space=pltpu.SMEM),
                  pl.BlockSpec((tc, D_MODEL), lambda i: (i, 0)),
                  pl.BlockSpec((tc, TOP_K), lambda i: (i, 0)),
                  pl.BlockSpec((1, 6, D_MODEL), lambda i: (i // per_batch, 0, 0)),
                  pl.BlockSpec(memory_space=pl.ANY)],
        out_specs=pl.BlockSpec((tc, D_MODEL), lambda i: (i, 0)),
        out_shape=jax.ShapeDtypeStruct((n, D_MODEL), F32),
        scratch_shapes=[pltpu.VMEM((2, TOP_K, tc, D_MODEL), F32), pltpu.SemaphoreType.DMA((2,))],
        compiler_params=_cparams(("arbitrary",)),
        name="moe_combine",
    )(dest_t, dest_t, x1_flat, gates, mods, ys)


def _moe_layer(x1, mods, norm_g, router_w, router_b, w1, w3, w2):
    bsz = x1.shape[0]
    n = bsz * SEQ
    cap = (n * TOP_K + MOE_BLOCK - 1) // MOE_BLOCK * MOE_BLOCK + N_EXPERTS * MOE_BLOCK
    rw_p = jnp.zeros((D_MODEL, LANES), F32).at[:, :N_EXPERTS].set(router_w)
    rb_p = jnp.zeros((1, LANES), F32).at[0, :N_EXPERTS].set(router_b)
    h2, e_idx, rank, gates, counts = _router_call(x1, mods, norm_g, rw_p, rb_p)
    counts = counts[0, :N_EXPERTS]
    padded = (counts + MOE_BLOCK - 1) // MOE_BLOCK * MOE_BLOCK
    padded_end = jnp.cumsum(padded)
    base = padded_end - padded
    dest_t = (base[e_idx] + rank).T.astype(I32)
    n_blocks = cap // MOE_BLOCK
    block_start = jnp.arange(n_blocks, dtype=I32) * MOE_BLOCK
    block_expert = jnp.minimum(jnp.sum(padded_end[None, :] <= block_start[:, None], axis=1),
                               N_EXPERTS - 1).astype(I32)
    n_active = (padded_end[-1:] // MOE_BLOCK).astype(I32)
    xs = _dispatch_call(dest_t, h2, cap)
    ys = _expert_call(block_expert, n_active, xs, w1, w3, w2)
    out = _combine_call(dest_t, x1.reshape(n, D_MODEL), gates, mods, ys)
    return out.reshape(bsz, SEQ, D_MODEL)


def _rope_tables():
    rows = SEQ // GRID_W
    row = jnp.repeat(jnp.arange(rows), GRID_W).astype(F32)
    col = jnp.tile(jnp.arange(GRID_W), rows).astype(F32)
    n_freq = MLA_ROPE // 4
    inv = ROPE_BASE ** (-jnp.arange(n_freq, dtype=F32) / n_freq)
    ang = jnp.concatenate([row[:, None] * inv, col[:, None] * inv], axis=-1)
    cos, sin = jnp.cos(ang), jnp.sin(ang)
    half = MLA_ROPE // 2
    ones = jnp.ones((SEQ, MLA_NOPE), F32)
    pad = jnp.zeros((SEQ, LANES - MLA_QK), F32)
    zn = jnp.zeros((SEQ, MLA_NOPE), F32)
    zh = jnp.zeros((SEQ, half), F32)
    cos_t = jnp.concatenate([ones, cos, cos, pad], axis=-1)
    sin_a = jnp.concatenate([zn, -sin, zh, pad], axis=-1)
    sin_b = jnp.concatenate([zn, zh, sin, pad], axis=-1)
    ident = jnp.concatenate([jnp.ones((CTX_LEN, MLA_QK), F32), jnp.zeros((CTX_LEN, LANES - MLA_QK), F32)], axis=-1)
    zero = jnp.zeros((CTX_LEN, LANES), F32)
    return (jnp.concatenate([cos_t, ident], axis=0), jnp.concatenate([sin_a, zero], axis=0),
            jnp.concatenate([sin_b, zero], axis=0))


def _relayout_w_in(w_in):
    zeros = lambda n: jnp.zeros((D_MODEL, n), F32)
    ga = w_in[:, OFF_GA:OFF_CQ]
    h = MLSTM_HEADS
    gate_tile = lambda d: jnp.concatenate([ga[:, d * h:(d + 1) * h], ga[:, (2 + d) * h:(3 + d) * h],
                                           zeros(LANES - 2 * h)], axis=-1)
    return jnp.concatenate([
        w_in[:, 0:OFF_GA],
        w_in[:, OFF_CQ:OFF_KR],
        zeros(MLA_NOPE), w_in[:, OFF_KR:OFF_GM], zeros(LANES - MLA_QK),
        w_in[:, OFF_GM:IN_COLS],
        gate_tile(0), gate_tile(1)], axis=-1).astype(BF16)


def _relayout_gate_b(gb):
    h = MLSTM_HEADS
    tile = lambda d: jnp.concatenate([gb[d * h:(d + 1) * h], gb[(2 + d) * h:(3 + d) * h],
                                      jnp.zeros((LANES - 2 * h,), F32)])
    return jnp.stack([tile(0), tile(1)])[:, None, :]


def _pad_heads(w, width):
    kdim = w.shape[0]
    w = w.reshape(kdim, MLA_HEADS, width)
    return jnp.pad(w, ((0, 0), (0, 0), (0, HEAD_PAD - width))).reshape(kdim, MLA_HEADS * HEAD_PAD)


def _layer_params(l, w_in, w_out, mlstm_gate_b, mlstm_norm_g, mla_cq_g, mla_ckv_g, mla_w_uq, mla_w_ukv,
                  mla_q_g, mla_k_g, gmlp_ln_g, gmlp_ln_b, gmlp_w_s, gmlp_b_s):
    ukv = mla_w_ukv[l].reshape(MLA_KV_RANK, MLA_HEADS, MLA_NOPE + MLA_V)
    pad1 = lambda g: jnp.pad(g, (0, LANES - MLA_QK))[None, :]
    return dict(
        w_in=_relayout_w_in(w_in[l]),
        gate_b=_relayout_gate_b(mlstm_gate_b[l]),
        mlstm_g=mlstm_norm_g[l][None, :],
        cq_g=mla_cq_g[l][None, :], ckv_g=mla_ckv_g[l][None, :],
        w_uq=_pad_heads(mla_w_uq[l], MLA_QK).astype(BF16),
        w_k=_pad_heads(ukv[:, :, :MLA_NOPE].reshape(MLA_KV_RANK, -1), MLA_NOPE).astype(BF16),
        w_v=_pad_heads(ukv[:, :, MLA_NOPE:].reshape(MLA_KV_RANK, -1), MLA_V).astype(BF16),
        q_g=pad1(mla_q_g[l]), k_g=pad1(mla_k_g[l]),
        ln_g=gmlp_ln_g[l][None, :], ln_b=gmlp_ln_b[l][None, :],
        w_s=gmlp_w_s[l].astype(BF16),
        b_s=jnp.repeat(gmlp_b_s[l].T, GMLP_GROUP_DIM, axis=1),
        w_out=w_out[l].astype(BF16),
    )


def kernel(x, c, ctx, c_ctx, ada_w, ada_b, norm1_g, norm2_g, w_in, w_out, mlstm_gate_b, mlstm_norm_g, mla_cq_g, mla_ckv_g, mla_w_uq, mla_w_ukv, mla_q_g, mla_k_g, gmlp_ln_g, gmlp_ln_b, gmlp_w_s, gmlp_b_s, ffn_w1, ffn_w3, ffn_w2, moe_router_w, moe_router_b, moe_w1, moe_w3, moe_w2):
    bsz = x.shape[0]
    assert x.shape[1:] == (SEQ, D_MODEL) and ctx.shape[1:] == (CTX_LEN, D_MODEL)
    mod_rows = -(-(bsz + 1) // 8) * 8
    cvec = jnp.zeros((mod_rows, D_MODEL), F32).at[:bsz].set(c).at[bsz].set(c_ctx)
    mods_all = _ada_call(cvec, ada_w.astype(BF16), ada_b[:, None, :]).reshape(DEPTH, mod_rows, 6, D_MODEL)
    rope_tabs = _rope_tables()
    xall = jnp.concatenate([x, ctx], axis=1)
    for l in range(DEPTH):
        last = l == DEPTH - 1
        p = _layer_params(l, w_in, w_out, mlstm_gate_b, mlstm_norm_g, mla_cq_g, mla_ckv_g, mla_w_uq, mla_w_ukv,
                          mla_q_g, mla_k_g, gmlp_ln_g, gmlp_ln_b, gmlp_w_s, gmlp_b_s)
        mods = mods_all[l]
        z = _inproj_call(xall, mods, norm1_g[l][None, :], p["w_in"], bsz)
        a_mix = _mlstm_call(z, p["gate_b"], p["mlstm_g"])
        q, k, v = _qkv_call(z, p, rope_tabs)
        t_out = SEQ if last else T_ALL
        b_mix = _attn_call(q, k, v, t_out)
        x1 = _mixout_call(xall, z, a_mix, b_mix, mods, p, bsz, t_out, 1024 if last else 768)
        j = l // 2
        if l % 2 == 0:
            assert not last
            xall = _ffn_call(x1, mods, norm2_g[l][None, :], ffn_w1[j].astype(BF16), ffn_w3[j].astype(BF16),
                             ffn_w2[j].astype(BF16), bsz)
        else:
            assert last
            xall = _moe_layer(x1, mods, norm2_g[l][None, :], moe_router_w[j], moe_router_b[j],
                              moe_w1[j].astype(BF16), moe_w3[j].astype(BF16), moe_w2[j].astype(BF16))
    return xall
```

```python
import functools

import jax
import jax.numpy as jnp
from jax import lax
from jax.experimental import pallas as pl
from jax.experimental.pallas import tpu as pltpu

F32 = jnp.float32
BF16 = jnp.bfloat16
I32 = jnp.int32
HIGHEST = lax.Precision.HIGHEST

D_MODEL = 1024
SEQ = 2048
CTX_LEN = 256
T_ALL = SEQ + CTX_LEN
DEPTH = 2
GRID_W = 64
EPS = 1e-6
MLSTM_HEADS = 4
MLSTM_HEAD_DIM = 64
MLSTM_WIDTH = 256
CHUNK = 128
MLA_HEADS = 8
MLA_Q_RANK = 256
MLA_KV_RANK = 128
MLA_NOPE = 64
MLA_ROPE = 32
MLA_V = 64
MLA_QK = 96
MLA_WIDTH = 512
ROPE_BASE = 10000.0
GMLP_GROUPS = 4
GMLP_GROUP_DIM = 64
GMLP_WIDTH = 256
D_FF = 2816
N_EXPERTS = 8
TOP_K = 2
D_FF_EXPERT = 3584
MOE_BLOCK = 512
OFF_GA = 4 * MLSTM_WIDTH
OFF_CQ = OFF_GA + 4 * MLSTM_HEADS
OFF_CKV = OFF_CQ + MLA_Q_RANK
OFF_KR = OFF_CKV + MLA_KV_RANK
OFF_GM = OFF_KR + MLA_ROPE
IN_COLS = OFF_GM + 2 * GMLP_WIDTH

LANES = 128
HEAD_PAD = LANES
ZC_B = 0
ZC_GM = 512
ZC_QV = 1024
ZC_O = 1536
ZC_G = 1792
Z_COLS = ZC_G + 2 * LANES
GATE_ROWS = 8
VMEM_LIMIT = 56 * 1024 * 1024

N_CHUNKS = T_ALL // CHUNK
N_LAT_CHUNKS = SEQ // CHUNK


def _cparams(sem, vmem=VMEM_LIMIT):
    return pltpu.CompilerParams(dimension_semantics=sem, vmem_limit_bytes=vmem)


def _rms(x, g):
    return x * lax.rsqrt(jnp.mean(x * x, axis=-1, keepdims=True) + EPS) * g


def _silu(x):
    return x * jax.nn.sigmoid(x)


def _modulated_norm(x, g, shift, scale):
    return _rms(x, g) * (1.0 + scale) + shift


def _row_mods(ml_ref, mc_ref, first, tile_start, rows):
    row = tile_start + lax.broadcasted_iota(I32, (rows, 1), 0)
    is_ctx = row >= SEQ
    return tuple(jnp.where(is_ctx, mc_ref[0, first + k:first + k + 1, :], ml_ref[0, first + k:first + k + 1, :])
                 for k in range(3))


def _ada_kernel(c_ref, w_ref, b_ref, o_ref):
    s = _silu(c_ref[...]).astype(BF16)
    o_ref[0] = jnp.dot(s, w_ref[0], preferred_element_type=F32) + b_ref[0]


def _ada_call(cvec, ada_w, ada_b):
    rows = cvec.shape[0]
    return pl.pallas_call(
        _ada_kernel,
        grid=(DEPTH, 6),
        in_specs=[pl.BlockSpec((rows, D_MODEL), lambda l, j: (0, 0)),
                  pl.BlockSpec((1, D_MODEL, D_MODEL), lambda l, j: (l, 0, j)),
                  pl.BlockSpec((1, 1, D_MODEL), lambda l, j: (l, 0, j))],
        out_specs=pl.BlockSpec((1, rows, D_MODEL), lambda l, j: (l, 0, j)),
        out_shape=jax.ShapeDtypeStruct((DEPTH, rows, 6 * D_MODEL), F32),
        compiler_params=_cparams(("arbitrary", "arbitrary")),
        name="adaln",
    )(cvec, ada_w, ada_b)


def _inproj_kernel(x_ref, ml_ref, mc_ref, g_ref, w_ref, wk_ref, wg_ref, z_ref, zk_ref, zg_ref, *, tr):
    shift, scale, _ = _row_mods(ml_ref, mc_ref, 0, pl.program_id(1) * tr, tr)
    xn = _modulated_norm(x_ref[0], g_ref[...], shift, scale).astype(BF16)
    z_ref[0] = jnp.dot(xn, w_ref[...], preferred_element_type=F32)
    nt = (((1,), (1,)), ((), ()))
    zk_ref[0] = lax.dot_general(wk_ref[...], xn, nt, preferred_element_type=F32)
    zg_ref[0] = lax.dot_general(wg_ref[...], xn, nt, preferred_element_type=F32)


def _inproj_call(xall, mods, norm_g, w_main, w_kt, w_gt, n_ctx_row):
    bsz = xall.shape[0]
    tr = 768
    const = lambda b, i: (0, 0)
    return pl.pallas_call(
        functools.partial(_inproj_kernel, tr=tr),
        grid=(bsz, T_ALL // tr),
        in_specs=[pl.BlockSpec((1, tr, D_MODEL), lambda b, i: (b, i, 0)),
                  pl.BlockSpec((1, 6, D_MODEL), lambda b, i: (b, 0, 0)),
                  pl.BlockSpec((1, 6, D_MODEL), lambda b, i: (n_ctx_row, 0, 0)),
                  pl.BlockSpec((1, D_MODEL), const),
                  pl.BlockSpec((D_MODEL, Z_COLS), const),
                  pl.BlockSpec((MLSTM_WIDTH, D_MODEL), const),
                  pl.BlockSpec((2 * GATE_ROWS, D_MODEL), const)],
        out_specs=[pl.BlockSpec((1, tr, Z_COLS), lambda b, i: (b, i, 0)),
                   pl.BlockSpec((1, MLSTM_WIDTH, tr), lambda b, i: (b, 0, i)),
                   pl.BlockSpec((1, 2 * GATE_ROWS, tr), lambda b, i: (b, 0, i))],
        out_shape=[jax.ShapeDtypeStruct((bsz, T_ALL, Z_COLS), F32),
                   jax.ShapeDtypeStruct((bsz, MLSTM_WIDTH, T_ALL), F32),
                   jax.ShapeDtypeStruct((bsz, 2 * GATE_ROWS, T_ALL), F32)],
        compiler_params=_cparams(("parallel", "parallel")),
        name="in_proj",
    )(xall, mods, mods, norm_g, w_main, w_kt, w_gt)


def _scan_chunk(d, j):
    fwd_chunk = (j + N_LAT_CHUNKS) % N_CHUNKS
    return jnp.where(d == 0, fwd_chunk, N_CHUNKS - 1 - j)


MLSTM_ROWS_PER_STEP = 4


def _log_sigmoid(x):
    return jnp.minimum(x, 0.0) - jnp.log1p(jnp.exp(-jnp.abs(x)))


def _mlstm_kernel(qv_ref, og_ref, zg_ref, kt_ref, gt_ref, gb_ref, gbt_ref, ng_ref, o_ref, c_ref, m_ref, hf_ref):
    L = CHUNK
    E = MLSTM_HEAD_DIM
    d = pl.program_id(1)
    j = pl.program_id(2)
    fwd = d == 0
    row0 = pl.multiple_of(_scan_chunk(d, j) * L, L)

    @pl.when(j == 0)
    def _():
        c_ref[...] = jnp.zeros_like(c_ref)
        m_ref[...] = jnp.zeros_like(m_ref)

    r_i = lax.broadcasted_iota(I32, (L, L), 0)
    c_i = lax.broadcasted_iota(I32, (L, L), 1)
    prec = jnp.logical_or(jnp.logical_and(fwd, c_i <= r_i), jnp.logical_and(jnp.logical_not(fwd), c_i >= r_i))
    succ = jnp.logical_or(jnp.logical_and(fwd, r_i <= c_i), jnp.logical_and(jnp.logical_not(fwd), r_i >= c_i))
    lane = lax.broadcasted_iota(I32, (L, LANES), 1)
    sub = lax.broadcasted_iota(I32, (LANES, L), 0)
    low_lane = lane < E
    low_sub = sub < E

    row_idx = lax.broadcasted_iota(I32, (L, LANES), 0)
    rows = range(qv_ref.shape[0])
    pairs = range(MLSTM_HEADS // 2)
    h_pairs = {}
    for bb in rows:
        gates_c = zg_ref[bb] + gb_ref[0]
        b_c = jnp.dot(prec.astype(F32), _log_sigmoid(gates_c), precision=HIGHEST, preferred_element_type=F32)
        c_c = pltpu.roll(gates_c, LANES - MLSTM_HEADS, 1) - b_c
        m_row = m_ref[bb]
        cm = c_c
        for sh in (1, 2, 4, 8, 16, 32, 64):
            prev = jnp.where(fwd, jnp.where(row_idx >= sh, pltpu.roll(cm, sh, 0), -jnp.inf),
                             jnp.where(row_idx < L - sh, pltpu.roll(cm, L - sh, 0), -jnp.inf))
            cm = jnp.maximum(cm, prev)
        m_c = jnp.maximum(cm, m_row)
        a_c = jnp.exp(m_row - m_c)
        floor_c = jnp.exp(-(b_c + m_c))
        m_last = jnp.maximum(jnp.max(c_c, axis=0, keepdims=True), m_row)
        b_last = jnp.where(fwd, b_c[L - 1:L, :], b_c[0:1, :])
        m_ref[bb] = b_last + m_last
        decay_row = jnp.exp(m_row - m_last)
        gates_r = gt_ref[bb] + gbt_ref[0]
        b_r = jnp.dot(_log_sigmoid(gates_r), succ.astype(F32), precision=HIGHEST, preferred_element_type=F32)
        c_r = gates_r[MLSTM_HEADS:2 * MLSTM_HEADS, :] - b_r[0:MLSTM_HEADS, :]
        w_r = [jnp.exp(c_r[h:h + 1, :] - m_last[:, h:h + 1]) for h in range(MLSTM_HEADS)]

        for pj in pairs:
            tile = slice(pj * LANES, (pj + 1) * LANES)
            q_t = qv_ref[bb, :, tile]
            v_t = qv_ref[bb, :, MLSTM_WIDTH + pj * LANES:MLSTM_WIDTH + (pj + 1) * LANES]
            kt_t = kt_ref[bb, tile, :]
            c_old = c_ref[bb, pj]
            c_old_b = c_old.astype(BF16)
            kt_b = kt_t.astype(BF16)
            hx = []
            upd = []
            for par in range(2):
                h = 2 * pj + par
                own_lane = low_lane if par == 0 else jnp.logical_not(low_lane)
                own_sub = low_sub if par == 0 else jnp.logical_not(low_sub)
                den_lane = E if par == 0 else 0
                qm = jnp.where(own_lane, q_t, 0.0).astype(BF16)
                v_ext = jnp.where(own_lane, v_t, jnp.where(lane == den_lane, 1.0, 0.0)).astype(BF16)
                sqk = jnp.dot(qm, kt_b, preferred_element_type=F32)
                q_c = jnp.dot(qm, c_old_b, preferred_element_type=F32)
                dmat = jnp.where(prec, jnp.exp(c_r[h:h + 1, :] - m_c[:, h:h + 1]), 0.0)
                pv = jnp.dot((sqk * dmat).astype(BF16), v_ext, preferred_element_type=F32)
                nd = pv + a_c[:, h:h + 1] * q_c
                den = nd[:, den_lane:den_lane + 1]
                hx.append(nd / jnp.maximum(jnp.abs(den), floor_c[:, h:h + 1]))
                kw_t = (jnp.where(own_sub, kt_t, 0.0) * w_r[h]).astype(BF16)
                upd.append(jnp.dot(kw_t, v_ext, preferred_element_type=F32))
            decay = jnp.where(low_sub, decay_row[:, 2 * pj:2 * pj + 1], decay_row[:, 2 * pj + 1:2 * pj + 2])
            c_ref[bb, pj] = decay * c_old + upd[0] + upd[1]
            h_pairs[bb, pj] = jnp.where(low_lane, hx[0], hx[1])

    @pl.when(fwd)
    def _():
        for (bb, pj), h_pair in h_pairs.items():
            hf_ref[bb, pl.ds(row0, L), pj * LANES:(pj + 1) * LANES] = h_pair

    @pl.when(jnp.logical_not(fwd))
    def _():
        for (bb, pj), h_pair in h_pairs.items():
            tile = slice(pj * LANES, (pj + 1) * LANES)
            hs = hf_ref[bb, pl.ds(row0, L), tile] + h_pair
            sq = hs * hs
            ss = jnp.where(low_lane,
                           jnp.sum(jnp.where(low_lane, sq, 0.0), axis=-1, keepdims=True),
                           jnp.sum(jnp.where(low_lane, 0.0, sq), axis=-1, keepdims=True))
            y = hs * lax.rsqrt(ss * (1.0 / E) + EPS) * ng_ref[:, tile]
            o_ref[bb, :, tile] = jax.nn.sigmoid(og_ref[bb, :, tile]) * y


def _mlstm_call(z, zkt, zgt, gate_b, gate_bt, norm_g):
    bsz = z.shape[0]
    nb = max(r for r in range(1, MLSTM_ROWS_PER_STEP + 1) if bsz % r == 0)
    last = N_CHUNKS - 1
    chunk =lambda b, d, j: _scan_chunk(d, j)
    return pl.pallas_call(
        _mlstm_kernel,
        grid=(bsz // nb, 2, N_CHUNKS),
        in_specs=[pl.BlockSpec((nb, CHUNK, 2 * MLSTM_WIDTH), lambda b, d, j: (b, chunk(b, d, j), ZC_QV // 512)),
                  pl.BlockSpec((nb, CHUNK, MLSTM_WIDTH), lambda b, d, j: (b, chunk(b, d, j), ZC_O // MLSTM_WIDTH)),
                  pl.BlockSpec((nb, CHUNK, LANES), lambda b, d, j: (b, chunk(b, d, j), ZC_G // LANES + d)),
                  pl.BlockSpec((nb, MLSTM_WIDTH, CHUNK), lambda b, d, j: (b, 0, chunk(b, d, j))),
                  pl.BlockSpec((nb, GATE_ROWS, CHUNK), lambda b, d, j: (b, d, chunk(b, d, j))),
                  pl.BlockSpec((1, 1, LANES), lambda b, d, j: (d, 0, 0)),
                  pl.BlockSpec((1, GATE_ROWS, LANES), lambda b, d, j: (d, 0, 0)),
                  pl.BlockSpec((1, MLSTM_WIDTH), lambda b, d, j: (0, 0))],
        out_specs=pl.BlockSpec((nb, CHUNK, MLSTM_WIDTH),
                               lambda b, d, j: (b, jnp.where(d == 0, last, last - j), 0)),
        out_shape=jax.ShapeDtypeStruct((bsz, T_ALL, MLSTM_WIDTH), F32),
        scratch_shapes=[pltpu.VMEM((nb, MLSTM_HEADS // 2, LANES, LANES), F32),
                        pltpu.VMEM((nb, 1, LANES), F32),
                        pltpu.VMEM((nb, T_ALL, MLSTM_WIDTH), F32)],
        compiler_params=_cparams(("parallel", "arbitrary", "arbitrary")),
        name="mlstm",
    )(z, z, z, zkt, zgt, gate_b, gate_bt, norm_g)


ROPE_HALF = MLA_ROPE // 2


def _rope(t, cos_t, sin_t):
    return t * cos_t + pltpu.roll(t, LANES - ROPE_HALF, 1) * sin_t


def _head_rms(t, g, real):
    ss = jnp.sum(jnp.where(real, t * t, 0.0), axis=-1, keepdims=True)
    return t * lax.rsqrt(ss * (1.0 / MLA_QK) + EPS) * g


def _qkv_kernel(z_ref, cqg_ref, ckvg_ref, wq_ref, wk_ref, wv_ref, qg_ref, kg_ref, ct_ref, st_ref,
                q_ref, k_ref, v_ref, *, tr):
    cqn = _rms(z_ref[0, :, 0:MLA_Q_RANK], cqg_ref[...]).astype(BF16)
    ckvn = _rms(z_ref[0, :, MLA_Q_RANK:MLA_Q_RANK + MLA_KV_RANK], ckvg_ref[...]).astype(BF16)
    k_rope = z_ref[0, :, MLA_Q_RANK + MLA_KV_RANK:MLA_Q_RANK + MLA_KV_RANK + LANES]
    q_raw = jnp.dot(cqn, wq_ref[...], preferred_element_type=F32)
    k_raw = jnp.dot(ckvn, wk_ref[...], preferred_element_type=F32)
    v_raw = jnp.dot(ckvn, wv_ref[...], preferred_element_type=F32)
    cos_t = ct_ref[...]
    sin_t = st_ref[...]
    lane = lax.broadcasted_iota(I32, (tr, LANES), 1)
    real = lane < MLA_QK
    for h in range(MLA_HEADS):
        sl = slice(h * HEAD_PAD, (h + 1) * HEAD_PAD)
        qh = _rope(_head_rms(q_raw[:, sl], qg_ref[...], real), cos_t, sin_t)
        q_ref[0, :, sl] = (qh * (MLA_QK ** -0.5)).astype(BF16)
        kh = _rope(_head_rms(k_raw[:, sl] + k_rope, kg_ref[...], real), cos_t, sin_t)
        k_ref[0, :, sl] = kh.astype(BF16)
        v_ref[0, :, sl] = jnp.where(lane == MLA_V, 1.0, v_raw[:, sl]).astype(BF16)


def _qkv_call(z, p, rope_tabs):
    bsz = z.shape[0]
    tr = 768
    hw = MLA_HEADS * HEAD_PAD
    const = lambda b, i: (0, 0)
    out = jax.ShapeDtypeStruct((bsz, T_ALL, hw), BF16)
    tab = pl.BlockSpec((tr, LANES), lambda b, i: (i, 0))
    ospec = pl.BlockSpec((1, tr, hw), lambda b, i: (b, i, 0))
    return pl.pallas_call(
        functools.partial(_qkv_kernel, tr=tr),
        grid=(bsz, T_ALL // tr),
        in_specs=[pl.BlockSpec((1, tr, 512), lambda b, i: (b, i, ZC_B // 512)),
                  pl.BlockSpec((1, MLA_Q_RANK), const), pl.BlockSpec((1, MLA_KV_RANK), const),
                  pl.BlockSpec((MLA_Q_RANK, hw), const), pl.BlockSpec((MLA_KV_RANK, hw), const),
                  pl.BlockSpec((MLA_KV_RANK, hw), const),
                  pl.BlockSpec((1, LANES), const), pl.BlockSpec((1, LANES), const),
                  tab, tab],
        out_specs=[ospec, ospec, ospec],
        out_shape=[out, out, out],
        compiler_params=_cparams(("parallel", "parallel")),
        name="mla_qkv",
    )(z, p["cq_g"], p["ckv_g"], p["w_uq"], p["w_k"], p["w_v"], p["q_g"], p["k_g"], *rope_tabs)


ATT_TQ = 256
ATT_HEADS_PER_STEP = 8


def _attn_kernel(q_ref, k_ref, v_ref, o_ref):
    i = pl.program_id(2)

    def run(k0):
        def scores(hh):
            sl = slice(hh * HEAD_PAD, (hh + 1) * HEAD_PAD)
            return lax.dot_general(q_ref[0, :, sl], k_ref[0, k0:T_ALL, sl], (((1,), (1,)), ((), ())),
                                   preferred_element_type=F32)

        s = scores(0)
        for hh in range(ATT_HEADS_PER_STEP):
            s_next = scores(hh + 1) if hh + 1 < ATT_HEADS_PER_STEP else None
            p = jnp.exp(s - jnp.max(s, axis=-1, keepdims=True)).astype(BF16)
            oe = jnp.dot(p, v_ref[0, k0:T_ALL, hh * HEAD_PAD:(hh + 1) * HEAD_PAD], preferred_element_type=F32)
            o_ref[0, :, hh * MLA_V:(hh + 1) * MLA_V] = oe[:, 0:MLA_V] / oe[:, MLA_V:MLA_V + 1]
            s = s_next

    @pl.when(i < SEQ // ATT_TQ)
    def _():
        run(0)

    @pl.when(i >= SEQ // ATT_TQ)
    def _():
        run(SEQ)


def _attn_call(q, k, v, t_out):
    bsz = q.shape[0]
    w = ATT_HEADS_PER_STEP * HEAD_PAD
    return pl.pallas_call(
        _attn_kernel,
        grid=(bsz, MLA_HEADS // ATT_HEADS_PER_STEP, t_out // ATT_TQ),
        in_specs=[pl.BlockSpec((1, ATT_TQ, w), lambda b, h, i: (b, i, h)),
                  pl.BlockSpec((1, T_ALL, w), lambda b, h, i: (b, 0, h)),
                  pl.BlockSpec((1, T_ALL, w), lambda b, h, i: (b, 0, h))],
        out_specs=pl.BlockSpec((1, ATT_TQ, ATT_HEADS_PER_STEP * MLA_V), lambda b, h, i: (b, i, h)),
        out_shape=jax.ShapeDtypeStruct((bsz, t_out, MLA_WIDTH), F32),
        compiler_params=_cparams(("parallel", "parallel", "arbitrary")),
        name="mla_attn",
    )(q, k, v)


def _mixout_kernel(x_ref, zg_ref, a_ref, b_ref, ml_ref, mc_ref, lng_ref, lnb_ref, ws_ref, bs_ref, wo_ref,
                   o_ref, cm_ref, *, tr):
    gd = GMLP_GROUP_DIM
    act = jax.nn.gelu(zg_ref[0])
    u = act[:, 0:GMLP_WIDTH]
    vv = act[:, GMLP_WIDTH:2 * GMLP_WIDTH]
    mu = jnp.mean(vv, axis=-1, keepdims=True)
    var = jnp.mean(jnp.square(vv - mu), axis=-1, keepdims=True)
    vn = ((vv - mu) * lax.rsqrt(var + EPS) * lng_ref[...] + lnb_ref[...]).astype(BF16)
    for c in range(tr // CHUNK):
        rows = slice(c * CHUNK, (c + 1) * CHUNK)
        for g in range(GMLP_GROUPS):
            cols = slice(g * gd, (g + 1) * gd)
            sv = jnp.dot(ws_ref[g], vn[rows, cols], preferred_element_type=F32) + bs_ref[:, cols]
            cm_ref[rows, cols] = u[rows, cols] * sv
    y = jnp.dot(a_ref[0].astype(BF16), wo_ref[0:MLSTM_WIDTH, :], preferred_element_type=F32)
    y += jnp.dot(b_ref[0].astype(BF16), wo_ref[MLSTM_WIDTH:MLSTM_WIDTH + MLA_WIDTH, :],
                 preferred_element_type=F32)
    y += jnp.dot(cm_ref[...].astype(BF16), wo_ref[MLSTM_WIDTH + MLA_WIDTH:, :], preferred_element_type=F32)
    _, _, gate = _row_mods(ml_ref, mc_ref, 0, pl.program_id(1) * tr, tr)
    o_ref[0] = x_ref[0] + gate * y


def _mixout_call(xall, z, a_mix, b_mix, mods, p, n_ctx_row, t_out, tr):
    bsz = xall.shape[0]
    const = lambda b, i: (0, 0)
    return pl.pallas_call(
        functools.partial(_mixout_kernel, tr=tr),
        grid=(bsz, t_out // tr),
        in_specs=[pl.BlockSpec((1, tr, D_MODEL), lambda b, i: (b, i, 0)),
                  pl.BlockSpec((1, tr, 2 * GMLP_WIDTH), lambda b, i: (b, i, ZC_GM // (2 * GMLP_WIDTH))),
                  pl.BlockSpec((1, tr, MLSTM_WIDTH), lambda b, i: (b, i, 0)),
                  pl.BlockSpec((1, tr, MLA_WIDTH), lambda b, i: (b, i, 0)),
                  pl.BlockSpec((1, 6, D_MODEL), lambda b, i: (b, 0, 0)),
                  pl.BlockSpec((1, 6, D_MODEL), lambda b, i: (n_ctx_row, 0, 0)),
                  pl.BlockSpec((1, GMLP_WIDTH), const), pl.BlockSpec((1, GMLP_WIDTH), const),
                  pl.BlockSpec((GMLP_GROUPS, CHUNK, CHUNK), lambda b, i: (0, 0, 0)),
                  pl.BlockSpec((CHUNK, GMLP_WIDTH), const),
                  pl.BlockSpec((D_MODEL, D_MODEL), const)],
        out_specs=pl.BlockSpec((1, tr, D_MODEL), lambda b, i: (b, i, 0)),
        out_shape=jax.ShapeDtypeStruct((bsz, t_out, D_MODEL), F32),
        scratch_shapes=[pltpu.VMEM((tr, GMLP_WIDTH), F32)],
        compiler_params=_cparams(("parallel", "parallel")),
        name="mix_out",
    )(xall, z, a_mix, b_mix, mods, mods, p["ln_g"], p["ln_b"], p["w_s"], p["b_s"], p["w_out"])


FFN_SPLIT = 2


def _ffn_kernel(x_ref, ml_ref, mc_ref, g_ref, w1_ref, w3_ref, w2_ref, o_ref, *, tr):
    shift, scale, gate = _row_mods(ml_ref, mc_ref, 3, pl.program_id(1) * tr, tr)
    x = x_ref[0]
    h = _modulated_norm(x, g_ref[...], shift, scale).astype(BF16)
    fc = D_FF // FFN_SPLIT
    y = jnp.zeros((tr, D_MODEL), F32)
    for f in range(FFN_SPLIT):
        cols = slice(f * fc, (f + 1) * fc)
        h1 = jnp.dot(h, w1_ref[:, cols], preferred_element_type=F32)
        h3 = jnp.dot(h, w3_ref[:, cols], preferred_element_type=F32)
        y += jnp.dot((_silu(h1) * h3).astype(BF16), w2_ref[cols, :], preferred_element_type=F32)
    o_ref[0] = x + gate * y


def _ffn_call(x1, mods, norm_g, w1, w3, w2, n_ctx_row):
    bsz = x1.shape[0]
    tr = 768
    const = lambda b, i: (0, 0)
    resident = pl.Buffered(1)
    return pl.pallas_call(
        functools.partial(_ffn_kernel, tr=tr),
        grid=(bsz, T_ALL // tr),
        in_specs=[pl.BlockSpec((1, tr, D_MODEL), lambda b, i: (b, i, 0)),
                  pl.BlockSpec((1, 6, D_MODEL), lambda b, i: (b, 0, 0)),
                  pl.BlockSpec((1, 6, D_MODEL), lambda b, i: (n_ctx_row, 0, 0)),
                  pl.BlockSpec((1, D_MODEL), const),
                  pl.BlockSpec((D_MODEL, D_FF), const, pipeline_mode=resident),
                  pl.BlockSpec((D_MODEL, D_FF), const, pipeline_mode=resident),
                  pl.BlockSpec((D_FF, D_MODEL), const, pipeline_mode=resident)],
        out_specs=pl.BlockSpec((1, tr, D_MODEL), lambda b, i: (b, i, 0)),
        out_shape=jax.ShapeDtypeStruct((bsz, T_ALL, D_MODEL), F32),
        compiler_params=_cparams(("parallel", "parallel")),
        name="dense_ffn",
    )(x1, mods, mods, norm_g, w1, w3, w2)


ROUTE_TR = 1024


def _router_kernel(x_ref, ml_ref, g_ref, rw_ref, rb_ref, h_ref, e_ref, r_ref, gt_ref, cnt_ref, run_ref, *, tr):
    step = pl.program_id(0) * pl.num_programs(1) + pl.program_id(1)

    @pl.when(step == 0)
    def _():
        run_ref[...] = jnp.zeros_like(run_ref)

    h = _modulated_norm(x_ref[0], g_ref[...], ml_ref[0, 3:4, :], ml_ref[0, 4:5, :])
    h_ref[...] = h
    lane = lax.broadcasted_iota(I32, (tr, LANES), 1)
    logits = jnp.dot(h, rw_ref[...], precision=HIGHEST, preferred_element_type=F32) + rb_ref[...]
    logits = jnp.where(lane < N_EXPERTS, logits, -jnp.inf)
    m1 = jnp.max(logits, axis=-1, keepdims=True)
    e1 = jnp.min(jnp.where(logits == m1, lane, LANES), axis=-1, keepdims=True)
    rest = jnp.where(lane == e1, -jnp.inf, logits)
    m2 = jnp.max(rest, axis=-1, keepdims=True)
    e2 = jnp.min(jnp.where(rest == m2, lane, LANES), axis=-1, keepdims=True)
    ex = jnp.exp(m2 - m1)
    g1 = 1.0 / (1.0 + ex)
    g2 = ex / (1.0 + ex)
    onehot = jnp.logical_or(lane == e1, lane == e2 + N_EXPERTS)
    oh = onehot.astype(F32)
    r_i = lax.broadcasted_iota(I32, (tr, tr), 0)
    c_i = lax.broadcasted_iota(I32, (tr, tr), 1)
    before = jnp.dot((c_i < r_i).astype(BF16), oh.astype(BF16), preferred_element_type=F32)
    tot = jnp.sum(oh, axis=0, keepdims=True)
    tot0_shift = pltpu.roll(tot, N_EXPERTS, 1)
    run = run_ref[...]
    first_half = lax.broadcasted_iota(I32, (1, LANES), 1) < N_EXPERTS
    offs = run + jnp.where(first_half, 0.0, tot0_shift)
    ranks = oh * (before + offs)
    rank1 = jnp.sum(jnp.where(lane < N_EXPERTS, ranks, 0.0), axis=-1, keepdims=True)
    rank2 = jnp.sum(jnp.where(lane >= N_EXPERTS, ranks, 0.0), axis=-1, keepdims=True)
    col = lax.broadcasted_iota(I32, (tr, TOP_K), 1)
    e_ref[...] = jnp.where(col == 0, e1, e2)
    r_ref[...] = jnp.where(col == 0, rank1, rank2).astype(I32)
    gt_ref[...] = jnp.where(col == 0, g1, g2)
    both = tot + jnp.where(first_half, pltpu.roll(tot, LANES - N_EXPERTS, 1), tot0_shift)
    new_run = run + both
    run_ref[...] = new_run
    cnt_ref[...] = new_run.astype(I32)


def _router_call(x1, mods, norm_g, rw_p, rb_p):
    bsz = x1.shape[0]
    tr = ROUTE_TR
    n = bsz * SEQ
    nt = SEQ // tr
    const = lambda b, i: (0, 0)
    tok = lambda b, i: (b * nt + i, 0)
    return pl.pallas_call(
        functools.partial(_router_kernel, tr=tr),
        grid=(bsz, nt),
        in_specs=[pl.BlockSpec((1, tr, D_MODEL), lambda b, i: (b, i, 0)),
                  pl.BlockSpec((1, 6, D_MODEL), lambda b, i: (b, 0, 0)),
                  pl.BlockSpec((1, D_MODEL), const),
                  pl.BlockSpec((D_MODEL, LANES), const),
                  pl.BlockSpec((1, LANES), const)],
        out_specs=[pl.BlockSpec((tr, D_MODEL), tok),
                   pl.BlockSpec((tr, TOP_K), tok), pl.BlockSpec((tr, TOP_K), tok), pl.BlockSpec((tr, TOP_K), tok),
                   pl.BlockSpec((1, LANES), const)],
        out_shape=[jax.ShapeDtypeStruct((n, D_MODEL), F32),
                   jax.ShapeDtypeStruct((n, TOP_K), I32), jax.ShapeDtypeStruct((n, TOP_K), I32),
                   jax.ShapeDtypeStruct((n, TOP_K), F32),
                   jax.ShapeDtypeStruct((1, LANES), I32)],
        scratch_shapes=[pltpu.VMEM((1, LANES), F32)],
        compiler_params=_cparams(("arbitrary", "arbitrary")),
        name="moe_router",
    )(x1, mods, norm_g, rw_p, rb_p)


DISPATCH_TD = 512
DMA_ISSUE_UNROLL = 8


def _row_copy(src, src_row, dst, dst_row, sem):
    return pltpu.make_async_copy(src.at[pl.ds(src_row, 1)], dst.at[pl.ds(dst_row, 1)], sem)


def _dispatch_kernel(dest_ref, h_ref, xs_in, xs_out, sem):
    del xs_in

    def issue(r, carry):
        for kk in range(TOP_K):
            _row_copy(h_ref, r, xs_out, dest_ref[kk, r], sem).start()
        return carry

    lax.fori_loop(0, DISPATCH_TD, issue, 0, unroll=DMA_ISSUE_UNROLL)

    for kk in range(TOP_K):
        pltpu.make_async_copy(h_ref, xs_out.at[pl.ds(0, DISPATCH_TD)], sem).wait()


def _dispatch_call(dest_t, h2, cap):
    n = h2.shape[0]
    xs0 = jnp.zeros((cap, D_MODEL), F32)
    return pl.pallas_call(
        _dispatch_kernel,
        grid=(n // DISPATCH_TD,),
        in_specs=[pl.BlockSpec((TOP_K, DISPATCH_TD), lambda i: (0, i), memory_space=pltpu.SMEM),
                  pl.BlockSpec((DISPATCH_TD, D_MODEL), lambda i: (i, 0)),
                  pl.BlockSpec(memory_space=pl.ANY)],
        out_specs=pl.BlockSpec(memory_space=pl.ANY),
        out_shape=jax.ShapeDtypeStruct((cap, D_MODEL), F32),
        scratch_shapes=[pltpu.SemaphoreType.DMA(())],
        input_output_aliases={2: 0},
        compiler_params=_cparams(("arbitrary",)),
        name="moe_dispatch",
    )(dest_t, h2, xs0)


EXPERT_FSPLIT = 2


def _expert_kernel(be_ref, na_ref, xs_ref, w1_ref, w3_ref, w2_ref, ys_ref):
    i = pl.program_id(0)
    f = pl.program_id(1)
    active = i < na_ref[0]

    @pl.when(jnp.logical_and(active, f == 0))
    def _():
        ys_ref[...] = jnp.zeros_like(ys_ref)

    @pl.when(active)
    def _():
        x = xs_ref[...].astype(BF16)
        h1 = jnp.dot(x, w1_ref[0], preferred_element_type=F32)
        h3 = jnp.dot(x, w3_ref[0], preferred_element_type=F32)
        ys_ref[...] += jnp.dot((_silu(h1) * h3).astype(BF16), w2_ref[0], preferred_element_type=F32)

    @pl.when(jnp.logical_not(active))
    def _():
        ys_ref[...] = jnp.zeros_like(ys_ref)


def _expert_call(block_expert, n_active, xs, w1, w3, w2):
    cap = xs.shape[0]
    nb = cap // MOE_BLOCK
    fc = D_FF_EXPERT // EXPERT_FSPLIT
    last_f = EXPERT_FSPLIT - 1

    def blk(i, na):
        return jnp.minimum(i, jnp.maximum(na[0] - 1, 0))

    def fidx(i, f, na):
        return jnp.where(i < na[0], f, last_f)

    grid_spec = pltpu.PrefetchScalarGridSpec(
        num_scalar_prefetch=2,
        grid=(nb, EXPERT_FSPLIT),
        in_specs=[pl.BlockSpec((MOE_BLOCK, D_MODEL), lambda i, f, be, na: (blk(i, na), 0)),
                  pl.BlockSpec((1, D_MODEL, fc), lambda i, f, be, na: (be[blk(i, na)], 0, fidx(i, f, na))),
                  pl.BlockSpec((1, D_MODEL, fc), lambda i, f, be, na: (be[blk(i, na)], 0, fidx(i, f, na))),
                  pl.BlockSpec((1, fc, D_MODEL), lambda i, f, be, na: (be[blk(i, na)], fidx(i, f, na), 0))],
        out_specs=pl.BlockSpec((MOE_BLOCK, D_MODEL), lambda i, f, be, na: (i, 0)),
    )
    return pl.pallas_call(
        _expert_kernel,
        grid_spec=grid_spec,
        out_shape=jax.ShapeDtypeStruct((cap, D_MODEL), F32),
        compiler_params=_cparams(("arbitrary", "arbitrary")),
        name="moe_experts",
    )(block_expert, n_active, xs, w1, w3, w2)


COMBINE_TC = 256


def _combine_kernel(dest_ref, dest_next_ref, x_ref, gt_ref, ml_ref, ys_hbm, o_ref, buf_ref, sem):
    i = pl.program_id(0)
    slot = i % 2

    def gather(d_ref, s):
        def issue(r, carry):
            for kk in range(TOP_K):
                _row_copy(ys_hbm, d_ref[kk, r], buf_ref.at[s, kk], r, sem.at[s]).start()
            return carry

        lax.fori_loop(0, COMBINE_TC, issue, 0, unroll=DMA_ISSUE_UNROLL)

    @pl.when(i == 0)
    def _():
        gather(dest_ref, 0)

    @pl.when(i + 1 < pl.num_programs(0))
    def _():
        gather(dest_next_ref, 1 - slot)

    for kk in range(TOP_K):
        pltpu.make_async_copy(ys_hbm.at[pl.ds(0, COMBINE_TC)], buf_ref.at[slot, kk], sem.at[slot]).wait()
    g = gt_ref[...]
    y = buf_ref[slot, 0] * g[:, 0:1] + buf_ref[slot, 1] * g[:, 1:2]
    o_ref[...] = x_ref[...] + ml_ref[0, 5:6, :] * y


def _combine_call(dest_t, x1_flat, gates, mods, ys):
    n = x1_flat.shape[0]
    tc = COMBINE_TC
    per_batch = SEQ // tc
    last = n // tc - 1
    return pl.pallas_call(
        _combine_kernel,
        grid=(n // tc,),
        in_specs=[pl.BlockSpec((TOP_K, tc), lambda i: (0, i), memory_space=pltpu.SMEM),
                  pl.BlockSpec((TOP_K, tc), lambda i: (0, jnp.minimum(i + 1, last)), memory_space=pltpu.SMEM),
                  pl.BlockSpec((tc, D_MODEL), lambda i: (i, 0)),
                  pl.BlockSpec((tc, TOP_K), lambda i: (i, 0)),
                  pl.BlockSpec((1, 6, D_MODEL), lambda i: (i // per_batch, 0, 0)),
                  pl.BlockSpec(memory_space=pl.ANY)],
        out_specs=pl.BlockSpec((tc, D_MODEL), lambda i: (i, 0)),
        out_shape=jax.ShapeDtypeStruct((n, D_MODEL), F32),
        scratch_shapes=[pltpu.VMEM((2, TOP_K, tc, D_MODEL), F32), pltpu.SemaphoreType.DMA((2,))],
        compiler_params=_cparams(("arbitrary",)),
        name="moe_combine",
    )(dest_t, dest_t, x1_flat, gates, mods, ys)


def _moe_layer(x1, mods, norm_g, router_w, router_b, w1, w3, w2):
    bsz = x1.shape[0]
    n = bsz * SEQ
    cap = (n * TOP_K + MOE_BLOCK - 1) // MOE_BLOCK * MOE_BLOCK + N_EXPERTS * MOE_BLOCK
    rw_p = jnp.zeros((D_MODEL, LANES), F32).at[:, :N_EXPERTS].set(router_w)
    rb_p = jnp.zeros((1, LANES), F32).at[0, :N_EXPERTS].set(router_b)
    h2, e_idx, rank, gates, counts = _router_call(x1, mods, norm_g, rw_p, rb_p)
    counts = counts[0, :N_EXPERTS]
    padded = (counts + MOE_BLOCK - 1) // MOE_BLOCK * MOE_BLOCK
    padded_end = jnp.cumsum(padded)
    base = padded_end - padded
    dest_t = (base[e_idx] + rank).T.astype(I32)
    n_blocks = cap // MOE_BLOCK
    block_start = jnp.arange(n_blocks, dtype=I32) * MOE_BLOCK
    block_expert = jnp.minimum(jnp.sum(padded_end[None, :] <= block_start[:, None], axis=1),
                               N_EXPERTS - 1).astype(I32)
    n_active = (padded_end[-1:] // MOE_BLOCK).astype(I32)
    xs = _dispatch_call(dest_t, h2, cap)
    ys = _expert_call(block_expert, n_active, xs, w1, w3, w2)
    out = _combine_call(dest_t, x1.reshape(n, D_MODEL), gates, mods, ys)
    return out.reshape(bsz, SEQ, D_MODEL)


def _rope_tables():
    rows = SEQ // GRID_W
    row = jnp.repeat(jnp.arange(rows), GRID_W).astype(F32)
    col = jnp.tile(jnp.arange(GRID_W), rows).astype(F32)
    n_freq = MLA_ROPE // 4
    inv = ROPE_BASE ** (-jnp.arange(n_freq, dtype=F32) / n_freq)
    ang = jnp.concatenate([row[:, None] * inv, col[:, None] * inv], axis=-1)
    cos, sin = jnp.cos(ang), jnp.sin(ang)
    ones = jnp.ones((SEQ, MLA_NOPE), F32)
    pad = jnp.zeros((SEQ, LANES - MLA_QK), F32)
    zn = jnp.zeros((SEQ, MLA_NOPE), F32)
    cos_t = jnp.concatenate([ones, cos, cos, pad], axis=-1)
    sin_t = jnp.concatenate([zn, -sin, sin, pad], axis=-1)
    ident = jnp.concatenate([jnp.ones((CTX_LEN, MLA_QK), F32), jnp.zeros((CTX_LEN, LANES - MLA_QK), F32)], axis=-1)
    zero = jnp.zeros((CTX_LEN, LANES), F32)
    return jnp.concatenate([cos_t, ident], axis=0), jnp.concatenate([sin_t, zero], axis=0)


def _rope_copy_pad(a):
    first_half = a[..., MLA_NOPE:MLA_NOPE + ROPE_HALF]
    zeros = jnp.zeros(a.shape[:-1] + (LANES - MLA_QK - ROPE_HALF,), a.dtype)
    return jnp.concatenate([a, first_half, zeros], axis=-1)


def _gate_cols(ga, d):
    h = MLSTM_HEADS
    return jnp.concatenate([ga[..., (2 + d) * h:(3 + d) * h], ga[..., d * h:(d + 1) * h]], axis=-1)


def _relayout_w_in(w_in):
    zeros = lambda n: jnp.zeros((D_MODEL, n), F32)
    w = MLSTM_WIDTH
    ga = w_in[:, OFF_GA:OFF_CQ]
    gate_tile = lambda d: jnp.concatenate([_gate_cols(ga, d), zeros(LANES - GATE_ROWS)], axis=-1)
    main = jnp.concatenate([
        w_in[:, OFF_CQ:OFF_KR],
        _rope_copy_pad(jnp.concatenate([zeros(MLA_NOPE), w_in[:, OFF_KR:OFF_GM]], axis=-1)),
        w_in[:, OFF_GM:IN_COLS],
        w_in[:, 0:w], w_in[:, 2 * w:3 * w],
        w_in[:, 3 * w:4 * w],
        gate_tile(0), gate_tile(1)], axis=-1)
    keys_t = (w_in[:, w:2 * w] * (MLSTM_HEAD_DIM ** -0.5)).T
    gates_t = jnp.concatenate([_gate_cols(ga, 0), _gate_cols(ga, 1)], axis=-1).T
    return main.astype(BF16), keys_t.astype(BF16), gates_t.astype(BF16)


def _relayout_gate_b(gb):
    per_dir = jnp.stack([_gate_cols(gb, 0), _gate_cols(gb, 1)])
    col_form = jnp.pad(per_dir, ((0, 0), (0, LANES - GATE_ROWS)))[:, None, :]
    row_form = jnp.broadcast_to(per_dir[:, :, None], (2, GATE_ROWS, LANES))
    return col_form, row_form


def _pad_heads(w, width):
    kdim = w.shape[0]
    w = w.reshape(kdim, MLA_HEADS, width)
    return jnp.pad(w, ((0, 0), (0, 0), (0, HEAD_PAD - width))).reshape(kdim, MLA_HEADS * HEAD_PAD)


def _layer_params(l, w_in, w_out, mlstm_gate_b, mlstm_norm_g, mla_cq_g, mla_ckv_g, mla_w_uq, mla_w_ukv,
                  mla_q_g, mla_k_g, gmlp_ln_g, gmlp_ln_b, gmlp_w_s, gmlp_b_s):
    ukv = mla_w_ukv[l].reshape(MLA_KV_RANK, MLA_HEADS, MLA_NOPE + MLA_V)
    pad1 = lambda g: _rope_copy_pad(g)[None, :]
    w_uq = _rope_copy_pad(mla_w_uq[l].reshape(MLA_Q_RANK, MLA_HEADS, MLA_QK)).reshape(MLA_Q_RANK, -1)
    return dict(
        w_in=_relayout_w_in(w_in[l]),
        gate_b=_relayout_gate_b(mlstm_gate_b[l]),
        mlstm_g=mlstm_norm_g[l][None, :],
        cq_g=mla_cq_g[l][None, :], ckv_g=mla_ckv_g[l][None, :],
        w_uq=w_uq.astype(BF16),
        w_k=_pad_heads(ukv[:, :, :MLA_NOPE].reshape(MLA_KV_RANK, -1), MLA_NOPE).astype(BF16),
        w_v=_pad_heads(ukv[:, :, MLA_NOPE:].reshape(MLA_KV_RANK, -1), MLA_V).astype(BF16),
        q_g=pad1(mla_q_g[l]), k_g=pad1(mla_k_g[l]),
        ln_g=gmlp_ln_g[l][None, :], ln_b=gmlp_ln_b[l][None, :],
        w_s=gmlp_w_s[l].astype(BF16),
        b_s=jnp.repeat(gmlp_b_s[l].T, GMLP_GROUP_DIM, axis=1),
        w_out=w_out[l].astype(BF16),
    )


def kernel(x, c, ctx, c_ctx, ada_w, ada_b, norm1_g, norm2_g, w_in, w_out, mlstm_gate_b, mlstm_norm_g, mla_cq_g, mla_ckv_g, mla_w_uq, mla_w_ukv, mla_q_g, mla_k_g, gmlp_ln_g, gmlp_ln_b, gmlp_w_s, gmlp_b_s, ffn_w1, ffn_w3, ffn_w2, moe_router_w, moe_router_b, moe_w1, moe_w3, moe_w2):
    bsz = x.shape[0]
    assert x.shape[1:] == (SEQ, D_MODEL) and ctx.shape[1:] == (CTX_LEN, D_MODEL)
    mod_rows = -(-(bsz + 1) // 8) * 8
    cvec = jnp.zeros((mod_rows, D_MODEL), F32).at[:bsz].set(c).at[bsz].set(c_ctx)
    mods_all = _ada_call(cvec, ada_w.astype(BF16), ada_b[:, None, :]).reshape(DEPTH, mod_rows, 6, D_MODEL)
    rope_tabs = _rope_tables()
    xall = jnp.concatenate([x, ctx], axis=1)
    for l in range(DEPTH):
        last = l == DEPTH - 1
        p = _layer_params(l, w_in, w_out, mlstm_gate_b, mlstm_norm_g, mla_cq_g, mla_ckv_g, mla_w_uq, mla_w_ukv,
                          mla_q_g, mla_k_g, gmlp_ln_g, gmlp_ln_b, gmlp_w_s, gmlp_b_s)
        mods = mods_all[l]
        z, zkt, zgt = _inproj_call(xall, mods, norm1_g[l][None, :], *p["w_in"], bsz)
        a_mix = _mlstm_call(z, zkt, zgt, *p["gate_b"], p["mlstm_g"])
        q, k, v = _qkv_call(z, p, rope_tabs)
        t_out = SEQ if last else T_ALL
        b_mix = _attn_call(q, k, v, t_out)
        x1 = _mixout_call(xall, z, a_mix, b_mix, mods, p, bsz, t_out, 1024 if last else 768)
        j = l // 2
        if l % 2 == 0:
            assert not last
            xall = _ffn_call(x1, mods, norm2_g[l][None, :], ffn_w1[j].astype(BF16), ffn_w3[j].astype(BF16),
                             ffn_w2[j].astype(BF16), bsz)
        else:
            assert last
            xall = _moe_layer(x1, mods, norm2_g[l][None, :], moe_router_w[j], moe_router_b[j],
                              moe_w1[j].astype(BF16), moe_w3[j].astype(BF16), moe_w2[j].astype(BF16))
    return xall
```

```python
import functools

import jax
import jax.numpy as jnp
from jax import lax
from jax.experimental import pallas as pl
from jax.experimental.pallas import tpu as pltpu

F32 = jnp.float32
BF16 = jnp.bfloat16
I32 = jnp.int32
HIGHEST = lax.Precision.HIGHEST

D_MODEL = 1024
SEQ = 2048
CTX_LEN = 256
T_ALL = SEQ + CTX_LEN
DEPTH = 2
GRID_W = 64
EPS = 1e-6
MLSTM_HEADS = 4
MLSTM_HEAD_DIM = 64
MLSTM_WIDTH = 256
CHUNK = 128
MLA_HEADS = 8
MLA_Q_RANK = 256
MLA_KV_RANK = 128
MLA_NOPE = 64
MLA_ROPE = 32
MLA_V = 64
MLA_QK = 96
MLA_WIDTH = 512
ROPE_BASE = 10000.0
GMLP_GROUPS = 4
GMLP_GROUP_DIM = 64
GMLP_WIDTH = 256
D_FF = 2816
N_EXPERTS = 8
TOP_K = 2
D_FF_EXPERT = 3584
MOE_BLOCK = 512
OFF_GA = 4 * MLSTM_WIDTH
OFF_CQ = OFF_GA + 4 * MLSTM_HEADS
OFF_CKV = OFF_CQ + MLA_Q_RANK
OFF_KR = OFF_CKV + MLA_KV_RANK
OFF_GM = OFF_KR + MLA_ROPE
IN_COLS = OFF_GM + 2 * GMLP_WIDTH

LANES = 128
HEAD_PAD = LANES
ZC_B = 0
ZC_GM = 512
ZC_QV = 1024
ZC_O = 1536
ZC_G = 1792
Z_COLS = ZC_G + 2 * LANES
GATE_ROWS = 8
VMEM_LIMIT = 56 * 1024 * 1024

N_CHUNKS = T_ALL // CHUNK
N_LAT_CHUNKS = SEQ // CHUNK


def _cparams(sem, vmem=VMEM_LIMIT):
    return pltpu.CompilerParams(dimension_semantics=sem, vmem_limit_bytes=vmem)


def _rms(x, g):
    return x * lax.rsqrt(jnp.mean(x * x, axis=-1, keepdims=True) + EPS) * g


def _silu(x):
    return x * jax.nn.sigmoid(x)


def _modulated_norm(x, g, shift, scale):
    return _rms(x, g) * (1.0 + scale) + shift


def _row_mods(ml_ref, mc_ref, first, tile_start, rows):
    row = tile_start + lax.broadcasted_iota(I32, (rows, 1), 0)
    is_ctx = row >= SEQ
    return tuple(jnp.where(is_ctx, mc_ref[0, first + k:first + k + 1, :], ml_ref[0, first + k:first + k + 1, :])
                 for k in range(3))


def _ada_kernel(c_ref, w_ref, b_ref, o_ref):
    s = _silu(c_ref[...]).astype(BF16)
    o_ref[0] = jnp.dot(s, w_ref[0], preferred_element_type=F32) + b_ref[0]


def _ada_call(cvec, ada_w, ada_b):
    rows = cvec.shape[0]
    return pl.pallas_call(
        _ada_kernel,
        grid=(DEPTH, 6),
        in_specs=[pl.BlockSpec((rows, D_MODEL), lambda l, j: (0, 0)),
                  pl.BlockSpec((1, D_MODEL, D_MODEL), lambda l, j: (l, 0, j)),
                  pl.BlockSpec((1, 1, D_MODEL), lambda l, j: (l, 0, j))],
        out_specs=pl.BlockSpec((1, rows, D_MODEL), lambda l, j: (l, 0, j)),
        out_shape=jax.ShapeDtypeStruct((DEPTH, rows, 6 * D_MODEL), F32),
        compiler_params=_cparams(("arbitrary", "arbitrary")),
        name="adaln",
    )(cvec, ada_w, ada_b)


def _inproj_kernel(x_ref, ml_ref, mc_ref, g_ref, w_ref, wk_ref, wg_ref, z_ref, zgc_ref, zk_ref, zg_ref, *, tr):
    shift, scale, _ = _row_mods(ml_ref, mc_ref, 0, pl.program_id(1) * tr, tr)
    xn = _modulated_norm(x_ref[0], g_ref[...], shift, scale).astype(BF16)
    z = jnp.dot(xn, w_ref[...], preferred_element_type=F32)
    z_ref[0] = z[:, 0:ZC_G].astype(z_ref.dtype)
    zgc_ref[0] = z[:, ZC_G:Z_COLS]
    nt = (((1,), (1,)), ((), ()))
    zk_ref[0] = lax.dot_general(wk_ref[...], xn, nt, preferred_element_type=F32).astype(zk_ref.dtype)
    zg_ref[0] = lax.dot_general(wg_ref[...], xn, nt, preferred_element_type=F32)


def _inproj_call(xall, mods, norm_g, w_main, w_kt, w_gt, n_ctx_row):
    bsz = xall.shape[0]
    tr = 768
    const = lambda b, i: (0, 0)
    return pl.pallas_call(
        functools.partial(_inproj_kernel, tr=tr),
        grid=(bsz, T_ALL // tr),
        in_specs=[pl.BlockSpec((1, tr, D_MODEL), lambda b, i: (b, i, 0)),
                  pl.BlockSpec((1, 6, D_MODEL), lambda b, i: (b, 0, 0)),
                  pl.BlockSpec((1, 6, D_MODEL), lambda b, i: (n_ctx_row, 0, 0)),
                  pl.BlockSpec((1, D_MODEL), const),
                  pl.BlockSpec((D_MODEL, Z_COLS), const),
                  pl.BlockSpec((MLSTM_WIDTH, D_MODEL), const),
                  pl.BlockSpec((2 * GATE_ROWS, D_MODEL), const)],
        out_specs=[pl.BlockSpec((1, tr, ZC_G), lambda b, i: (b, i, 0)),
                   pl.BlockSpec((1, tr, Z_COLS - ZC_G), lambda b, i: (b, i, 0)),
                   pl.BlockSpec((1, MLSTM_WIDTH, tr), lambda b, i: (b, 0, i)),
                   pl.BlockSpec((1, 2 * GATE_ROWS, tr), lambda b, i: (b, 0, i))],
        out_shape=[jax.ShapeDtypeStruct((bsz, T_ALL, ZC_G), BF16),
                   jax.ShapeDtypeStruct((bsz, T_ALL, Z_COLS - ZC_G), F32),
                   jax.ShapeDtypeStruct((bsz, MLSTM_WIDTH, T_ALL), BF16),
                   jax.ShapeDtypeStruct((bsz, 2 * GATE_ROWS, T_ALL), F32)],
        compiler_params=_cparams(("parallel", "parallel")),
        name="in_proj",
    )(xall, mods, mods, norm_g, w_main, w_kt, w_gt)


def _scan_chunk(d, j):
    fwd_chunk = (j + N_LAT_CHUNKS) % N_CHUNKS
    return jnp.where(d == 0, fwd_chunk, N_CHUNKS - 1 - j)


MLSTM_ROWS_PER_STEP = 4


def _log_sigmoid(x):
    return jnp.minimum(x, 0.0) - jnp.log1p(jnp.exp(-jnp.abs(x)))


def _mlstm_kernel(qv_ref, og_ref, zg_ref, kt_ref, gt_ref, gb_ref, gbt_ref, ng_ref, o_ref, c_ref, m_ref, hf_ref):
    L = CHUNK
    E = MLSTM_HEAD_DIM
    d = pl.program_id(1)
    j = pl.program_id(2)
    fwd = d == 0
    row0 = pl.multiple_of(_scan_chunk(d, j) * L, L)

    @pl.when(j == 0)
    def _():
        c_ref[...] = jnp.zeros_like(c_ref)
        m_ref[...] = jnp.zeros_like(m_ref)

    r_i = lax.broadcasted_iota(I32, (L, L), 0)
    c_i = lax.broadcasted_iota(I32, (L, L), 1)
    prec = jnp.logical_or(jnp.logical_and(fwd, c_i <= r_i), jnp.logical_and(jnp.logical_not(fwd), c_i >= r_i))
    succ = jnp.logical_or(jnp.logical_and(fwd, r_i <= c_i), jnp.logical_and(jnp.logical_not(fwd), r_i >= c_i))
    lane = lax.broadcasted_iota(I32, (L, LANES), 1)
    sub = lax.broadcasted_iota(I32, (LANES, L), 0)
    low_lane = lane < E
    low_sub = sub < E

    row_idx = lax.broadcasted_iota(I32, (L, LANES), 0)
    rows = range(qv_ref.shape[0])
    pairs = range(MLSTM_HEADS // 2)
    h_pairs = {}
    for bb in rows:
        gates_c = zg_ref[bb] + gb_ref[0]
        b_c = jnp.dot(prec.astype(F32), _log_sigmoid(gates_c), precision=HIGHEST, preferred_element_type=F32)
        c_c = pltpu.roll(gates_c, LANES - MLSTM_HEADS, 1) - b_c
        m_row = m_ref[bb]
        cm = c_c
        for sh in (1, 2, 4, 8, 16, 32, 64):
            prev = jnp.where(fwd, jnp.where(row_idx >= sh, pltpu.roll(cm, sh, 0), -jnp.inf),
                             jnp.where(row_idx < L - sh, pltpu.roll(cm, L - sh, 0), -jnp.inf))
            cm = jnp.maximum(cm, prev)
        m_c = jnp.maximum(cm, m_row)
        a_c = jnp.exp(m_row - m_c)
        floor_c = jnp.exp(-(b_c + m_c))
        m_last = jnp.maximum(jnp.max(c_c, axis=0, keepdims=True), m_row)
        b_last = jnp.where(fwd, b_c[L - 1:L, :], b_c[0:1, :])
        m_ref[bb] = b_last + m_last
        decay_row = jnp.exp(m_row - m_last)
        gates_r = gt_ref[bb] + gbt_ref[0]
        b_r = jnp.dot(_log_sigmoid(gates_r), succ.astype(F32), precision=HIGHEST, preferred_element_type=F32)
        c_r = gates_r[MLSTM_HEADS:2 * MLSTM_HEADS, :] - b_r[0:MLSTM_HEADS, :]
        w_r = [jnp.exp(c_r[h:h + 1, :] - m_last[:, h:h + 1]) for h in range(MLSTM_HEADS)]

        for pj in pairs:
            tile = slice(pj * LANES, (pj + 1) * LANES)
            q_t = qv_ref[bb, :, tile]
            v_t = qv_ref[bb, :, MLSTM_WIDTH + pj * LANES:MLSTM_WIDTH + (pj + 1) * LANES]
            kt_t = kt_ref[bb, tile, :]
            c_old = c_ref[bb, pj]
            c_old_b = c_old.astype(BF16)
            kt_b = kt_t.astype(BF16)
            hx = []
            upd = []
            for par in range(2):
                h = 2 * pj + par
                own_lane = low_lane if par == 0 else jnp.logical_not(low_lane)
                own_sub = low_sub if par == 0 else jnp.logical_not(low_sub)
                den_lane = E if par == 0 else 0
                qm = jnp.where(own_lane, q_t, 0.0).astype(BF16)
                v_ext = jnp.where(own_lane, v_t, jnp.where(lane == den_lane, 1.0, 0.0)).astype(BF16)
                sqk = jnp.dot(qm, kt_b, preferred_element_type=F32)
                q_c = jnp.dot(qm, c_old_b, preferred_element_type=F32)
                dmat = jnp.where(prec, jnp.exp(c_r[h:h + 1, :] - m_c[:, h:h + 1]), 0.0)
                pv = jnp.dot((sqk * dmat).astype(BF16), v_ext, preferred_element_type=F32)
                nd = pv + a_c[:, h:h + 1] * q_c
                den = nd[:, den_lane:den_lane + 1]
                hx.append(nd / jnp.maximum(jnp.abs(den), floor_c[:, h:h + 1]))
                kw_t = (jnp.where(own_sub, kt_t, 0.0) * w_r[h]).astype(BF16)
                upd.append(jnp.dot(kw_t, v_ext, preferred_element_type=F32))
            decay = jnp.where(low_sub, decay_row[:, 2 * pj:2 * pj + 1], decay_row[:, 2 * pj + 1:2 * pj + 2])
            c_ref[bb, pj] = decay * c_old + upd[0] + upd[1]
            h_pairs[bb, pj] = jnp.where(low_lane, hx[0], hx[1])

    @pl.when(fwd)
    def _():
        for (bb, pj), h_pair in h_pairs.items():
            hf_ref[bb, pl.ds(row0, L), pj * LANES:(pj + 1) * LANES] = h_pair

    @pl.when(jnp.logical_not(fwd))
    def _():
        for (bb, pj), h_pair in h_pairs.items():
            tile = slice(pj * LANES, (pj + 1) * LANES)
            hs = hf_ref[bb, pl.ds(row0, L), tile] + h_pair
            sq = hs * hs
            ss = jnp.where(low_lane,
                           jnp.sum(jnp.where(low_lane, sq, 0.0), axis=-1, keepdims=True),
                           jnp.sum(jnp.where(low_lane, 0.0, sq), axis=-1, keepdims=True))
            y = hs * lax.rsqrt(ss * (1.0 / E) + EPS) * ng_ref[:, tile]
            o_ref[bb, :, tile] = (jax.nn.sigmoid(og_ref[bb, :, tile].astype(F32)) * y).astype(o_ref.dtype)


def _mlstm_call(z, zgc, zkt, zgt, gate_b, gate_bt, norm_g):
    bsz = z.shape[0]
    nb = max(r for r in range(1, MLSTM_ROWS_PER_STEP + 1) if bsz % r == 0)
    last = N_CHUNKS - 1
    chunk = lambda b, d, j: _scan_chunk(d, j)
    return pl.pallas_call(
        _mlstm_kernel,
        grid=(bsz // nb, 2, N_CHUNKS),
        in_specs=[pl.BlockSpec((nb, CHUNK, 2 * MLSTM_WIDTH), lambda b, d, j: (b, chunk(b, d, j), ZC_QV // 512)),
                  pl.BlockSpec((nb, CHUNK, MLSTM_WIDTH), lambda b, d, j: (b, chunk(b, d, j), ZC_O // MLSTM_WIDTH)),
                  pl.BlockSpec((nb, CHUNK, LANES), lambda b, d, j: (b, chunk(b, d, j), d)),
                  pl.BlockSpec((nb, MLSTM_WIDTH, CHUNK), lambda b, d, j: (b, 0, chunk(b, d, j))),
                  pl.BlockSpec((nb, GATE_ROWS, CHUNK), lambda b, d, j: (b, d, chunk(b, d, j))),
                  pl.BlockSpec((1, 1, LANES), lambda b, d, j: (d, 0, 0)),
                  pl.BlockSpec((1, GATE_ROWS, LANES), lambda b, d, j: (d, 0, 0)),
                  pl.BlockSpec((1, MLSTM_WIDTH), lambda b, d, j: (0, 0))],
        out_specs=pl.BlockSpec((nb, CHUNK, MLSTM_WIDTH),
                               lambda b, d, j: (b, jnp.where(d == 0, last, last - j), 0)),
        out_shape=jax.ShapeDtypeStruct((bsz, T_ALL, MLSTM_WIDTH), BF16),
        scratch_shapes=[pltpu.VMEM((nb, MLSTM_HEADS // 2, LANES, LANES), F32),
                        pltpu.VMEM((nb, 1, LANES), F32),
                        pltpu.VMEM((nb, T_ALL, MLSTM_WIDTH), F32)],
        compiler_params=_cparams(("parallel", "arbitrary", "arbitrary")),
        name="mlstm",
    )(z, z, zgc, zkt, zgt, gate_b, gate_bt, norm_g)


ROPE_HALF = MLA_ROPE // 2


def _rope(t, cos_t, sin_t):
    return t * cos_t + pltpu.roll(t, LANES - ROPE_HALF, 1) * sin_t


def _head_rms(t, g, real):
    ss = jnp.sum(jnp.where(real, t * t, 0.0), axis=-1, keepdims=True)
    return t * lax.rsqrt(ss * (1.0 / MLA_QK) + EPS) * g


def _qkv_kernel(z_ref, cqg_ref, ckvg_ref, wq_ref, wk_ref, wv_ref, qg_ref, kg_ref, ct_ref, st_ref,
                q_ref, k_ref, v_ref, *, tr):
    cqn = _rms(z_ref[0, :, 0:MLA_Q_RANK].astype(F32), cqg_ref[...]).astype(BF16)
    ckvn = _rms(z_ref[0, :, MLA_Q_RANK:MLA_Q_RANK + MLA_KV_RANK].astype(F32), ckvg_ref[...]).astype(BF16)
    k_rope = z_ref[0, :, MLA_Q_RANK + MLA_KV_RANK:MLA_Q_RANK + MLA_KV_RANK + LANES].astype(F32)
    q_raw = jnp.dot(cqn, wq_ref[...], preferred_element_type=F32)
    k_raw = jnp.dot(ckvn, wk_ref[...], preferred_element_type=F32)
    v_raw = jnp.dot(ckvn, wv_ref[...], preferred_element_type=F32)
    cos_t = ct_ref[...]
    sin_t = st_ref[...]
    lane = lax.broadcasted_iota(I32, (tr, LANES), 1)
    real = lane < MLA_QK
    for h in range(MLA_HEADS):
        sl = slice(h * HEAD_PAD, (h + 1) * HEAD_PAD)
        qh = _rope(_head_rms(q_raw[:, sl], qg_ref[...], real), cos_t, sin_t)
        q_ref[0, :, sl] = (qh * (MLA_QK ** -0.5)).astype(BF16)
        kh = _rope(_head_rms(k_raw[:, sl] + k_rope, kg_ref[...], real), cos_t, sin_t)
        k_ref[0, :, sl] = kh.astype(BF16)
        v_ref[0, :, sl] = jnp.where(lane == MLA_V, 1.0, v_raw[:, sl]).astype(BF16)


def _qkv_call(z, p, rope_tabs):
    bsz = z.shape[0]
    tr = 768
    hw = MLA_HEADS * HEAD_PAD
    const = lambda b, i: (0, 0)
    out = jax.ShapeDtypeStruct((bsz, T_ALL, hw), BF16)
    tab = pl.BlockSpec((tr, LANES), lambda b, i: (i, 0))
    ospec = pl.BlockSpec((1, tr, hw), lambda b, i: (b, i, 0))
    return pl.pallas_call(
        functools.partial(_qkv_kernel, tr=tr),
        grid=(bsz, T_ALL // tr),
        in_specs=[pl.BlockSpec((1, tr, 512), lambda b, i: (b, i, ZC_B // 512)),
                  pl.BlockSpec((1, MLA_Q_RANK), const), pl.BlockSpec((1, MLA_KV_RANK), const),
                  pl.BlockSpec((MLA_Q_RANK, hw), const), pl.BlockSpec((MLA_KV_RANK, hw), const),
                  pl.BlockSpec((MLA_KV_RANK, hw), const),
                  pl.BlockSpec((1, LANES), const), pl.BlockSpec((1, LANES), const),
                  tab, tab],
        out_specs=[ospec, ospec, ospec],
        out_shape=[out, out, out],
        compiler_params=_cparams(("parallel", "parallel")),
        name="mla_qkv",
    )(z, p["cq_g"], p["ckv_g"], p["w_uq"], p["w_k"], p["w_v"], p["q_g"], p["k_g"], *rope_tabs)


ATT_TQ = 512
assert SEQ % ATT_TQ == 0 and CTX_LEN <= ATT_TQ


def _attn_kernel(q_ref, k_ref, v_ref, o_ref):
    i = pl.program_id(1)

    def run(rows, k0):
        def scores(hh):
            sl = slice(hh * HEAD_PAD, (hh + 1) * HEAD_PAD)
            return lax.dot_general(q_ref[0, 0:rows, sl], k_ref[0, k0:T_ALL, sl], (((1,), (1,)), ((), ())),
                                   preferred_element_type=F32)

        s = scores(0)
        for hh in range(MLA_HEADS):
            s_next = scores(hh + 1) if hh + 1 < MLA_HEADS else None
            p = jnp.exp(s - jnp.max(s, axis=-1, keepdims=True)).astype(BF16)
            oe = jnp.dot(p, v_ref[0, k0:T_ALL, hh * HEAD_PAD:(hh + 1) * HEAD_PAD], preferred_element_type=F32)
            o_ref[0, 0:rows, hh * MLA_V:(hh + 1) * MLA_V] = (
                oe[:, 0:MLA_V] / oe[:, MLA_V:MLA_V + 1]).astype(o_ref.dtype)
            s = s_next

    @pl.when(i < SEQ // ATT_TQ)
    def _():
        run(ATT_TQ, 0)

    @pl.when(i >= SEQ // ATT_TQ)
    def _():
        run(CTX_LEN, SEQ)
        o_ref[0, CTX_LEN:ATT_TQ, :] = jnp.zeros((ATT_TQ - CTX_LEN, MLA_WIDTH), o_ref.dtype)


def _attn_call(q, k, v, t_out):
    bsz = q.shape[0]
    w = MLA_HEADS * HEAD_PAD
    return pl.pallas_call(
        _attn_kernel,
        grid=(bsz, pl.cdiv(t_out, ATT_TQ)),
        in_specs=[pl.BlockSpec((1, ATT_TQ, w), lambda b, i: (b, i, 0)),
                  pl.BlockSpec((1, T_ALL, w), lambda b, i: (b, 0, 0)),
                  pl.BlockSpec((1, T_ALL, w), lambda b, i: (b, 0, 0))],
        out_specs=pl.BlockSpec((1, ATT_TQ, MLA_WIDTH), lambda b, i: (b, i, 0)),
        out_shape=jax.ShapeDtypeStruct((bsz, t_out, MLA_WIDTH), BF16),
        compiler_params=_cparams(("parallel", "arbitrary")),
        name="mla_attn",
    )(q, k, v)


def _mixout_kernel(x_ref, zg_ref, a_ref, b_ref, ml_ref, mc_ref, lng_ref, lnb_ref, ws_ref, bs_ref, wo_ref,
                   o_ref, cm_ref, *, tr):
    gd = GMLP_GROUP_DIM
    act = jax.nn.gelu(zg_ref[0].astype(F32))
    u = act[:, 0:GMLP_WIDTH]
    vv = act[:, GMLP_WIDTH:2 * GMLP_WIDTH]
    mu = jnp.mean(vv, axis=-1, keepdims=True)
    var = jnp.mean(jnp.square(vv - mu), axis=-1, keepdims=True)
    vn = ((vv - mu) * lax.rsqrt(var + EPS) * lng_ref[...] + lnb_ref[...]).astype(BF16)
    for c in range(tr // CHUNK):
        rows = slice(c * CHUNK, (c + 1) * CHUNK)
        for g in range(GMLP_GROUPS):
            cols = slice(g * gd, (g + 1) * gd)
            sv = jnp.dot(ws_ref[g], vn[rows, cols], preferred_element_type=F32) + bs_ref[:, cols]
            cm_ref[rows, cols] = u[rows, cols] * sv
    y = jnp.dot(a_ref[0].astype(BF16), wo_ref[0:MLSTM_WIDTH, :], preferred_element_type=F32)
    y += jnp.dot(b_ref[0].astype(BF16), wo_ref[MLSTM_WIDTH:MLSTM_WIDTH + MLA_WIDTH, :],
                 preferred_element_type=F32)
    y += jnp.dot(cm_ref[...].astype(BF16), wo_ref[MLSTM_WIDTH + MLA_WIDTH:, :], preferred_element_type=F32)
    _, _, gate = _row_mods(ml_ref, mc_ref, 0, pl.program_id(1) * tr, tr)
    o_ref[0] = x_ref[0] + gate * y


def _mixout_call(xall, z, a_mix, b_mix, mods, p, n_ctx_row, t_out, tr):
    bsz = xall.shape[0]
    const = lambda b, i: (0, 0)
    return pl.pallas_call(
        functools.partial(_mixout_kernel, tr=tr),
        grid=(bsz, t_out // tr),
        in_specs=[pl.BlockSpec((1, tr, D_MODEL), lambda b, i: (b, i, 0)),
                  pl.BlockSpec((1, tr, 2 * GMLP_WIDTH), lambda b, i: (b, i, ZC_GM // (2 * GMLP_WIDTH))),
                  pl.BlockSpec((1, tr, MLSTM_WIDTH), lambda b, i: (b, i, 0)),
                  pl.BlockSpec((1, tr, MLA_WIDTH), lambda b, i: (b, i, 0)),
                  pl.BlockSpec((1, 6, D_MODEL), lambda b, i: (b, 0, 0)),
                  pl.BlockSpec((1, 6, D_MODEL), lambda b, i: (n_ctx_row, 0, 0)),
                  pl.BlockSpec((1, GMLP_WIDTH), const), pl.BlockSpec((1, GMLP_WIDTH), const),
                  pl.BlockSpec((GMLP_GROUPS, CHUNK, CHUNK), lambda b, i: (0, 0, 0)),
                  pl.BlockSpec((CHUNK, GMLP_WIDTH), const),
                  pl.BlockSpec((D_MODEL, D_MODEL), const)],
        out_specs=pl.BlockSpec((1, tr, D_MODEL), lambda b, i: (b, i, 0)),
        out_shape=jax.ShapeDtypeStruct((bsz, t_out, D_MODEL), F32),
        scratch_shapes=[pltpu.VMEM((tr, GMLP_WIDTH), F32)],
        compiler_params=_cparams(("parallel", "parallel")),
        name="mix_out",
    )(xall, z, a_mix, b_mix, mods, mods, p["ln_g"], p["ln_b"], p["w_s"], p["b_s"], p["w_out"])


FFN_SPLIT = 2


def _ffn_kernel(x_ref, ml_ref, mc_ref, g_ref, w1_ref, w3_ref, w2_ref, o_ref, *, tr):
    shift, scale, gate = _row_mods(ml_ref, mc_ref, 3, pl.program_id(1) * tr, tr)
    x = x_ref[0]
    h = _modulated_norm(x, g_ref[...], shift, scale).astype(BF16)
    fc = D_FF // FFN_SPLIT
    y = jnp.zeros((tr, D_MODEL), F32)
    for f in range(FFN_SPLIT):
        cols = slice(f * fc, (f + 1) * fc)
        h1 = jnp.dot(h, w1_ref[:, cols], preferred_element_type=F32)
        h3 = jnp.dot(h, w3_ref[:, cols], preferred_element_type=F32)
        y += jnp.dot((_silu(h1) * h3).astype(BF16), w2_ref[cols, :], preferred_element_type=F32)
    o_ref[0] = x + gate * y


def _ffn_call(x1, mods, norm_g, w1, w3, w2, n_ctx_row):
    bsz = x1.shape[0]
    tr = 768
    const = lambda b, i: (0, 0)
    resident = pl.Buffered(1)
    return pl.pallas_call(
        functools.partial(_ffn_kernel, tr=tr),
        grid=(bsz, T_ALL // tr),
        in_specs=[pl.BlockSpec((1, tr, D_MODEL), lambda b, i: (b, i, 0)),
                  pl.BlockSpec((1, 6, D_MODEL), lambda b, i: (b, 0, 0)),
                  pl.BlockSpec((1, 6, D_MODEL), lambda b, i: (n_ctx_row, 0, 0)),
                  pl.BlockSpec((1, D_MODEL), const),
                  pl.BlockSpec((D_MODEL, D_FF), const, pipeline_mode=resident),
                  pl.BlockSpec((D_MODEL, D_FF), const, pipeline_mode=resident),
                  pl.BlockSpec((D_FF, D_MODEL), const, pipeline_mode=resident)],
        out_specs=pl.BlockSpec((1, tr, D_MODEL), lambda b, i: (b, i, 0)),
        out_shape=jax.ShapeDtypeStruct((bsz, T_ALL, D_MODEL), F32),
        compiler_params=_cparams(("parallel", "parallel")),
        name="dense_ffn",
    )(x1, mods, mods, norm_g, w1, w3, w2)


ROUTE_TR = 1024


def _router_kernel(x_ref, ml_ref, g_ref, rw_ref, rb_ref, h_ref, e_ref, r_ref, gt_ref, cnt_ref, run_ref, *, tr):
    step = pl.program_id(0) * pl.num_programs(1) + pl.program_id(1)

    @pl.when(step == 0)
    def _():
        run_ref[...] = jnp.zeros_like(run_ref)

    h = _modulated_norm(x_ref[0], g_ref[...], ml_ref[0, 3:4, :], ml_ref[0, 4:5, :])
    h_ref[...] = h
    lane = lax.broadcasted_iota(I32, (tr, LANES), 1)
    logits = jnp.dot(h, rw_ref[...], precision=HIGHEST, preferred_element_type=F32) + rb_ref[...]
    logits = jnp.where(lane < N_EXPERTS, logits, -jnp.inf)
    m1 = jnp.max(logits, axis=-1, keepdims=True)
    e1 = jnp.min(jnp.where(logits == m1, lane, LANES), axis=-1, keepdims=True)
    rest = jnp.where(lane == e1, -jnp.inf, logits)
    m2 = jnp.max(rest, axis=-1, keepdims=True)
    e2 = jnp.min(jnp.where(rest == m2, lane, LANES), axis=-1, keepdims=True)
    ex = jnp.exp(m2 - m1)
    g1 = 1.0 / (1.0 + ex)
    g2 = ex / (1.0 + ex)
    onehot = jnp.logical_or(lane == e1, lane == e2 + N_EXPERTS)
    oh = onehot.astype(F32)
    r_i = lax.broadcasted_iota(I32, (tr, tr), 0)
    c_i = lax.broadcasted_iota(I32, (tr, tr), 1)
    before = jnp.dot((c_i < r_i).astype(BF16), oh.astype(BF16), preferred_element_type=F32)
    tot = jnp.sum(oh, axis=0, keepdims=True)
    tot0_shift = pltpu.roll(tot, N_EXPERTS, 1)
    run = run_ref[...]
    first_half = lax.broadcasted_iota(I32, (1, LANES), 1) < N_EXPERTS
    offs = run + jnp.where(first_half, 0.0, tot0_shift)
    ranks = oh * (before + offs)
    rank1 = jnp.sum(jnp.where(lane < N_EXPERTS, ranks, 0.0), axis=-1, keepdims=True)
    rank2 = jnp.sum(jnp.where(lane >= N_EXPERTS, ranks, 0.0), axis=-1, keepdims=True)
    col = lax.broadcasted_iota(I32, (tr, TOP_K), 1)
    e_ref[...] = jnp.where(col == 0, e1, e2)
    r_ref[...] = jnp.where(col == 0, rank1, rank2).astype(I32)
    gt_ref[...] = jnp.where(col == 0, g1, g2)
    both = tot + jnp.where(first_half, pltpu.roll(tot, LANES - N_EXPERTS, 1), tot0_shift)
    new_run = run + both
    run_ref[...] = new_run
    cnt_ref[...] = new_run.astype(I32)


def _router_call(x1, mods, norm_g, rw_p, rb_p):
    bsz = x1.shape[0]
    tr = ROUTE_TR
    n = bsz * SEQ
    nt = SEQ // tr
    const = lambda b, i: (0, 0)
    tok = lambda b, i: (b * nt + i, 0)
    return pl.pallas_call(
        functools.partial(_router_kernel, tr=tr),
        grid=(bsz, nt),
        in_specs=[pl.BlockSpec((1, tr, D_MODEL), lambda b, i: (b, i, 0)),
                  pl.BlockSpec((1, 6, D_MODEL), lambda b, i: (b, 0, 0)),
                  pl.BlockSpec((1, D_MODEL), const),
                  pl.BlockSpec((D_MODEL, LANES), const),
                  pl.BlockSpec((1, LANES), const)],
        out_specs=[pl.BlockSpec((tr, D_MODEL), tok),
                   pl.BlockSpec((tr, TOP_K), tok), pl.BlockSpec((tr, TOP_K), tok), pl.BlockSpec((tr, TOP_K), tok),
                   pl.BlockSpec((1, LANES), const)],
        out_shape=[jax.ShapeDtypeStruct((n, D_MODEL), F32),
                   jax.ShapeDtypeStruct((n, TOP_K), I32), jax.ShapeDtypeStruct((n, TOP_K), I32),
                   jax.ShapeDtypeStruct((n, TOP_K), F32),
                   jax.ShapeDtypeStruct((1, LANES), I32)],
        scratch_shapes=[pltpu.VMEM((1, LANES), F32)],
        compiler_params=_cparams(("arbitrary", "arbitrary")),
        name="moe_router",
    )(x1, mods, norm_g, rw_p, rb_p)


DISPATCH_TD = 512
DMA_ISSUE_UNROLL = 8


def _row_copy(src, src_row, dst, dst_row, sem):
    return pltpu.make_async_copy(src.at[pl.ds(src_row, 1)], dst.at[pl.ds(dst_row, 1)], sem)


def _dispatch_kernel(dest_ref, h_ref, xs_in, xs_out, sem):
    del xs_in

    def issue(r, carry):
        for kk in range(TOP_K):
            _row_copy(h_ref, r, xs_out, dest_ref[kk, r], sem).start()
        return carry

    lax.fori_loop(0, DISPATCH_TD, issue, 0, unroll=DMA_ISSUE_UNROLL)

    for kk in range(TOP_K):
        pltpu.make_async_copy(h_ref, xs_out.at[pl.ds(0, DISPATCH_TD)], sem).wait()


def _dispatch_call(dest_t, h2, cap):
    n = h2.shape[0]
    xs0 = jnp.zeros((cap, D_MODEL), F32)
    return pl.pallas_call(
        _dispatch_kernel,
        grid=(n // DISPATCH_TD,),
        in_specs=[pl.BlockSpec((TOP_K, DISPATCH_TD), lambda i: (0, i), memory_space=pltpu.SMEM),
                  pl.BlockSpec((DISPATCH_TD, D_MODEL), lambda i: (i, 0)),
                  pl.BlockSpec(memory_space=pl.ANY)],
        out_specs=pl.BlockSpec(memory_space=pl.ANY),
        out_shape=jax.ShapeDtypeStruct((cap, D_MODEL), F32),
        scratch_shapes=[pltpu.SemaphoreType.DMA(())],
        input_output_aliases={2: 0},
        compiler_params=_cparams(("arbitrary",)),
        name="moe_dispatch",
    )(dest_t, h2, xs0)


EXPERT_FSPLIT = 2


def _expert_kernel(be_ref, na_ref, xs_ref, w1_ref, w3_ref, w2_ref, ys_ref):
    i = pl.program_id(0)
    f = pl.program_id(1)
    active = i < na_ref[0]

    @pl.when(jnp.logical_and(active, f == 0))
    def _():
        ys_ref[...] = jnp.zeros_like(ys_ref)

    @pl.when(active)
    def _():
        x = xs_ref[...].astype(BF16)
        h1 = jnp.dot(x, w1_ref[0], preferred_element_type=F32)
        h3 = jnp.dot(x, w3_ref[0], preferred_element_type=F32)
        ys_ref[...] += jnp.dot((_silu(h1) * h3).astype(BF16), w2_ref[0], preferred_element_type=F32)

    @pl.when(jnp.logical_not(active))
    def _():
        ys_ref[...] = jnp.zeros_like(ys_ref)


def _expert_call(block_expert, n_active, xs, w1, w3, w2):
    cap = xs.shape[0]
    nb = cap // MOE_BLOCK
    fc = D_FF_EXPERT // EXPERT_FSPLIT
    last_f = EXPERT_FSPLIT - 1

    def blk(i, na):
        return jnp.minimum(i, jnp.maximum(na[0] - 1, 0))

    def fidx(i, f, na):
        return jnp.where(i < na[0], f, last_f)

    grid_spec = pltpu.PrefetchScalarGridSpec(
        num_scalar_prefetch=2,
        grid=(nb, EXPERT_FSPLIT),
        in_specs=[pl.BlockSpec((MOE_BLOCK, D_MODEL), lambda i, f, be, na: (blk(i, na), 0)),
                  pl.BlockSpec((1, D_MODEL, fc), lambda i, f, be, na: (be[blk(i, na)], 0, fidx(i, f, na))),
                  pl.BlockSpec((1, D_MODEL, fc), lambda i, f, be, na: (be[blk(i, na)], 0, fidx(i, f, na))),
                  pl.BlockSpec((1, fc, D_MODEL), lambda i, f, be, na: (be[blk(i, na)], fidx(i, f, na), 0))],
        out_specs=pl.BlockSpec((MOE_BLOCK, D_MODEL), lambda i, f, be, na: (i, 0)),
    )
    return pl.pallas_call(
        _expert_kernel,
        grid_spec=grid_spec,
        out_shape=jax.ShapeDtypeStruct((cap, D_MODEL), F32),
        compiler_params=_cparams(("arbitrary", "arbitrary")),
        name="moe_experts",
    )(block_expert, n_active, xs, w1, w3, w2)


COMBINE_TC = 256


def _combine_kernel(dest_ref, dest_next_ref, x_ref, gt_ref, ml_ref, ys_hbm, o_ref, buf_ref, sem):
    i = pl.program_id(0)
    slot = i % 2

    def gather(d_ref, s):
        def issue(r, carry):
            for kk in range(TOP_K):
                _row_copy(ys_hbm, d_ref[kk, r], buf_ref.at[s, kk], r, sem.at[s]).start()
            return carry

        lax.fori_loop(0, COMBINE_TC, issue, 0, unroll=DMA_ISSUE_UNROLL)

    @pl.when(i == 0)
    def _():
        gather(dest_ref, 0)

    @pl.when(i + 1 < pl.num_programs(0))
    def _():
        gather(dest_next_ref, 1 - slot)

    for kk in range(TOP_K):
        pltpu.make_async_copy(ys_hbm.at[pl.ds(0, COMBINE_TC)], buf_ref.at[slot, kk], sem.at[slot]).wait()
    g = gt_ref[...]
    y = buf_ref[slot, 0] * g[:, 0:1] + buf_ref[slot, 1] * g[:, 1:2]
    o_ref[...] = x_ref[...] + ml_ref[0, 5:6, :] * y


def _combine_call(dest_t, x1_flat, gates, mods, ys):
    n = x1_flat.shape[0]
    tc = COMBINE_TC
    per_batch = SEQ // tc
    last = n // tc - 1
    return pl.pallas_call(
        _combine_kernel,
        grid=(n // tc,),
        in_specs=[pl.BlockSpec((TOP_K, tc), lambda i: (0, i), memory_space=pltpu.SMEM),
                  pl.BlockSpec((TOP_K, tc), lambda i: (0, jnp.minimum(i + 1, last)), memory_space=pltpu.SMEM),
                  pl.BlockSpec((tc, D_MODEL), lambda i: (i, 0)),
                  pl.BlockSpec((tc, TOP_K), lambda i: (i, 0)),
                  pl.BlockSpec((1, 6, D_MODEL), lambda i: (i // per_batch, 0, 0)),
                  pl.BlockSpec(memory_space=pl.ANY)],
        out_specs=pl.BlockSpec((tc, D_MODEL), lambda i: (i, 0)),
        out_shape=jax.ShapeDtypeStruct((n, D_MODEL), F32),
        scratch_shapes=[pltpu.VMEM((2, TOP_K, tc, D_MODEL), F32), pltpu.SemaphoreType.DMA((2,))],
        compiler_params=_cparams(("arbitrary",)),
        name="moe_combine",
    )(dest_t, dest_t, x1_flat, gates, mods, ys)


def _moe_layer(x1, mods, norm_g, router_w, router_b, w1, w3, w2):
    bsz = x1.shape[0]
    n = bsz * SEQ
    cap = (n * TOP_K + MOE_BLOCK - 1) // MOE_BLOCK * MOE_BLOCK + N_EXPERTS * MOE_BLOCK
    rw_p = jnp.zeros((D_MODEL, LANES), F32).at[:, :N_EXPERTS].set(router_w)
    rb_p = jnp.zeros((1, LANES), F32).at[0, :N_EXPERTS].set(router_b)
    h2, e_idx, rank, gates, counts = _router_call(x1, mods, norm_g, rw_p, rb_p)
    counts = counts[0, :N_EXPERTS]
    padded = (counts + MOE_BLOCK - 1) // MOE_BLOCK * MOE_BLOCK
    padded_end = jnp.cumsum(padded)
    base = padded_end - padded
    dest_t = (base[e_idx] + rank).T.astype(I32)
    n_blocks = cap // MOE_BLOCK
    block_start = jnp.arange(n_blocks, dtype=I32) * MOE_BLOCK
    block_expert = jnp.minimum(jnp.sum(padded_end[None, :] <= block_start[:, None], axis=1),
                               N_EXPERTS - 1).astype(I32)
    n_active = (padded_end[-1:] // MOE_BLOCK).astype(I32)
    xs = _dispatch_call(dest_t, h2, cap)
    ys = _expert_call(block_expert, n_active, xs, w1, w3, w2)
    out = _combine_call(dest_t, x1.reshape(n, D_MODEL), gates, mods, ys)
    return out.reshape(bsz, SEQ, D_MODEL)


def _rope_tables():
    rows = SEQ // GRID_W
    row = jnp.repeat(jnp.arange(rows), GRID_W).astype(F32)
    col = jnp.tile(jnp.arange(GRID_W), rows).astype(F32)
    n_freq = MLA_ROPE // 4
    inv = ROPE_BASE ** (-jnp.arange(n_freq, dtype=F32) / n_freq)
    ang = jnp.concatenate([row[:, None] * inv, col[:, None] * inv], axis=-1)
    cos, sin = jnp.cos(ang), jnp.sin(ang)
    ones = jnp.ones((SEQ, MLA_NOPE), F32)
    pad = jnp.zeros((SEQ, LANES - MLA_QK), F32)
    zn = jnp.zeros((SEQ, MLA_NOPE), F32)
    cos_t = jnp.concatenate([ones, cos, cos, pad], axis=-1)
    sin_t = jnp.concatenate([zn, -sin, sin, pad], axis=-1)
    ident = jnp.concatenate([jnp.ones((CTX_LEN, MLA_QK), F32), jnp.zeros((CTX_LEN, LANES - MLA_QK), F32)], axis=-1)
    zero = jnp.zeros((CTX_LEN, LANES), F32)
    return jnp.concatenate([cos_t, ident], axis=0), jnp.concatenate([sin_t, zero], axis=0)


def _rope_copy_pad(a):
    first_half = a[..., MLA_NOPE:MLA_NOPE + ROPE_HALF]
    zeros = jnp.zeros(a.shape[:-1] + (LANES - MLA_QK - ROPE_HALF,), a.dtype)
    return jnp.concatenate([a, first_half, zeros], axis=-1)


def _gate_cols(ga, d):
    h = MLSTM_HEADS
    return jnp.concatenate([ga[..., (2 + d) * h:(3 + d) * h], ga[..., d * h:(d + 1) * h]], axis=-1)


def _relayout_w_in(w_in):
    zeros = lambda n: jnp.zeros((D_MODEL, n), F32)
    w = MLSTM_WIDTH
    ga = w_in[:, OFF_GA:OFF_CQ]
    gate_tile = lambda d: jnp.concatenate([_gate_cols(ga, d), zeros(LANES - GATE_ROWS)], axis=-1)
    main = jnp.concatenate([
        w_in[:, OFF_CQ:OFF_KR],
        _rope_copy_pad(jnp.concatenate([zeros(MLA_NOPE), w_in[:, OFF_KR:OFF_GM]], axis=-1)),
        w_in[:, OFF_GM:IN_COLS],
        w_in[:, 0:w], w_in[:, 2 * w:3 * w],
        w_in[:, 3 * w:4 * w],
        gate_tile(0), gate_tile(1)], axis=-1)
    keys_t = (w_in[:, w:2 * w] * (MLSTM_HEAD_DIM ** -0.5)).T
    gates_t = jnp.concatenate([_gate_cols(ga, 0), _gate_cols(ga, 1)], axis=-1).T
    return main.astype(BF16), keys_t.astype(BF16), gates_t.astype(BF16)


def _relayout_gate_b(gb):
    per_dir = jnp.stack([_gate_cols(gb, 0), _gate_cols(gb, 1)])
    col_form = jnp.pad(per_dir, ((0, 0), (0, LANES - GATE_ROWS)))[:, None, :]
    row_form = jnp.broadcast_to(per_dir[:, :, None], (2, GATE_ROWS, LANES))
    return col_form, row_form


def _pad_heads(w, width):
    kdim = w.shape[0]
    w = w.reshape(kdim, MLA_HEADS, width)
    return jnp.pad(w, ((0, 0), (0, 0), (0, HEAD_PAD - width))).reshape(kdim, MLA_HEADS * HEAD_PAD)


def _layer_params(l, w_in, w_out, mlstm_gate_b, mlstm_norm_g, mla_cq_g, mla_ckv_g, mla_w_uq, mla_w_ukv,
                  mla_q_g, mla_k_g, gmlp_ln_g, gmlp_ln_b, gmlp_w_s, gmlp_b_s):
    ukv = mla_w_ukv[l].reshape(MLA_KV_RANK, MLA_HEADS, MLA_NOPE + MLA_V)
    pad1 = lambda g: _rope_copy_pad(g)[None, :]
    w_uq = _rope_copy_pad(mla_w_uq[l].reshape(MLA_Q_RANK, MLA_HEADS, MLA_QK)).reshape(MLA_Q_RANK, -1)
    return dict(
        w_in=_relayout_w_in(w_in[l]),
        gate_b=_relayout_gate_b(mlstm_gate_b[l]),
        mlstm_g=mlstm_norm_g[l][None, :],
        cq_g=mla_cq_g[l][None, :], ckv_g=mla_ckv_g[l][None, :],
        w_uq=w_uq.astype(BF16),
        w_k=_pad_heads(ukv[:, :, :MLA_NOPE].reshape(MLA_KV_RANK, -1), MLA_NOPE).astype(BF16),
        w_v=_pad_heads(ukv[:, :, MLA_NOPE:].reshape(MLA_KV_RANK, -1), MLA_V).astype(BF16),
        q_g=pad1(mla_q_g[l]), k_g=pad1(mla_k_g[l]),
        ln_g=gmlp_ln_g[l][None, :], ln_b=gmlp_ln_b[l][None, :],
        w_s=gmlp_w_s[l].astype(BF16),
        b_s=jnp.repeat(gmlp_b_s[l].T, GMLP_GROUP_DIM, axis=1),
        w_out=w_out[l].astype(BF16),
    )


def kernel(x, c, ctx, c_ctx, ada_w, ada_b, norm1_g, norm2_g, w_in, w_out, mlstm_gate_b, mlstm_norm_g, mla_cq_g, mla_ckv_g, mla_w_uq, mla_w_ukv, mla_q_g, mla_k_g, gmlp_ln_g, gmlp_ln_b, gmlp_w_s, gmlp_b_s, ffn_w1, ffn_w3, ffn_w2, moe_router_w, moe_router_b, moe_w1, moe_w3, moe_w2):
    bsz = x.shape[0]
    assert x.shape[1:] == (SEQ, D_MODEL) and ctx.shape[1:] == (CTX_LEN, D_MODEL)
    mod_rows = -(-(bsz + 1) // 8) * 8
    cvec = jnp.zeros((mod_rows, D_MODEL), F32).at[:bsz].set(c).at[bsz].set(c_ctx)
    mods_all = _ada_call(cvec, ada_w.astype(BF16), ada_b[:, None, :]).reshape(DEPTH, mod_rows, 6, D_MODEL)
    rope_tabs = _rope_tables()
    xall = jnp.concatenate([x, ctx], axis=1)
    for l in range(DEPTH):
        last = l == DEPTH - 1
        p = _layer_params(l, w_in, w_out, mlstm_gate_b, mlstm_norm_g, mla_cq_g, mla_ckv_g, mla_w_uq, mla_w_ukv,
                          mla_q_g, mla_k_g, gmlp_ln_g, gmlp_ln_b, gmlp_w_s, gmlp_b_s)
        mods = mods_all[l]
        z, zgc, zkt, zgt = _inproj_call(xall, mods, norm1_g[l][None, :], *p["w_in"], bsz)
        a_mix = _mlstm_call(z, zgc, zkt, zgt, *p["gate_b"], p["mlstm_g"])
        q, k, v = _qkv_call(z, p, rope_tabs)
        t_out = SEQ if last else T_ALL
        b_mix = _attn_call(q, k, v, t_out)
        x1 = _mixout_call(xall, z, a_mix, b_mix, mods, p, bsz, t_out, 1024 if last else 768)
        j = l // 2
        if l % 2 == 0:
            assert not last
            xall = _ffn_call(x1, mods, norm2_g[l][None, :], ffn_w1[j].astype(BF16), ffn_w3[j].astype(BF16),
                             ffn_w2[j].astype(BF16), bsz)
        else:
            assert last
            xall = _moe_layer(x1, mods, norm2_g[l][None, :], moe_router_w[j], moe_router_b[j],
                              moe_w1[j].astype(BF16), moe_w3[j].astype(BF16), moe_w2[j].astype(BF16))
    return xall
```

```python
import functools

import jax
import jax.numpy as jnp
from jax import lax
from jax.experimental import pallas as pl
from jax.experimental.pallas import tpu as pltpu

F32 = jnp.float32
BF16 = jnp.bfloat16
I32 = jnp.int32
HIGHEST = lax.Precision.HIGHEST

D_MODEL = 1024
SEQ = 2048
CTX_LEN = 256
T_ALL = SEQ + CTX_LEN
DEPTH = 2
GRID_W = 64
EPS = 1e-6
MLSTM_HEADS = 4
MLSTM_HEAD_DIM = 64
MLSTM_WIDTH = 256
CHUNK = 128
MLA_HEADS = 8
MLA_Q_RANK = 256
MLA_KV_RANK = 128
MLA_NOPE = 64
MLA_ROPE = 32
MLA_V = 64
MLA_QK = 96
MLA_WIDTH = 512
ROPE_BASE = 10000.0
GMLP_GROUPS = 4
GMLP_GROUP_DIM = 64
GMLP_WIDTH = 256
D_FF = 2816
N_EXPERTS = 8
TOP_K = 2
D_FF_EXPERT = 3584
MOE_BLOCK = 512
OFF_GA = 4 * MLSTM_WIDTH
OFF_CQ = OFF_GA + 4 * MLSTM_HEADS
OFF_CKV = OFF_CQ + MLA_Q_RANK
OFF_KR = OFF_CKV + MLA_KV_RANK
OFF_GM = OFF_KR + MLA_ROPE
IN_COLS = OFF_GM + 2 * GMLP_WIDTH

LANES = 128
HEAD_PAD = LANES
ZC_B = 0
ZC_GM = 512
ZC_QV = 1024
ZC_O = 1536
ZC_G = 1792
Z_COLS = ZC_G + 2 * LANES
GATE_ROWS = 8
VMEM_LIMIT = 56 * 1024 * 1024

N_CHUNKS = T_ALL // CHUNK
N_LAT_CHUNKS = SEQ // CHUNK


def _cparams(sem, vmem=VMEM_LIMIT):
    return pltpu.CompilerParams(dimension_semantics=sem, vmem_limit_bytes=vmem)


def _rms(x, g):
    return x * lax.rsqrt(jnp.mean(x * x, axis=-1, keepdims=True) + EPS) * g


def _silu(x):
    return x * jax.nn.sigmoid(x)


def _modulated_norm(x, g, shift, scale):
    return _rms(x, g) * (1.0 + scale) + shift


def _row_mods(ml_ref, mc_ref, first, tile_start, rows):
    row = tile_start + lax.broadcasted_iota(I32, (rows, 1), 0)
    is_ctx = row >= SEQ
    return tuple(jnp.where(is_ctx, mc_ref[0, first + k:first + k + 1, :], ml_ref[0, first + k:first + k + 1, :])
                 for k in range(3))


def _ada_kernel(c_ref, w_ref, b_ref, o_ref):
    s = _silu(c_ref[...]).astype(BF16)
    o_ref[0] = jnp.dot(s, w_ref[0], preferred_element_type=F32) + b_ref[0]


def _ada_call(cvec, ada_w, ada_b):
    rows = cvec.shape[0]
    return pl.pallas_call(
        _ada_kernel,
        grid=(DEPTH, 6),
        in_specs=[pl.BlockSpec((rows, D_MODEL), lambda l, j: (0, 0)),
                  pl.BlockSpec((1, D_MODEL, D_MODEL), lambda l, j: (l, 0, j)),
                  pl.BlockSpec((1, 1, D_MODEL), lambda l, j: (l, 0, j))],
        out_specs=pl.BlockSpec((1, rows, D_MODEL), lambda l, j: (l, 0, j)),
        out_shape=jax.ShapeDtypeStruct((DEPTH, rows, 6 * D_MODEL), F32),
        compiler_params=_cparams(("arbitrary", "arbitrary")),
        name="adaln",
    )(cvec, ada_w, ada_b)


def _inproj_kernel(x_ref, ml_ref, mc_ref, g_ref, w_ref, wk_ref, wg_ref, z_ref, zgc_ref, zk_ref, zg_ref, *, tr):
    shift, scale, _ = _row_mods(ml_ref, mc_ref, 0, pl.program_id(1) * tr, tr)
    xn = _modulated_norm(x_ref[0], g_ref[...], shift, scale).astype(BF16)
    z = jnp.dot(xn, w_ref[...], preferred_element_type=F32)
    z_ref[0] = z[:, 0:ZC_G].astype(z_ref.dtype)
    zgc_ref[0] = z[:, ZC_G:Z_COLS]
    nt = (((1,), (1,)), ((), ()))
    zk_ref[0] = lax.dot_general(wk_ref[...], xn, nt, preferred_element_type=F32).astype(zk_ref.dtype)
    zg_ref[0] = lax.dot_general(wg_ref[...], xn, nt, preferred_element_type=F32)


def _inproj_call(xall, mods, norm_g, w_main, w_kt, w_gt, n_ctx_row):
    bsz = xall.shape[0]
    tr = 768
    const = lambda b, i: (0, 0)
    return pl.pallas_call(
        functools.partial(_inproj_kernel, tr=tr),
        grid=(bsz, T_ALL // tr),
        in_specs=[pl.BlockSpec((1, tr, D_MODEL), lambda b, i: (b, i, 0)),
                  pl.BlockSpec((1, 6, D_MODEL), lambda b, i: (b, 0, 0)),
                  pl.BlockSpec((1, 6, D_MODEL), lambda b, i: (n_ctx_row, 0, 0)),
                  pl.BlockSpec((1, D_MODEL), const),
                  pl.BlockSpec((D_MODEL, Z_COLS), const),
                  pl.BlockSpec((MLSTM_WIDTH, D_MODEL), const),
                  pl.BlockSpec((2 * GATE_ROWS, D_MODEL), const)],
        out_specs=[pl.BlockSpec((1, tr, ZC_G), lambda b, i: (b, i, 0)),
                   pl.BlockSpec((1, tr, Z_COLS - ZC_G), lambda b, i: (b, i, 0)),
                   pl.BlockSpec((1, MLSTM_WIDTH, tr), lambda b, i: (b, 0, i)),
                   pl.BlockSpec((1, 2 * GATE_ROWS, tr), lambda b, i: (b, 0, i))],
        out_shape=[jax.ShapeDtypeStruct((bsz, T_ALL, ZC_G), BF16),
                   jax.ShapeDtypeStruct((bsz, T_ALL, Z_COLS - ZC_G), F32),
                   jax.ShapeDtypeStruct((bsz, MLSTM_WIDTH, T_ALL), BF16),
                   jax.ShapeDtypeStruct((bsz, 2 * GATE_ROWS, T_ALL), F32)],
        compiler_params=_cparams(("parallel", "parallel")),
        name="in_proj",
    )(xall, mods, mods, norm_g, w_main, w_kt, w_gt)


def _scan_chunk(d, j):
    fwd_chunk = (j + N_LAT_CHUNKS) % N_CHUNKS
    return jnp.where(d == 0, fwd_chunk, N_CHUNKS - 1 - j)


MLSTM_ROWS_PER_STEP = 8


def _log_sigmoid(x):
    return jnp.minimum(x, 0.0) - jnp.log1p(jnp.exp(-jnp.abs(x)))


def _mlstm_kernel(*refs, n_cast):
    ins, rest = refs[:8], refs[8:]
    cast_in, o_ref, cast_out, scratch = rest[:n_cast], rest[n_cast], rest[n_cast + 1:2 * n_cast + 1], rest[2 * n_cast + 1:]
    for src, dst in zip(cast_in, cast_out):
        dst[...] = src[...].astype(dst.dtype)
    _mlstm_step(*ins, o_ref, *scratch)


def _mlstm_step(qv_ref, og_ref, zg_ref, kt_ref, gt_ref, gb_ref, gbt_ref, ng_ref, o_ref, c_ref, m_ref, hf_ref):
    L = CHUNK
    E = MLSTM_HEAD_DIM
    d = pl.program_id(1)
    j = pl.program_id(2)
    fwd = d == 0
    row0 = pl.multiple_of(_scan_chunk(d, j) * L, L)

    @pl.when(j == 0)
    def _():
        c_ref[...] = jnp.zeros_like(c_ref)
        m_ref[...] = jnp.zeros_like(m_ref)

    r_i = lax.broadcasted_iota(I32, (L, L), 0)
    c_i = lax.broadcasted_iota(I32, (L, L), 1)
    prec = jnp.logical_or(jnp.logical_and(fwd, c_i <= r_i), jnp.logical_and(jnp.logical_not(fwd), c_i >= r_i))
    succ = jnp.logical_or(jnp.logical_and(fwd, r_i <= c_i), jnp.logical_and(jnp.logical_not(fwd), r_i >= c_i))
    lane = lax.broadcasted_iota(I32, (L, LANES), 1)
    sub = lax.broadcasted_iota(I32, (LANES, L), 0)
    low_lane = lane < E
    low_sub = sub < E

    row_idx = lax.broadcasted_iota(I32, (L, LANES), 0)
    rows = range(qv_ref.shape[0])
    pairs = range(MLSTM_HEADS // 2)
    h_pairs = {}
    for bb in rows:
        gates_c = zg_ref[bb] + gb_ref[0]
        gates_r = gt_ref[bb] + gbt_ref[0]
        logf_r = _log_sigmoid(gates_r)
        logf_c = jnp.concatenate([logf_r, jnp.zeros((LANES - GATE_ROWS, L), F32)], axis=0).T
        b_c = jnp.dot(prec.astype(F32), logf_c, precision=HIGHEST, preferred_element_type=F32)
        c_c = pltpu.roll(gates_c, LANES - MLSTM_HEADS, 1) - b_c
        m_row = m_ref[bb]
        cm = c_c
        for sh in (1, 2, 4, 8, 16, 32, 64):
            prev = jnp.where(fwd, jnp.where(row_idx >= sh, pltpu.roll(cm, sh, 0), -jnp.inf),
                             jnp.where(row_idx < L - sh, pltpu.roll(cm, L - sh, 0), -jnp.inf))
            cm = jnp.maximum(cm, prev)
        m_c = jnp.maximum(cm, m_row)
        a_c = jnp.exp(m_row - m_c)
        floor_c = jnp.exp(-(b_c + m_c))
        m_last = jnp.maximum(jnp.max(c_c, axis=0, keepdims=True), m_row)
        b_last = jnp.where(fwd, b_c[L - 1:L, :], b_c[0:1, :])
        m_ref[bb] = b_last + m_last
        decay_row = jnp.exp(m_row - m_last)
        b_r = jnp.dot(logf_r, succ.astype(F32), precision=HIGHEST, preferred_element_type=F32)
        c_r = gates_r[MLSTM_HEADS:2 * MLSTM_HEADS, :] - b_r[0:MLSTM_HEADS, :]
        w_r = [jnp.exp(c_r[h:h + 1, :] - m_last[:, h:h + 1]) for h in range(MLSTM_HEADS)]

        for pj in pairs:
            tile = slice(pj * LANES, (pj + 1) * LANES)
            q_t = qv_ref[bb, :, tile]
            v_t = qv_ref[bb, :, MLSTM_WIDTH + pj * LANES:MLSTM_WIDTH + (pj + 1) * LANES]
            kt_t = kt_ref[bb, tile, :]
            c_old = c_ref[bb, pj]
            c_old_b = c_old.astype(BF16)
            kt_b = kt_t.astype(BF16)
            hx = []
            upd = []
            for par in range(2):
                h = 2 * pj + par
                own_lane = low_lane if par == 0 else jnp.logical_not(low_lane)
                own_sub = low_sub if par == 0 else jnp.logical_not(low_sub)
                den_lane = E if par == 0 else 0
                qm = jnp.where(own_lane, q_t, 0.0).astype(BF16)
                v_ext = jnp.where(own_lane, v_t, jnp.where(lane == den_lane, 1.0, 0.0)).astype(BF16)
                sqk = jnp.dot(qm, kt_b, preferred_element_type=F32)
                q_c = jnp.dot(qm, c_old_b, preferred_element_type=F32)
                dmat = jnp.where(prec, jnp.exp(c_r[h:h + 1, :] - m_c[:, h:h + 1]), 0.0)
                pv = jnp.dot((sqk * dmat).astype(BF16), v_ext, preferred_element_type=F32)
                nd = pv + a_c[:, h:h + 1] * q_c
                den = nd[:, den_lane:den_lane + 1]
                hx.append(nd / jnp.maximum(jnp.abs(den), floor_c[:, h:h + 1]))
                kw_t = (jnp.where(own_sub, kt_t, 0.0) * w_r[h]).astype(BF16)
                upd.append(jnp.dot(kw_t, v_ext, preferred_element_type=F32))
            decay = jnp.where(low_sub, decay_row[:, 2 * pj:2 * pj + 1], decay_row[:, 2 * pj + 1:2 * pj + 2])
            c_ref[bb, pj] = decay * c_old + upd[0] + upd[1]
            h_pairs[bb, pj] = jnp.where(low_lane, hx[0], hx[1])

    @pl.when(fwd)
    def _():
        for (bb, pj), h_pair in h_pairs.items():
            hf_ref[bb, pl.ds(row0, L), pj * LANES:(pj + 1) * LANES] = h_pair

    @pl.when(jnp.logical_not(fwd))
    def _():
        for (bb, pj), h_pair in h_pairs.items():
            tile = slice(pj * LANES, (pj + 1) * LANES)
            hs = hf_ref[bb, pl.ds(row0, L), tile] + h_pair
            sq = hs * hs
            ss = jnp.where(low_lane,
                           jnp.sum(jnp.where(low_lane, sq, 0.0), axis=-1, keepdims=True),
                           jnp.sum(jnp.where(low_lane, 0.0, sq), axis=-1, keepdims=True))
            y = hs * lax.rsqrt(ss * (1.0 / E) + EPS) * ng_ref[:, tile]
            o_ref[bb, :, tile] = (jax.nn.sigmoid(og_ref[bb, :, tile].astype(F32)) * y).astype(o_ref.dtype)


def _mlstm_call(z, zgc, zkt, zgt, gate_b, gate_bt, norm_g, cast_ws=()):
    bsz = z.shape[0]
    nb = max(r for r in range(1, MLSTM_ROWS_PER_STEP + 1) if bsz % r == 0)
    last = N_CHUNKS - 1
    chunk = lambda b, d, j: _scan_chunk(d, j)
    steps = (bsz // nb) * 2 * N_CHUNKS
    n_blk = 1 << (steps.bit_length() - 1)
    cast_idx = lambda b, d, j: (jnp.minimum((b * 2 + d) * N_CHUNKS + j, n_blk - 1), 0)
    cast_specs = [pl.BlockSpec((w.shape[0] // n_blk, w.shape[1]), cast_idx) for w in cast_ws]
    assert all(w.shape[0] % (16 * n_blk) == 0 for w in cast_ws)
    return pl.pallas_call(
        functools.partial(_mlstm_kernel, n_cast=len(cast_ws)),
        grid=(bsz // nb, 2, N_CHUNKS),
        in_specs=[pl.BlockSpec((nb, CHUNK, 2 * MLSTM_WIDTH), lambda b, d, j: (b, chunk(b, d, j), ZC_QV // 512)),
                  pl.BlockSpec((nb, CHUNK, MLSTM_WIDTH), lambda b, d, j: (b, chunk(b, d, j), ZC_O // MLSTM_WIDTH)),
                  pl.BlockSpec((nb, CHUNK, LANES), lambda b, d, j: (b, chunk(b, d, j), d)),
                  pl.BlockSpec((nb, MLSTM_WIDTH, CHUNK), lambda b, d, j: (b, 0, chunk(b, d, j))),
                  pl.BlockSpec((nb, GATE_ROWS, CHUNK), lambda b, d, j: (b, d, chunk(b, d, j))),
                  pl.BlockSpec((1, 1, LANES), lambda b, d, j: (d, 0, 0)),
                  pl.BlockSpec((1, GATE_ROWS, LANES), lambda b, d, j: (d, 0, 0)),
                  pl.BlockSpec((1, MLSTM_WIDTH), lambda b, d, j: (0, 0))] + cast_specs,
        out_specs=[pl.BlockSpec((nb, CHUNK, MLSTM_WIDTH),
                                lambda b, d, j: (b, jnp.where(d == 0, last, last - j), 0))] + cast_specs,
        out_shape=[jax.ShapeDtypeStruct((bsz, T_ALL, MLSTM_WIDTH), BF16)]
        + [jax.ShapeDtypeStruct(w.shape, BF16) for w in cast_ws],
        scratch_shapes=[pltpu.VMEM((nb, MLSTM_HEADS // 2, LANES, LANES), F32),
                        pltpu.VMEM((nb, 1, LANES), F32),
                        pltpu.VMEM((nb, T_ALL, MLSTM_WIDTH), F32)],
        compiler_params=_cparams(("arbitrary", "arbitrary", "arbitrary")),
        name="mlstm",
    )(z, z, zgc, zkt, zgt, gate_b, gate_bt, norm_g, *cast_ws)


ROPE_HALF = MLA_ROPE // 2


def _rope(t, cos_t, sin_t):
    return t * cos_t + pltpu.roll(t, LANES - ROPE_HALF, 1) * sin_t


HEAD_PAIR = 2 * HEAD_PAD


def _head_rms_scale(raw, ones_ref):
    sq = (raw * raw).astype(BF16)
    ss = jnp.concatenate([jnp.dot(sq[:, j * HEAD_PAIR:(j + 1) * HEAD_PAIR], ones_ref[...],
                                  preferred_element_type=F32) for j in range(MLA_HEADS // 2)], axis=-1)
    return lax.rsqrt(ss * (1.0 / MLA_QK) + EPS)


def _qkv_kernel(z_ref, cqg_ref, ckvg_ref, wq_ref, wk_ref, wv_ref, qg_ref, kg_ref, ones_ref, ct_ref, st_ref,
                q_ref, k_ref, v_ref, *, tr):
    cqn = _rms(z_ref[0, :, 0:MLA_Q_RANK].astype(F32), cqg_ref[...]).astype(BF16)
    ckvn = _rms(z_ref[0, :, MLA_Q_RANK:MLA_Q_RANK + MLA_KV_RANK].astype(F32), ckvg_ref[...]).astype(BF16)
    k_rope = z_ref[0, :, MLA_Q_RANK + MLA_KV_RANK:MLA_Q_RANK + MLA_KV_RANK + LANES].astype(F32)
    q_raw = jnp.dot(cqn, wq_ref[...], preferred_element_type=F32)
    k_raw = jnp.dot(ckvn, wk_ref[...], preferred_element_type=F32) + jnp.concatenate([k_rope] * MLA_HEADS, axis=-1)
    v_raw = jnp.dot(ckvn, wv_ref[...], preferred_element_type=F32)
    q_n = q_raw * _head_rms_scale(q_raw, ones_ref)
    k_n = k_raw * _head_rms_scale(k_raw, ones_ref)
    cos_t = ct_ref[...]
    sin_t = st_ref[...]
    lane = lax.broadcasted_iota(I32, (tr, LANES), 1)
    for h in range(MLA_HEADS):
        sl = slice(h * HEAD_PAD, (h + 1) * HEAD_PAD)
        qh = _rope(q_n[:, sl] * qg_ref[...], cos_t, sin_t)
        q_ref[0, :, sl] = (qh * (MLA_QK ** -0.5)).astype(BF16)
        kh = _rope(k_n[:, sl] * kg_ref[...], cos_t, sin_t)
        k_ref[0, :, sl] = kh.astype(BF16)
        v_ref[0, :, sl] = jnp.where(lane == MLA_V, 1.0, v_raw[:, sl]).astype(BF16)


def _qkv_call(z, p, rope_tabs):
    bsz = z.shape[0]
    tr = 768
    hw = MLA_HEADS * HEAD_PAD
    const = lambda b, i: (0, 0)
    out = jax.ShapeDtypeStruct((bsz, T_ALL, hw), BF16)
    tab = pl.BlockSpec((tr, LANES), lambda b, i: (i, 0))
    ospec = pl.BlockSpec((1, tr, hw), lambda b, i: (b, i, 0))
    return pl.pallas_call(
        functools.partial(_qkv_kernel, tr=tr),
        grid=(bsz, T_ALL // tr),
        in_specs=[pl.BlockSpec((1, tr, 512), lambda b, i: (b, i, ZC_B // 512)),
                  pl.BlockSpec((1, MLA_Q_RANK), const), pl.BlockSpec((1, MLA_KV_RANK), const),
                  pl.BlockSpec((MLA_Q_RANK, hw), const), pl.BlockSpec((MLA_KV_RANK, hw), const),
                  pl.BlockSpec((MLA_KV_RANK, hw), const),
                  pl.BlockSpec((1, LANES), const), pl.BlockSpec((1, LANES), const),
                  pl.BlockSpec((HEAD_PAIR, HEAD_PAIR), const),
                  tab, tab],
        out_specs=[ospec, ospec, ospec],
        out_shape=[out, out, out],
        compiler_params=_cparams(("parallel", "parallel")),
        name="mla_qkv",
    )(z, p["cq_g"], p["ckv_g"], p["w_uq"], p["w_k"], p["w_v"], p["q_g"], p["k_g"], _head_ones(), *rope_tabs)


def _head_ones():
    r = jnp.arange(HEAD_PAIR)
    same_head = (r[:, None] // HEAD_PAD) == (r[None, :] // HEAD_PAD)
    real_row = (r[:, None] % HEAD_PAD) < MLA_QK
    return jnp.logical_and(same_head, real_row).astype(BF16)


ATT_TQ = 512
assert SEQ % ATT_TQ == 0 and CTX_LEN <= ATT_TQ


def _attn_kernel(q_ref, k_ref, v_ref, o_ref):
    i = pl.program_id(1)

    def run(rows, k0):
        def scores(hh):
            sl = slice(hh * HEAD_PAD, (hh + 1) * HEAD_PAD)
            return lax.dot_general(q_ref[0, 0:rows, sl], k_ref[0, k0:T_ALL, sl], (((1,), (1,)), ((), ())),
                                   preferred_element_type=F32)

        s = scores(0)
        for hh in range(MLA_HEADS):
            s_next = scores(hh + 1) if hh + 1 < MLA_HEADS else None
            p = jnp.exp(s - jnp.max(s, axis=-1, keepdims=True)).astype(BF16)
            oe = jnp.dot(p, v_ref[0, k0:T_ALL, hh * HEAD_PAD:(hh + 1) * HEAD_PAD], preferred_element_type=F32)
            o_ref[0, 0:rows, hh * MLA_V:(hh + 1) * MLA_V] = (
                oe[:, 0:MLA_V] / oe[:, MLA_V:MLA_V + 1]).astype(o_ref.dtype)
            s = s_next

    @pl.when(i < SEQ // ATT_TQ)
    def _():
        run(ATT_TQ, 0)

    @pl.when(i >= SEQ // ATT_TQ)
    def _():
        run(CTX_LEN, SEQ)
        o_ref[0, CTX_LEN:ATT_TQ, :] = jnp.zeros((ATT_TQ - CTX_LEN, MLA_WIDTH), o_ref.dtype)


def _attn_call(q, k, v, t_out):
    bsz = q.shape[0]
    w = MLA_HEADS * HEAD_PAD
    return pl.pallas_call(
        _attn_kernel,
        grid=(bsz, pl.cdiv(t_out, ATT_TQ)),
        in_specs=[pl.BlockSpec((1, ATT_TQ, w), lambda b, i: (b, i, 0)),
                  pl.BlockSpec((1, T_ALL, w), lambda b, i: (b, 0, 0)),
                  pl.BlockSpec((1, T_ALL, w), lambda b, i: (b, 0, 0))],
        out_specs=pl.BlockSpec((1, ATT_TQ, MLA_WIDTH), lambda b, i: (b, i, 0)),
        out_shape=jax.ShapeDtypeStruct((bsz, t_out, MLA_WIDTH), BF16),
        compiler_params=_cparams(("parallel", "arbitrary")),
        name="mla_attn",
    )(q, k, v)


def _mixout_kernel(x_ref, zg_ref, a_ref, b_ref, ml_ref, mc_ref, lng_ref, lnb_ref, ws_ref, bs_ref, wo_ref,
                   o_ref, cm_ref, *, tr):
    gd = GMLP_GROUP_DIM
    act = jax.nn.gelu(zg_ref[0].astype(F32))
    u = act[:, 0:GMLP_WIDTH]
    vv = act[:, GMLP_WIDTH:2 * GMLP_WIDTH]
    mu = jnp.mean(vv, axis=-1, keepdims=True)
    var = jnp.mean(jnp.square(vv - mu), axis=-1, keepdims=True)
    vn = ((vv - mu) * lax.rsqrt(var + EPS) * lng_ref[...] + lnb_ref[...]).astype(BF16)
    for c in range(tr // CHUNK):
        rows = slice(c * CHUNK, (c + 1) * CHUNK)
        for g in range(GMLP_GROUPS):
            cols = slice(g * gd, (g + 1) * gd)
            sv = jnp.dot(ws_ref[g], vn[rows, cols], preferred_element_type=F32) + bs_ref[:, cols]
            cm_ref[rows, cols] = u[rows, cols] * sv
    y = jnp.dot(a_ref[0].astype(BF16), wo_ref[0:MLSTM_WIDTH, :], preferred_element_type=F32)
    y += jnp.dot(b_ref[0].astype(BF16), wo_ref[MLSTM_WIDTH:MLSTM_WIDTH + MLA_WIDTH, :],
                 preferred_element_type=F32)
    y += jnp.dot(cm_ref[...].astype(BF16), wo_ref[MLSTM_WIDTH + MLA_WIDTH:, :], preferred_element_type=F32)
    _, _, gate = _row_mods(ml_ref, mc_ref, 0, pl.program_id(1) * tr, tr)
    o_ref[0] = x_ref[0] + gate * y


def _mixout_call(xall, z, a_mix, b_mix, mods, p, n_ctx_row, t_out, tr):
    bsz = xall.shape[0]
    const = lambda b, i: (0, 0)
    return pl.pallas_call(
        functools.partial(_mixout_kernel, tr=tr),
        grid=(bsz, t_out // tr),
        in_specs=[pl.BlockSpec((1, tr, D_MODEL), lambda b, i: (b, i, 0)),
                  pl.BlockSpec((1, tr, 2 * GMLP_WIDTH), lambda b, i: (b, i, ZC_GM // (2 * GMLP_WIDTH))),
                  pl.BlockSpec((1, tr, MLSTM_WIDTH), lambda b, i: (b, i, 0)),
                  pl.BlockSpec((1, tr, MLA_WIDTH), lambda b, i: (b, i, 0)),
                  pl.BlockSpec((1, 6, D_MODEL), lambda b, i: (b, 0, 0)),
                  pl.BlockSpec((1, 6, D_MODEL), lambda b, i: (n_ctx_row, 0, 0)),
                  pl.BlockSpec((1, GMLP_WIDTH), const), pl.BlockSpec((1, GMLP_WIDTH), const),
                  pl.BlockSpec((GMLP_GROUPS, CHUNK, CHUNK), lambda b, i: (0, 0, 0)),
                  pl.BlockSpec((CHUNK, GMLP_WIDTH), const),
                  pl.BlockSpec((D_MODEL, D_MODEL), const)],
        out_specs=pl.BlockSpec((1, tr, D_MODEL), lambda b, i: (b, i, 0)),
        out_shape=jax.ShapeDtypeStruct((bsz, t_out, D_MODEL), F32),
        scratch_shapes=[pltpu.VMEM((tr, GMLP_WIDTH), F32)],
        compiler_params=_cparams(("parallel", "parallel")),
        name="mix_out",
    )(xall, z, a_mix, b_mix, mods, mods, p["ln_g"], p["ln_b"], p["w_s"], p["b_s"], p["w_out"])


FFN_SPLIT = 2


def _ffn_kernel(x_ref, ml_ref, mc_ref, g_ref, w1_ref, w3_ref, w2_ref, o_ref, *, tr):
    shift, scale, gate = _row_mods(ml_ref, mc_ref, 3, pl.program_id(1) * tr, tr)
    x = x_ref[0]
    h = _modulated_norm(x, g_ref[...], shift, scale).astype(BF16)
    fc = D_FF // FFN_SPLIT
    y = jnp.zeros((tr, D_MODEL), F32)
    for f in range(FFN_SPLIT):
        cols = slice(f * fc, (f + 1) * fc)
        h1 = jnp.dot(h, w1_ref[:, cols], preferred_element_type=F32)
        h3 = jnp.dot(h, w3_ref[:, cols], preferred_element_type=F32)
        y += jnp.dot((_silu(h1) * h3).astype(BF16), w2_ref[cols, :], preferred_element_type=F32)
    o_ref[0] = x + gate * y


def _ffn_call(x1, mods, norm_g, w1, w3, w2, n_ctx_row):
    bsz = x1.shape[0]
    tr = 768
    const = lambda b, i: (0, 0)
    resident = pl.Buffered(1)
    return pl.pallas_call(
        functools.partial(_ffn_kernel, tr=tr),
        grid=(bsz, T_ALL // tr),
        in_specs=[pl.BlockSpec((1, tr, D_MODEL), lambda b, i: (b, i, 0)),
                  pl.BlockSpec((1, 6, D_MODEL), lambda b, i: (b, 0, 0)),
                  pl.BlockSpec((1, 6, D_MODEL), lambda b, i: (n_ctx_row, 0, 0)),
                  pl.BlockSpec((1, D_MODEL), const),
                  pl.BlockSpec((D_MODEL, D_FF), const, pipeline_mode=resident),
                  pl.BlockSpec((D_MODEL, D_FF), const, pipeline_mode=resident),
                  pl.BlockSpec((D_FF, D_MODEL), const, pipeline_mode=resident)],
        out_specs=pl.BlockSpec((1, tr, D_MODEL), lambda b, i: (b, i, 0)),
        out_shape=jax.ShapeDtypeStruct((bsz, T_ALL, D_MODEL), F32),
        compiler_params=_cparams(("parallel", "parallel")),
        name="dense_ffn",
    )(x1, mods, mods, norm_g, w1, w3, w2)


ROUTE_TR = 1024


def _router_kernel(x_ref, ml_ref, g_ref, rw_ref, rb_ref, h_ref, e_ref, r_ref, gt_ref, cnt_ref, run_ref, *, tr):
    step = pl.program_id(0) * pl.num_programs(1) + pl.program_id(1)

    @pl.when(step == 0)
    def _():
        run_ref[...] = jnp.zeros_like(run_ref)

    h = _modulated_norm(x_ref[0], g_ref[...], ml_ref[0, 3:4, :], ml_ref[0, 4:5, :])
    h_ref[...] = h
    lane = lax.broadcasted_iota(I32, (tr, LANES), 1)
    logits = jnp.dot(h, rw_ref[...], precision=HIGHEST, preferred_element_type=F32) + rb_ref[...]
    logits = jnp.where(lane < N_EXPERTS, logits, -jnp.inf)
    m1 = jnp.max(logits, axis=-1, keepdims=True)
    e1 = jnp.min(jnp.where(logits == m1, lane, LANES), axis=-1, keepdims=True)
    rest = jnp.where(lane == e1, -jnp.inf, logits)
    m2 = jnp.max(rest, axis=-1, keepdims=True)
    e2 = jnp.min(jnp.where(rest == m2, lane, LANES), axis=-1, keepdims=True)
    ex = jnp.exp(m2 - m1)
    g1 = 1.0 / (1.0 + ex)
    g2 = ex / (1.0 + ex)
    onehot = jnp.logical_or(lane == e1, lane == e2 + N_EXPERTS)
    oh = onehot.astype(F32)
    r_i = lax.broadcasted_iota(I32, (tr, tr), 0)
    c_i = lax.broadcasted_iota(I32, (tr, tr), 1)
    before = jnp.dot((c_i < r_i).astype(BF16), oh.astype(BF16), preferred_element_type=F32)
    tot = jnp.sum(oh, axis=0, keepdims=True)
    tot0_shift = pltpu.roll(tot, N_EXPERTS, 1)
    run = run_ref[...]
    first_half = lax.broadcasted_iota(I32, (1, LANES), 1) < N_EXPERTS
    offs = run + jnp.where(first_half, 0.0, tot0_shift)
    ranks = oh * (before + offs)
    rank1 = jnp.sum(jnp.where(lane < N_EXPERTS, ranks, 0.0), axis=-1, keepdims=True)
    rank2 = jnp.sum(jnp.where(lane >= N_EXPERTS, ranks, 0.0), axis=-1, keepdims=True)
    col = lax.broadcasted_iota(I32, (tr, TOP_K), 1)
    e_ref[...] = jnp.where(col == 0, e1, e2)
    r_ref[...] = jnp.where(col == 0, rank1, rank2).astype(I32)
    gt_ref[...] = jnp.where(col == 0, g1, g2)
    both = tot + jnp.where(first_half, pltpu.roll(tot, LANES - N_EXPERTS, 1), tot0_shift)
    new_run = run + both
    run_ref[...] = new_run
    cnt_ref[...] = new_run.astype(I32)


def _router_call(x1, mods, norm_g, rw_p, rb_p):
    bsz = x1.shape[0]
    tr = ROUTE_TR
    n = bsz * SEQ
    nt = SEQ // tr
    const = lambda b, i: (0, 0)
    tok = lambda b, i: (b * nt + i, 0)
    return pl.pallas_call(
        functools.partial(_router_kernel, tr=tr),
        grid=(bsz, nt),
        in_specs=[pl.BlockSpec((1, tr, D_MODEL), lambda b, i: (b, i, 0)),
                  pl.BlockSpec((1, 6, D_MODEL), lambda b, i: (b, 0, 0)),
                  pl.BlockSpec((1, D_MODEL), const),
                  pl.BlockSpec((D_MODEL, LANES), const),
                  pl.BlockSpec((1, LANES), const)],
        out_specs=[pl.BlockSpec((tr, D_MODEL), tok),
                   pl.BlockSpec((tr, TOP_K), tok), pl.BlockSpec((tr, TOP_K), tok), pl.BlockSpec((tr, TOP_K), tok),
                   pl.BlockSpec((1, LANES), const)],
        out_shape=[jax.ShapeDtypeStruct((n, D_MODEL), F32),
                   jax.ShapeDtypeStruct((n, TOP_K), I32), jax.ShapeDtypeStruct((n, TOP_K), I32),
                   jax.ShapeDtypeStruct((n, TOP_K), F32),
                   jax.ShapeDtypeStruct((1, LANES), I32)],
        scratch_shapes=[pltpu.VMEM((1, LANES), F32)],
        compiler_params=_cparams(("arbitrary", "arbitrary")),
        name="moe_router",
    )(x1, mods, norm_g, rw_p, rb_p)


DISPATCH_TD = 512
SUBLANES = 8
ZERO_BURST = MOE_BLOCK + SUBLANES
N_ZERO_BURSTS = 2 * N_EXPERTS
DMA_ISSUE_UNROLL = 8


def _row_copy(src, src_row, dst, dst_row, sem):
    return pltpu.make_async_copy(src.at[pl.ds(src_row, 1)], dst.at[pl.ds(dst_row, 1)], sem)


def _dispatch_kernel(pad_ref, dest_ref, h_ref, xs_out, zero_ref, sem):
    @pl.when(pl.program_id(0) == 0)
    def _():
        zero_ref[...] = jnp.zeros_like(zero_ref)
        for e in range(N_ZERO_BURSTS):
            start = pl.multiple_of(pad_ref[e], SUBLANES)
            burst = pltpu.make_async_copy(zero_ref, xs_out.at[pl.ds(start, ZERO_BURST)], sem)
            burst.start()
            burst.wait()

    def issue(r, carry):
        for kk in range(TOP_K):
            _row_copy(h_ref, r, xs_out, dest_ref[kk, r], sem).start()
        return carry

    lax.fori_loop(0, DISPATCH_TD, issue, 0, unroll=DMA_ISSUE_UNROLL)

    for kk in range(TOP_K):
        pltpu.make_async_copy(h_ref, xs_out.at[pl.ds(0, DISPATCH_TD)], sem).wait()


def _dispatch_call(pad_start, dest_t, h2, cap):
    n = h2.shape[0]
    grid_spec = pltpu.PrefetchScalarGridSpec(
        num_scalar_prefetch=1,
        grid=(n // DISPATCH_TD,),
        in_specs=[pl.BlockSpec((TOP_K, DISPATCH_TD), lambda i, pad: (0, i), memory_space=pltpu.SMEM),
                  pl.BlockSpec((DISPATCH_TD, D_MODEL), lambda i, pad: (i, 0))],
        out_specs=pl.BlockSpec(memory_space=pl.ANY),
        scratch_shapes=[pltpu.VMEM((ZERO_BURST, D_MODEL), F32), pltpu.SemaphoreType.DMA(())],
    )
    return pl.pallas_call(
        _dispatch_kernel,
        grid_spec=grid_spec,
        out_shape=jax.ShapeDtypeStruct((cap, D_MODEL), F32),
        compiler_params=_cparams(("arbitrary",)),
        name="moe_dispatch",
    )(pad_start, dest_t, h2)


EXPERT_FSPLIT = 2


def _expert_kernel(be_ref, na_ref, xs_ref, w1_ref, w3_ref, w2_ref, ys_ref):
    i = pl.program_id(0)
    f = pl.program_id(1)
    active = i < na_ref[0]

    @pl.when(jnp.logical_and(active, f == 0))
    def _():
        ys_ref[...] = jnp.zeros_like(ys_ref)

    @pl.when(active)
    def _():
        x = xs_ref[...].astype(BF16)
        h1 = jnp.dot(x, w1_ref[0], preferred_element_type=F32)
        h3 = jnp.dot(x, w3_ref[0], preferred_element_type=F32)
        ys_ref[...] += jnp.dot((_silu(h1) * h3).astype(BF16), w2_ref[0], preferred_element_type=F32)

    @pl.when(jnp.logical_not(active))
    def _():
        ys_ref[...] = jnp.zeros_like(ys_ref)


def _expert_call(block_expert, n_active, xs, w1, w3, w2):
    cap = xs.shape[0]
    nb = cap // MOE_BLOCK
    fc = D_FF_EXPERT // EXPERT_FSPLIT
    last_f = EXPERT_FSPLIT - 1

    def blk(i, na):
        return jnp.minimum(i, jnp.maximum(na[0] - 1, 0))

    def fidx(i, f, na):
        return jnp.where(i < na[0], f, last_f)

    grid_spec = pltpu.PrefetchScalarGridSpec(
        num_scalar_prefetch=2,
        grid=(nb, EXPERT_FSPLIT),
        in_specs=[pl.BlockSpec((MOE_BLOCK, D_MODEL), lambda i, f, be, na: (blk(i, na), 0)),
                  pl.BlockSpec((1, D_MODEL, fc), lambda i, f, be, na: (be[blk(i, na)], 0, fidx(i, f, na))),
                  pl.BlockSpec((1, D_MODEL, fc), lambda i, f, be, na: (be[blk(i, na)], 0, fidx(i, f, na))),
                  pl.BlockSpec((1, fc, D_MODEL), lambda i, f, be, na: (be[blk(i, na)], fidx(i, f, na), 0))],
        out_specs=pl.BlockSpec((MOE_BLOCK, D_MODEL), lambda i, f, be, na: (i, 0)),
    )
    return pl.pallas_call(
        _expert_kernel,
        grid_spec=grid_spec,
        out_shape=jax.ShapeDtypeStruct((cap, D_MODEL), F32),
        compiler_params=_cparams(("arbitrary", "arbitrary")),
        name="moe_experts",
    )(block_expert, n_active, xs, w1, w3, w2)


COMBINE_TC = 256


def _combine_kernel(dest_ref, dest_next_ref, x_ref, gt_ref, ml_ref, ys_hbm, o_ref, buf_ref, sem):
    i = pl.program_id(0)
    slot = i % 2

    def gather(d_ref, s):
        def issue(r, carry):
            for kk in range(TOP_K):
                _row_copy(ys_hbm, d_ref[kk, r], buf_ref.at[s, kk], r, sem.at[s]).start()
            return carry

        lax.fori_loop(0, COMBINE_TC, issue, 0, unroll=DMA_ISSUE_UNROLL)

    @pl.when(i == 0)
    def _():
        gather(dest_ref, 0)

    @pl.when(i + 1 < pl.num_programs(0))
    def _():
        gather(dest_next_ref, 1 - slot)

    for kk in range(TOP_K):
        pltpu.make_async_copy(ys_hbm.at[pl.ds(0, COMBINE_TC)], buf_ref.at[slot, kk], sem.at[slot]).wait()
    g = gt_ref[...]
    y = buf_ref[slot, 0] * g[:, 0:1] + buf_ref[slot, 1] * g[:, 1:2]
    o_ref[...] = x_ref[...] + ml_ref[0, 5:6, :] * y


def _combine_call(dest_t, x1_flat, gates, mods, ys):
    n = x1_flat.shape[0]
    tc = COMBINE_TC
    per_batch = SEQ // tc
    last = n // tc - 1
    return pl.pallas_call(
        _combine_kernel,
        grid=(n // tc,),
        in_specs=[pl.BlockSpec((TOP_K, tc), lambda i: (0, i), memory_space=pltpu.SMEM),
                  pl.BlockSpec((TOP_K, tc), lambda i: (0, jnp.minimum(i + 1, last)), memory_space=pltpu.SMEM),
                  pl.BlockSpec((tc, D_MODEL), lambda i: (i, 0)),
                  pl.BlockSpec((tc, TOP_K), lambda i: (i, 0)),
                  pl.BlockSpec((1, 6, D_MODEL), lambda i: (i // per_batch, 0, 0)),
                  pl.BlockSpec(memory_space=pl.ANY)],
        out_specs=pl.BlockSpec((tc, D_MODEL), lambda i: (i, 0)),
        out_shape=jax.ShapeDtypeStruct((n, D_MODEL), F32),
        scratch_shapes=[pltpu.VMEM((2, TOP_K, tc, D_MODEL), F32), pltpu.SemaphoreType.DMA((2,))],
        compiler_params=_cparams(("arbitrary",)),
        name="moe_combine",
    )(dest_t, dest_t, x1_flat, gates, mods, ys)


def _moe_layer(x1, mods, norm_g, router_w, router_b, w1, w3, w2):
    bsz = x1.shape[0]
    n = bsz * SEQ
    cap = (n * TOP_K + MOE_BLOCK - 1) // MOE_BLOCK * MOE_BLOCK + N_EXPERTS * MOE_BLOCK
    rw_p = jnp.zeros((D_MODEL, LANES), F32).at[:, :N_EXPERTS].set(router_w)
    rb_p = jnp.zeros((1, LANES), F32).at[0, :N_EXPERTS].set(router_b)
    h2, e_idx, rank, gates, counts = _router_call(x1, mods, norm_g, rw_p, rb_p)
    counts = counts[0, :N_EXPERTS]
    padded = (counts + MOE_BLOCK - 1) // MOE_BLOCK * MOE_BLOCK
    padded_end = jnp.cumsum(padded)
    base = padded_end - padded
    dest_t = (base[e_idx] + rank).T.astype(I32)
    n_blocks = cap // MOE_BLOCK
    block_start = jnp.arange(n_blocks, dtype=I32) * MOE_BLOCK
    block_expert = jnp.minimum(jnp.sum(padded_end[None, :] <= block_start[:, None], axis=1),
                               N_EXPERTS - 1).astype(I32)
    n_active = (padded_end[-1:] // MOE_BLOCK).astype(I32)
    tail = padded_end[-1] + jnp.arange(N_EXPERTS, dtype=I32) * MOE_BLOCK
    pad_start = jnp.concatenate([base + counts, tail])
    pad_start = (jnp.minimum(pad_start, cap - ZERO_BURST) // SUBLANES * SUBLANES).astype(I32)
    xs = _dispatch_call(pad_start, dest_t, h2, cap)
    ys = _expert_call(block_expert, n_active, xs, w1, w3, w2)
    out = _combine_call(dest_t, x1.reshape(n, D_MODEL), gates, mods, ys)
    return out.reshape(bsz, SEQ, D_MODEL)


def _rope_tables():
    rows = SEQ // GRID_W
    row = jnp.repeat(jnp.arange(rows), GRID_W).astype(F32)
    col = jnp.tile(jnp.arange(GRID_W), rows).astype(F32)
    n_freq = MLA_ROPE // 4
    inv = ROPE_BASE ** (-jnp.arange(n_freq, dtype=F32) / n_freq)
    ang = jnp.concatenate([row[:, None] * inv, col[:, None] * inv], axis=-1)
    cos, sin = jnp.cos(ang), jnp.sin(ang)
    ones = jnp.ones((SEQ, MLA_NOPE), F32)
    pad = jnp.zeros((SEQ, LANES - MLA_QK), F32)
    zn = jnp.zeros((SEQ, MLA_NOPE), F32)
    cos_t = jnp.concatenate([ones, cos, cos, pad], axis=-1)
    sin_t = jnp.concatenate([zn, -sin, sin, pad], axis=-1)
    ident = jnp.concatenate([jnp.ones((CTX_LEN, MLA_QK), F32), jnp.zeros((CTX_LEN, LANES - MLA_QK), F32)], axis=-1)
    zero = jnp.zeros((CTX_LEN, LANES), F32)
    return jnp.concatenate([cos_t, ident], axis=0), jnp.concatenate([sin_t, zero], axis=0)


def _rope_copy_pad(a):
    first_half = a[..., MLA_NOPE:MLA_NOPE + ROPE_HALF]
    zeros = jnp.zeros(a.shape[:-1] + (LANES - MLA_QK - ROPE_HALF,), a.dtype)
    return jnp.concatenate([a, first_half, zeros], axis=-1)


def _gate_cols(ga, d):
    h = MLSTM_HEADS
    return jnp.concatenate([ga[..., (2 + d) * h:(3 + d) * h], ga[..., d * h:(d + 1) * h]], axis=-1)


def _relayout_w_in(w_in):
    zeros = lambda n: jnp.zeros((D_MODEL, n), F32)
    w = MLSTM_WIDTH
    ga = w_in[:, OFF_GA:OFF_CQ]
    gate_tile = lambda d: jnp.concatenate([_gate_cols(ga, d), zeros(LANES - GATE_ROWS)], axis=-1)
    main = jnp.concatenate([
        w_in[:, OFF_CQ:OFF_KR],
        _rope_copy_pad(jnp.concatenate([zeros(MLA_NOPE), w_in[:, OFF_KR:OFF_GM]], axis=-1)),
        w_in[:, OFF_GM:IN_COLS],
        w_in[:, 0:w], w_in[:, 2 * w:3 * w],
        w_in[:, 3 * w:4 * w],
        gate_tile(0), gate_tile(1)], axis=-1)
    keys_t = (w_in[:, w:2 * w] * (MLSTM_HEAD_DIM ** -0.5)).T
    gates_t = jnp.concatenate([_gate_cols(ga, 0), _gate_cols(ga, 1)], axis=-1).T
    return main.astype(BF16), keys_t.astype(BF16), gates_t.astype(BF16)


def _relayout_gate_b(gb):
    per_dir = jnp.stack([_gate_cols(gb, 0), _gate_cols(gb, 1)])
    col_form = jnp.pad(per_dir, ((0, 0), (0, LANES - GATE_ROWS)))[:, None, :]
    row_form = jnp.broadcast_to(per_dir[:, :, None], (2, GATE_ROWS, LANES))
    return col_form, row_form


def _pad_heads(w, width):
    kdim = w.shape[0]
    w = w.reshape(kdim, MLA_HEADS, width)
    return jnp.pad(w, ((0, 0), (0, 0), (0, HEAD_PAD - width))).reshape(kdim, MLA_HEADS * HEAD_PAD)


def _layer_params(l, w_in, w_out, mlstm_gate_b, mlstm_norm_g, mla_cq_g, mla_ckv_g, mla_w_uq, mla_w_ukv,
                  mla_q_g, mla_k_g, gmlp_ln_g, gmlp_ln_b, gmlp_w_s, gmlp_b_s):
    ukv = mla_w_ukv[l].reshape(MLA_KV_RANK, MLA_HEADS, MLA_NOPE + MLA_V)
    pad1 = lambda g: _rope_copy_pad(g)[None, :]
    w_uq = _rope_copy_pad(mla_w_uq[l].reshape(MLA_Q_RANK, MLA_HEADS, MLA_QK)).reshape(MLA_Q_RANK, -1)
    return dict(
        w_in=_relayout_w_in(w_in[l]),
        gate_b=_relayout_gate_b(mlstm_gate_b[l]),
        mlstm_g=mlstm_norm_g[l][None, :],
        cq_g=mla_cq_g[l][None, :], ckv_g=mla_ckv_g[l][None, :],
        w_uq=w_uq.astype(BF16),
        w_k=_pad_heads(ukv[:, :, :MLA_NOPE].reshape(MLA_KV_RANK, -1), MLA_NOPE).astype(BF16),
        w_v=_pad_heads(ukv[:, :, MLA_NOPE:].reshape(MLA_KV_RANK, -1), MLA_V).astype(BF16),
        q_g=pad1(mla_q_g[l]), k_g=pad1(mla_k_g[l]),
        ln_g=gmlp_ln_g[l][None, :], ln_b=gmlp_ln_b[l][None, :],
        w_s=gmlp_w_s[l].astype(BF16),
        b_s=jnp.repeat(gmlp_b_s[l].T, GMLP_GROUP_DIM, axis=1),
        w_out=w_out[l].astype(BF16),
    )


def kernel(x, c, ctx, c_ctx, ada_w, ada_b, norm1_g, norm2_g, w_in, w_out, mlstm_gate_b, mlstm_norm_g, mla_cq_g, mla_ckv_g, mla_w_uq, mla_w_ukv, mla_q_g, mla_k_g, gmlp_ln_g, gmlp_ln_b, gmlp_w_s, gmlp_b_s, ffn_w1, ffn_w3, ffn_w2, moe_router_w, moe_router_b, moe_w1, moe_w3, moe_w2):
    bsz = x.shape[0]
    assert x.shape[1:] == (SEQ, D_MODEL) and ctx.shape[1:] == (CTX_LEN, D_MODEL)
    mod_rows = -(-(bsz + 1) // 8) * 8
    cvec = jnp.zeros((mod_rows, D_MODEL), F32).at[:bsz].set(c).at[bsz].set(c_ctx)
    mods_all = _ada_call(cvec, ada_w.astype(BF16), ada_b[:, None, :]).reshape(DEPTH, mod_rows, 6, D_MODEL)
    rope_tabs = _rope_tables()
    xall = jnp.concatenate([x, ctx], axis=1)
    moe_bf16 = {}
    for l in range(DEPTH):
        last = l == DEPTH - 1
        p = _layer_params(l, w_in, w_out, mlstm_gate_b, mlstm_norm_g, mla_cq_g, mla_ckv_g, mla_w_uq, mla_w_ukv,
                          mla_q_g, mla_k_g, gmlp_ln_g, gmlp_ln_b, gmlp_w_s, gmlp_b_s)
        mods = mods_all[l]
        z, zgc, zkt, zgt = _inproj_call(xall, mods, norm1_g[l][None, :], *p["w_in"], bsz)
        cast_ws = ()
        if (l + 1) % 2 == 1 and l + 1 < DEPTH:
            jn = (l + 1) // 2
            cast_ws = (moe_w1[jn].reshape(-1, D_FF_EXPERT), moe_w3[jn].reshape(-1, D_FF_EXPERT),
                       moe_w2[jn].reshape(-1, D_MODEL))
        a_mix, *cast = _mlstm_call(z, zgc, zkt, zgt, *p["gate_b"], p["mlstm_g"], cast_ws)
        if cast:
            moe_bf16[(l + 1) // 2] = (cast[0].reshape(N_EXPERTS, D_MODEL, D_FF_EXPERT),
                                      cast[1].reshape(N_EXPERTS, D_MODEL, D_FF_EXPERT),
                                      cast[2].reshape(N_EXPERTS, D_FF_EXPERT, D_MODEL))
        q, k, v = _qkv_call(z, p, rope_tabs)
        t_out = SEQ if last else T_ALL
        b_mix = _attn_call(q, k, v, t_out)
        x1 = _mixout_call(xall, z, a_mix, b_mix, mods, p, bsz, t_out, 1024 if last else 768)
        j = l // 2
        if l % 2 == 0:
            assert not last
            xall = _ffn_call(x1, mods, norm2_g[l][None, :], ffn_w1[j].astype(BF16), ffn_w3[j].astype(BF16),
                             ffn_w2[j].astype(BF16), bsz)
        else:
            assert last
            if j not in moe_bf16:
                moe_bf16[j] = (moe_w1[j].astype(BF16), moe_w3[j].astype(BF16), moe_w2[j].astype(BF16))
            xall = _moe_layer(x1, mods, norm2_g[l][None, :], moe_router_w[j], moe_router_b[j], *moe_bf16[j])
    return xall
```

```python
import functools

import jax
import jax.numpy as jnp
from jax import lax
from jax.experimental import pallas as pl
from jax.experimental.pallas import tpu as pltpu

F32 = jnp.float32
BF16 = jnp.bfloat16
I32 = jnp.int32
HIGHEST = lax.Precision.HIGHEST

D_MODEL = 1024
SEQ = 2048
CTX_LEN = 256
T_ALL = SEQ + CTX_LEN
DEPTH = 2
GRID_W = 64
EPS = 1e-6
MLSTM_HEADS = 4
MLSTM_HEAD_DIM = 64
MLSTM_WIDTH = 256
CHUNK = 128
MLA_HEADS = 8
MLA_Q_RANK = 256
MLA_KV_RANK = 128
MLA_NOPE = 64
MLA_ROPE = 32
MLA_V = 64
MLA_QK = 96
MLA_WIDTH = 512
ROPE_BASE = 10000.0
GMLP_GROUPS = 4
GMLP_GROUP_DIM = 64
GMLP_WIDTH = 256
D_FF = 2816
N_EXPERTS = 8
TOP_K = 2
D_FF_EXPERT = 3584
MOE_BLOCK = 512
OFF_GA = 4 * MLSTM_WIDTH
OFF_CQ = OFF_GA + 4 * MLSTM_HEADS
OFF_CKV = OFF_CQ + MLA_Q_RANK
OFF_KR = OFF_CKV + MLA_KV_RANK
OFF_GM = OFF_KR + MLA_ROPE
IN_COLS = OFF_GM + 2 * GMLP_WIDTH

LANES = 128
HEAD_PAD = LANES
ZC_B = 0
ZC_GM = 512
ZC_QV = 1024
ZC_O = 1536
ZC_G = 1792
Z_COLS = ZC_G + 2 * LANES
GATE_ROWS = 8
VMEM_LIMIT = 56 * 1024 * 1024

N_CHUNKS = T_ALL // CHUNK
N_LAT_CHUNKS = SEQ // CHUNK


def _cparams(sem, vmem=VMEM_LIMIT):
    return pltpu.CompilerParams(dimension_semantics=sem, vmem_limit_bytes=vmem)


def _rms(x, g):
    return x * lax.rsqrt(jnp.mean(x * x, axis=-1, keepdims=True) + EPS) * g


def _silu(x):
    return x * jax.nn.sigmoid(x)


def _modulated_norm(x, g, shift, scale):
    return _rms(x, g) * (1.0 + scale) + shift


def _row_mods(ml_ref, mc_ref, first, tile_start, rows):
    row = tile_start + lax.broadcasted_iota(I32, (rows, 1), 0)
    is_ctx = row >= SEQ
    return tuple(jnp.where(is_ctx, mc_ref[0, first + k:first + k + 1, :], ml_ref[0, first + k:first + k + 1, :])
                 for k in range(3))


def _ada_kernel(c_ref, w_ref, b_ref, o_ref):
    s = _silu(c_ref[...]).astype(BF16)
    o_ref[0] = jnp.dot(s, w_ref[0], preferred_element_type=F32) + b_ref[0]


def _ada_call(cvec, ada_w, ada_b):
    rows = cvec.shape[0]
    return pl.pallas_call(
        _ada_kernel,
        grid=(DEPTH, 6),
        in_specs=[pl.BlockSpec((rows, D_MODEL), lambda l, j: (0, 0)),
                  pl.BlockSpec((1, D_MODEL, D_MODEL), lambda l, j: (l, 0, j)),
                  pl.BlockSpec((1, 1, D_MODEL), lambda l, j: (l, 0, j))],
        out_specs=pl.BlockSpec((1, rows, D_MODEL), lambda l, j: (l, 0, j)),
        out_shape=jax.ShapeDtypeStruct((DEPTH, rows, 6 * D_MODEL), F32),
        compiler_params=_cparams(("arbitrary", "arbitrary")),
        name="adaln",
    )(cvec, ada_w, ada_b)


def _stream_specs(src, tr):
    if not isinstance(src, tuple):
        return [pl.BlockSpec((1, tr, D_MODEL), lambda b, i: (b, i, 0))], [src]
    x, ctx = src
    n_full = SEQ // tr
    tail = SEQ - n_full * tr
    assert tail + CTX_LEN == tr and (n_full * tr) % tail == 0
    return ([pl.BlockSpec((1, tr, D_MODEL), lambda b, i: (b, jnp.minimum(i, n_full - 1), 0)),
             pl.BlockSpec((1, tail, D_MODEL), lambda b, i: (b, n_full * tr // tail, 0)),
             pl.BlockSpec((1, CTX_LEN, D_MODEL), lambda b, i: (b, 0, 0))], [x, x, ctx])


def _stream_tile(src_refs, i, tr):
    if len(src_refs) == 1:
        return src_refs[0][0]
    full, tail, ctx = src_refs
    mixed = jnp.concatenate([tail[0], ctx[0]], axis=0)
    return jnp.where(i == SEQ // tr, mixed, full[0])


def _inproj_kernel(*refs, tr, n_src):
    src_refs = refs[:n_src]
    ml_ref, mc_ref, g_ref, w_ref, wk_ref, wg_ref, z_ref, zgc_ref, zk_ref, zg_ref = refs[n_src:]
    shift, scale, _ = _row_mods(ml_ref, mc_ref, 0, pl.program_id(1) * tr, tr)
    x = _stream_tile(src_refs, pl.program_id(1), tr)
    xn = _modulated_norm(x, g_ref[...], shift, scale).astype(BF16)
    z = jnp.dot(xn, w_ref[...], preferred_element_type=F32)
    z_ref[0] = z[:, 0:ZC_G].astype(z_ref.dtype)
    zgc_ref[0] = z[:, ZC_G:Z_COLS]
    nt = (((1,), (1,)), ((), ()))
    zk_ref[0] = lax.dot_general(wk_ref[...], xn, nt, preferred_element_type=F32).astype(zk_ref.dtype)
    zg_ref[0] = lax.dot_general(wg_ref[...], xn, nt, preferred_element_type=F32)


def _inproj_call(src, mods, norm_g, w_main, w_kt, w_gt, n_ctx_row):
    tr = 768
    src_specs, src_ops = _stream_specs(src, tr)
    bsz = src_ops[0].shape[0]
    const = lambda b, i: (0, 0)
    return pl.pallas_call(
        functools.partial(_inproj_kernel, tr=tr, n_src=len(src_ops)),
        grid=(bsz, T_ALL // tr),
        in_specs=src_specs + [
                  pl.BlockSpec((1, 6, D_MODEL), lambda b, i: (b, 0, 0)),
                  pl.BlockSpec((1, 6, D_MODEL), lambda b, i: (n_ctx_row, 0, 0)),
                  pl.BlockSpec((1, D_MODEL), const),
                  pl.BlockSpec((D_MODEL, Z_COLS), const),
                  pl.BlockSpec((MLSTM_WIDTH, D_MODEL), const),
                  pl.BlockSpec((2 * GATE_ROWS, D_MODEL), const)],
        out_specs=[pl.BlockSpec((1, tr, ZC_G), lambda b, i: (b, i, 0)),
                   pl.BlockSpec((1, tr, Z_COLS - ZC_G), lambda b, i: (b, i, 0)),
                   pl.BlockSpec((1, MLSTM_WIDTH, tr), lambda b, i: (b, 0, i)),
                   pl.BlockSpec((1, 2 * GATE_ROWS, tr), lambda b, i: (b, 0, i))],
        out_shape=[jax.ShapeDtypeStruct((bsz, T_ALL, ZC_G), BF16),
                   jax.ShapeDtypeStruct((bsz, T_ALL, Z_COLS - ZC_G), F32),
                   jax.ShapeDtypeStruct((bsz, MLSTM_WIDTH, T_ALL), BF16),
                   jax.ShapeDtypeStruct((bsz, 2 * GATE_ROWS, T_ALL), F32)],
        compiler_params=_cparams(("parallel", "parallel")),
        name="in_proj",
    )(*src_ops, mods, mods, norm_g, w_main, w_kt, w_gt)


def _scan_chunk(d, j):
    fwd_chunk = (j + N_LAT_CHUNKS) % N_CHUNKS
    return jnp.where(d == 0, fwd_chunk, N_CHUNKS - 1 - j)


MLSTM_ROWS_PER_STEP = 8


def _log_sigmoid(x):
    return jnp.minimum(x, 0.0) - jnp.log1p(jnp.exp(-jnp.abs(x)))


def _bf16_head(v):
    bits = lax.bitcast_convert_type(v, jnp.uint32) & jnp.uint32(0xFFFF0000)
    return lax.bitcast_convert_type(bits, F32)


def _bf16_terms(x, axis):
    hi = _bf16_head(x)
    rest = x - hi
    mid = _bf16_head(rest)
    lo = rest - mid
    return jnp.concatenate([hi, mid, lo], axis=axis).astype(BF16)


def _mlstm_kernel(*refs, n_cast):
    ins, rest = refs[:8], refs[8:]
    cast_in, o_ref, cast_out, scratch = rest[:n_cast], rest[n_cast], rest[n_cast + 1:2 * n_cast + 1], rest[2 * n_cast + 1:]
    for src, dst in zip(cast_in, cast_out):
        dst[...] = src[...].astype(dst.dtype)
    _mlstm_step(*ins, o_ref, *scratch)


def _mlstm_step(qv_ref, og_ref, zg_ref, kt_ref, gt_ref, gb_ref, gbt_ref, ng_ref, o_ref, c_ref, m_ref, hf_ref):
    L = CHUNK
    E = MLSTM_HEAD_DIM
    d = pl.program_id(1)
    j = pl.program_id(2)
    fwd = d == 0
    row0 = pl.multiple_of(_scan_chunk(d, j) * L, L)

    @pl.when(j == 0)
    def _():
        c_ref[...] = jnp.zeros_like(c_ref)
        m_ref[...] = jnp.zeros_like(m_ref)

    r_i = lax.broadcasted_iota(I32, (L, L), 0)
    c_i = lax.broadcasted_iota(I32, (L, L), 1)
    prec = jnp.logical_or(jnp.logical_and(fwd, c_i <= r_i), jnp.logical_and(jnp.logical_not(fwd), c_i >= r_i))
    succ = jnp.logical_or(jnp.logical_and(fwd, r_i <= c_i), jnp.logical_and(jnp.logical_not(fwd), r_i >= c_i))
    lane = lax.broadcasted_iota(I32, (L, LANES), 1)
    sub = lax.broadcasted_iota(I32, (LANES, L), 0)
    low_lane = lane < E
    low_sub = sub < E

    row_idx = lax.broadcasted_iota(I32, (L, LANES), 0)
    rows = range(qv_ref.shape[0])
    pairs = range(MLSTM_HEADS // 2)
    h_pairs = {}
    for bb in rows:
        gates_c = zg_ref[bb] + gb_ref[0]
        gates_r = gt_ref[bb] + gbt_ref[0]
        logf_r = _log_sigmoid(gates_r)
        logf_c = jnp.concatenate([logf_r, jnp.zeros((LANES - GATE_ROWS, L), F32)], axis=0).T
        parts_c = jnp.dot(prec.astype(BF16), _bf16_terms(logf_c, 1), preferred_element_type=F32)
        b_c = parts_c[:, 0:LANES] + parts_c[:, LANES:2 * LANES] + parts_c[:, 2 * LANES:3 * LANES]
        c_c = pltpu.roll(gates_c, LANES - MLSTM_HEADS, 1) - b_c
        m_row = m_ref[bb]
        cm = c_c
        for sh in (1, 2, 4, 8, 16, 32, 64):
            prev = jnp.where(fwd, jnp.where(row_idx >= sh, pltpu.roll(cm, sh, 0), -jnp.inf),
                             jnp.where(row_idx < L - sh, pltpu.roll(cm, L - sh, 0), -jnp.inf))
            cm = jnp.maximum(cm, prev)
        m_c = jnp.maximum(cm, m_row)
        a_c = jnp.exp(m_row - m_c)
        floor_c = jnp.exp(-(b_c + m_c))
        m_last = jnp.maximum(jnp.max(c_c, axis=0, keepdims=True), m_row)
        b_last = jnp.where(fwd, b_c[L - 1:L, :], b_c[0:1, :])
        m_ref[bb] = b_last + m_last
        decay_row = jnp.exp(m_row - m_last)
        parts_r = jnp.dot(_bf16_terms(logf_r, 0), succ.astype(BF16), preferred_element_type=F32)
        b_r = parts_r[0:GATE_ROWS] + parts_r[GATE_ROWS:2 * GATE_ROWS] + parts_r[2 * GATE_ROWS:3 * GATE_ROWS]
        c_r = gates_r[MLSTM_HEADS:2 * MLSTM_HEADS, :] - b_r[0:MLSTM_HEADS, :]
        w_r = [jnp.exp(c_r[h:h + 1, :] - m_last[:, h:h + 1]) for h in range(MLSTM_HEADS)]

        for pj in pairs:
            tile = slice(pj * LANES, (pj + 1) * LANES)
            q_t = qv_ref[bb, :, tile]
            v_t = qv_ref[bb, :, MLSTM_WIDTH + pj * LANES:MLSTM_WIDTH + (pj + 1) * LANES]
            kt_t = kt_ref[bb, tile, :]
            c_old = c_ref[bb, pj]
            q_both = jnp.concatenate([jnp.where(low_lane, q_t, 0.0), jnp.where(low_lane, 0.0, q_t)],
                                     axis=0).astype(BF16)
            kc = jnp.concatenate([kt_t.astype(BF16), c_old.astype(BF16)], axis=1)
            qk_qc = jnp.dot(q_both, kc, preferred_element_type=F32)
            hx = []
            upd = []
            for par in range(2):
                h = 2 * pj + par
                own_lane = low_lane if par == 0 else jnp.logical_not(low_lane)
                own_sub = low_sub if par == 0 else jnp.logical_not(low_sub)
                den_lane = E if par == 0 else 0
                v_ext = jnp.where(own_lane, v_t, jnp.where(lane == den_lane, 1.0, 0.0)).astype(BF16)
                sqk = qk_qc[par * L:(par + 1) * L, 0:L]
                q_c = qk_qc[par * L:(par + 1) * L, L:L + LANES]
                dmat = jnp.where(prec, jnp.exp(c_r[h:h + 1, :] - m_c[:, h:h + 1]), 0.0)
                kw_t = jnp.where(own_sub, kt_t, 0.0) * w_r[h]
                pv_upd = jnp.dot(jnp.concatenate([sqk * dmat, kw_t], axis=0).astype(BF16), v_ext,
                                 preferred_element_type=F32)
                nd = pv_upd[0:L] + a_c[:, h:h + 1] * q_c
                den = nd[:, den_lane:den_lane + 1]
                hx.append(nd / jnp.maximum(jnp.abs(den), floor_c[:, h:h + 1]))
                upd.append(pv_upd[L:L + LANES])
            decay = jnp.where(low_sub, decay_row[:, 2 * pj:2 * pj + 1], decay_row[:, 2 * pj + 1:2 * pj + 2])
            c_ref[bb, pj] = decay * c_old + upd[0] + upd[1]
            h_pairs[bb, pj] = jnp.where(low_lane, hx[0], hx[1])

    @pl.when(fwd)
    def _():
        for (bb, pj), h_pair in h_pairs.items():
            hf_ref[bb, pl.ds(row0, L), pj * LANES:(pj + 1) * LANES] = h_pair

    @pl.when(jnp.logical_not(fwd))
    def _():
        for (bb, pj), h_pair in h_pairs.items():
            tile = slice(pj * LANES, (pj + 1) * LANES)
            hs = hf_ref[bb, pl.ds(row0, L), tile] + h_pair
            sq = hs * hs
            ss = jnp.where(low_lane,
                           jnp.sum(jnp.where(low_lane, sq, 0.0), axis=-1, keepdims=True),
                           jnp.sum(jnp.where(low_lane, 0.0, sq), axis=-1, keepdims=True))
            y = hs * lax.rsqrt(ss * (1.0 / E) + EPS) * ng_ref[:, tile]
            o_ref[bb, :, tile] = (jax.nn.sigmoid(og_ref[bb, :, tile].astype(F32)) * y).astype(o_ref.dtype)


def _mlstm_call(z, zgc, zkt, zgt, gate_b, gate_bt, norm_g, cast_ws=()):
    bsz = z.shape[0]
    nb = max(r for r in range(1, MLSTM_ROWS_PER_STEP + 1) if bsz % r == 0)
    last = N_CHUNKS - 1
    chunk = lambda b, d, j: _scan_chunk(d, j)
    steps = (bsz // nb) * 2 * N_CHUNKS
    n_blk = 1 << (steps.bit_length() - 1)
    cast_idx = lambda b, d, j: (jnp.minimum((b * 2 + d) * N_CHUNKS + j, n_blk - 1), 0)
    cast_specs = [pl.BlockSpec((w.shape[0] // n_blk, w.shape[1]), cast_idx) for w in cast_ws]
    assert all(w.shape[0] % (16 * n_blk) == 0 for w in cast_ws)
    return pl.pallas_call(
        functools.partial(_mlstm_kernel, n_cast=len(cast_ws)),
        grid=(bsz // nb, 2, N_CHUNKS),
        in_specs=[pl.BlockSpec((nb, CHUNK, 2 * MLSTM_WIDTH), lambda b, d, j: (b, chunk(b, d, j), ZC_QV // 512)),
                  pl.BlockSpec((nb, CHUNK, MLSTM_WIDTH), lambda b, d, j: (b, chunk(b, d, j), ZC_O // MLSTM_WIDTH)),
                  pl.BlockSpec((nb, CHUNK, LANES), lambda b, d, j: (b, chunk(b, d, j), d)),
                  pl.BlockSpec((nb, MLSTM_WIDTH, CHUNK), lambda b, d, j: (b, 0, chunk(b, d, j))),
                  pl.BlockSpec((nb, GATE_ROWS, CHUNK), lambda b, d, j: (b, d, chunk(b, d, j))),
                  pl.BlockSpec((1, 1, LANES), lambda b, d, j: (d, 0, 0)),
                  pl.BlockSpec((1, GATE_ROWS, LANES), lambda b, d, j: (d, 0, 0)),
                  pl.BlockSpec((1, MLSTM_WIDTH), lambda b, d, j: (0, 0))] + cast_specs,
        out_specs=[pl.BlockSpec((nb, CHUNK, MLSTM_WIDTH),
                                lambda b, d, j: (b, jnp.where(d == 0, last, last - j), 0))] + cast_specs,
        out_shape=[jax.ShapeDtypeStruct((bsz, T_ALL, MLSTM_WIDTH), BF16)]
        + [jax.ShapeDtypeStruct(w.shape, BF16) for w in cast_ws],
        scratch_shapes=[pltpu.VMEM((nb, MLSTM_HEADS // 2, LANES, LANES), F32),
                        pltpu.VMEM((nb, 1, LANES), F32),
                        pltpu.VMEM((nb, T_ALL, MLSTM_WIDTH), F32)],
        compiler_params=_cparams(("arbitrary", "arbitrary", "arbitrary")),
        name="mlstm",
    )(z, z, zgc, zkt, zgt, gate_b, gate_bt, norm_g, *cast_ws)


ROPE_HALF = MLA_ROPE // 2


def _rope(t, cos_t, sin_t):
    return t * cos_t + pltpu.roll(t, LANES - ROPE_HALF, 1) * sin_t


HEAD_PAIR = 2 * HEAD_PAD


def _head_rms_scale(raw, ones_ref):
    sq = (raw * raw).astype(BF16)
    ss = jnp.concatenate([jnp.dot(sq[:, j * HEAD_PAIR:(j + 1) * HEAD_PAIR], ones_ref[...],
                                  preferred_element_type=F32) for j in range(MLA_HEADS // 2)], axis=-1)
    return lax.rsqrt(ss * (1.0 / MLA_QK) + EPS)


def _qkv_kernel(z_ref, cqg_ref, ckvg_ref, wq_ref, wk_ref, wv_ref, qg_ref, kg_ref, ones_ref, ct_ref, st_ref,
                q_ref, k_ref, v_ref, *, tr):
    cqn = _rms(z_ref[0, :, 0:MLA_Q_RANK].astype(F32), cqg_ref[...]).astype(BF16)
    ckvn = _rms(z_ref[0, :, MLA_Q_RANK:MLA_Q_RANK + MLA_KV_RANK].astype(F32), ckvg_ref[...]).astype(BF16)
    k_rope = z_ref[0, :, MLA_Q_RANK + MLA_KV_RANK:MLA_Q_RANK + MLA_KV_RANK + LANES].astype(F32)
    q_raw = jnp.dot(cqn, wq_ref[...], preferred_element_type=F32)
    k_raw = jnp.dot(ckvn, wk_ref[...], preferred_element_type=F32) + jnp.concatenate([k_rope] * MLA_HEADS, axis=-1)
    v_raw = jnp.dot(ckvn, wv_ref[...], preferred_element_type=F32)
    q_n = q_raw * _head_rms_scale(q_raw, ones_ref)
    k_n = k_raw * _head_rms_scale(k_raw, ones_ref)
    cos_t = ct_ref[...]
    sin_t = st_ref[...]
    lane = lax.broadcasted_iota(I32, (tr, LANES), 1)
    for h in range(MLA_HEADS):
        sl = slice(h * HEAD_PAD, (h + 1) * HEAD_PAD)
        qh = _rope(q_n[:, sl] * qg_ref[...], cos_t, sin_t)
        q_ref[0, :, sl] = (qh * (MLA_QK ** -0.5)).astype(BF16)
        kh = _rope(k_n[:, sl] * kg_ref[...], cos_t, sin_t)
        k_ref[0, :, sl] = kh.astype(BF16)
        v_ref[0, :, sl] = jnp.where(lane == MLA_V, 1.0, v_raw[:, sl]).astype(BF16)


def _qkv_call(z, p, rope_tabs):
    bsz = z.shape[0]
    tr = 768
    hw = MLA_HEADS * HEAD_PAD
    const = lambda b, i: (0, 0)
    out = jax.ShapeDtypeStruct((bsz, T_ALL, hw), BF16)
    tab = pl.BlockSpec((tr, LANES), lambda b, i: (i, 0))
    ospec = pl.BlockSpec((1, tr, hw), lambda b, i: (b, i, 0))
    return pl.pallas_call(
        functools.partial(_qkv_kernel, tr=tr),
        grid=(bsz, T_ALL // tr),
        in_specs=[pl.BlockSpec((1, tr, 512), lambda b, i: (b, i, ZC_B // 512)),
                  pl.BlockSpec((1, MLA_Q_RANK), const), pl.BlockSpec((1, MLA_KV_RANK), const),
                  pl.BlockSpec((MLA_Q_RANK, hw), const), pl.BlockSpec((MLA_KV_RANK, hw), const),
                  pl.BlockSpec((MLA_KV_RANK, hw), const),
                  pl.BlockSpec((1, LANES), const), pl.BlockSpec((1, LANES), const),
                  pl.BlockSpec((HEAD_PAIR, HEAD_PAIR), const),
                  tab, tab],
        out_specs=[ospec, ospec, ospec],
        out_shape=[out, out, out],
        compiler_params=_cparams(("parallel", "parallel")),
        name="mla_qkv",
    )(z, p["cq_g"], p["ckv_g"], p["w_uq"], p["w_k"], p["w_v"], p["q_g"], p["k_g"], _head_ones(), *rope_tabs)


def _head_ones():
    r = jnp.arange(HEAD_PAIR)
    same_head = (r[:, None] // HEAD_PAD) == (r[None, :] // HEAD_PAD)
    real_row = (r[:, None] % HEAD_PAD) < MLA_QK
    return jnp.logical_and(same_head, real_row).astype(BF16)


ATT_TQ = 512
assert SEQ % ATT_TQ == 0 and CTX_LEN <= ATT_TQ


def _attn_kernel(q_ref, k_ref, v_ref, o_ref):
    i = pl.program_id(1)

    def run(rows, k0):
        def scores(hh):
            sl = slice(hh * HEAD_PAD, (hh + 1) * HEAD_PAD)
            return lax.dot_general(q_ref[0, 0:rows, sl], k_ref[0, k0:T_ALL, sl], (((1,), (1,)), ((), ())),
                                   preferred_element_type=F32)

        s = scores(0)
        for hh in range(MLA_HEADS):
            s_next = scores(hh + 1) if hh + 1 < MLA_HEADS else None
            p = jnp.exp(s - jnp.max(s, axis=-1, keepdims=True)).astype(BF16)
            oe = jnp.dot(p, v_ref[0, k0:T_ALL, hh * HEAD_PAD:(hh + 1) * HEAD_PAD], preferred_element_type=F32)
            o_ref[0, 0:rows, hh * MLA_V:(hh + 1) * MLA_V] = (
                oe[:, 0:MLA_V] / oe[:, MLA_V:MLA_V + 1]).astype(o_ref.dtype)
            s = s_next

    @pl.when(i < SEQ // ATT_TQ)
    def _():
        run(ATT_TQ, 0)

    @pl.when(i >= SEQ // ATT_TQ)
    def _():
        run(CTX_LEN, SEQ)
        o_ref[0, CTX_LEN:ATT_TQ, :] = jnp.zeros((ATT_TQ - CTX_LEN, MLA_WIDTH), o_ref.dtype)


def _attn_call(q, k, v, t_out):
    bsz = q.shape[0]
    w = MLA_HEADS * HEAD_PAD
    return pl.pallas_call(
        _attn_kernel,
        grid=(bsz, pl.cdiv(t_out, ATT_TQ)),
        in_specs=[pl.BlockSpec((1, ATT_TQ, w), lambda b, i: (b, i, 0)),
                  pl.BlockSpec((1, T_ALL, w), lambda b, i: (b, 0, 0)),
                  pl.BlockSpec((1, T_ALL, w), lambda b, i: (b, 0, 0))],
        out_specs=pl.BlockSpec((1, ATT_TQ, MLA_WIDTH), lambda b, i: (b, i, 0)),
        out_shape=jax.ShapeDtypeStruct((bsz, t_out, MLA_WIDTH), BF16),
        compiler_params=_cparams(("parallel", "arbitrary")),
        name="mla_attn",
    )(q, k, v)


def _mixout_kernel(*refs, tr, n_src):
    src_refs = refs[:n_src]
    zg_ref, a_ref, b_ref, ml_ref, mc_ref, lng_ref, lnb_ref, ws_ref, bs_ref, wo_ref, o_ref, cm_ref = refs[n_src:]
    gd = GMLP_GROUP_DIM
    act = jax.nn.gelu(zg_ref[0].astype(F32))
    u = act[:, 0:GMLP_WIDTH]
    vv = act[:, GMLP_WIDTH:2 * GMLP_WIDTH]
    mu = jnp.mean(vv, axis=-1, keepdims=True)
    var = jnp.mean(jnp.square(vv - mu), axis=-1, keepdims=True)
    vn = ((vv - mu) * lax.rsqrt(var + EPS) * lng_ref[...] + lnb_ref[...]).astype(BF16)
    for c in range(tr // CHUNK):
        rows = slice(c * CHUNK, (c + 1) * CHUNK)
        for g in range(GMLP_GROUPS):
            cols = slice(g * gd, (g + 1) * gd)
            sv = jnp.dot(ws_ref[g], vn[rows, cols], preferred_element_type=F32) + bs_ref[:, cols]
            cm_ref[rows, cols] = u[rows, cols] * sv
    y = jnp.dot(a_ref[0].astype(BF16), wo_ref[0:MLSTM_WIDTH, :], preferred_element_type=F32)
    y += jnp.dot(b_ref[0].astype(BF16), wo_ref[MLSTM_WIDTH:MLSTM_WIDTH + MLA_WIDTH, :],
                 preferred_element_type=F32)
    y += jnp.dot(cm_ref[...].astype(BF16), wo_ref[MLSTM_WIDTH + MLA_WIDTH:, :], preferred_element_type=F32)
    _, _, gate = _row_mods(ml_ref, mc_ref, 0, pl.program_id(1) * tr, tr)
    o_ref[0] = _stream_tile(src_refs, pl.program_id(1), tr) + gate * y


def _mixout_call(src, z, a_mix, b_mix, mods, p, n_ctx_row, t_out, tr):
    src_specs, src_ops = _stream_specs(src, tr)
    bsz = src_ops[0].shape[0]
    const = lambda b, i: (0, 0)
    return pl.pallas_call(
        functools.partial(_mixout_kernel, tr=tr, n_src=len(src_ops)),
        grid=(bsz, t_out // tr),
        in_specs=src_specs + [
                  pl.BlockSpec((1, tr, 2 * GMLP_WIDTH), lambda b, i: (b, i, ZC_GM // (2 * GMLP_WIDTH))),
                  pl.BlockSpec((1, tr, MLSTM_WIDTH), lambda b, i: (b, i, 0)),
                  pl.BlockSpec((1, tr, MLA_WIDTH), lambda b, i: (b, i, 0)),
                  pl.BlockSpec((1, 6, D_MODEL), lambda b, i: (b, 0, 0)),
                  pl.BlockSpec((1, 6, D_MODEL), lambda b, i: (n_ctx_row, 0, 0)),
                  pl.BlockSpec((1, GMLP_WIDTH), const), pl.BlockSpec((1, GMLP_WIDTH), const),
                  pl.BlockSpec((GMLP_GROUPS, CHUNK, CHUNK), lambda b, i: (0, 0, 0)),
                  pl.BlockSpec((CHUNK, GMLP_WIDTH), const),
                  pl.BlockSpec((D_MODEL, D_MODEL), const)],
        out_specs=pl.BlockSpec((1, tr, D_MODEL), lambda b, i: (b, i, 0)),
        out_shape=jax.ShapeDtypeStruct((bsz, t_out, D_MODEL), F32),
        scratch_shapes=[pltpu.VMEM((tr, GMLP_WIDTH), F32)],
        compiler_params=_cparams(("parallel", "parallel")),
        name="mix_out",
    )(*src_ops, z, a_mix, b_mix, mods, mods, p["ln_g"], p["ln_b"], p["w_s"], p["b_s"], p["w_out"])


FFN_SPLIT = 2


def _ffn_kernel(x_ref, ml_ref, mc_ref, g_ref, w1_ref, w3_ref, w2_ref, o_ref, *, tr):
    shift, scale, gate = _row_mods(ml_ref, mc_ref, 3, pl.program_id(1) * tr, tr)
    x = x_ref[0]
    h = _modulated_norm(x, g_ref[...], shift, scale).astype(BF16)
    fc = D_FF // FFN_SPLIT
    y = jnp.zeros((tr, D_MODEL), F32)
    for f in range(FFN_SPLIT):
        cols = slice(f * fc, (f + 1) * fc)
        h1 = jnp.dot(h, w1_ref[:, cols], preferred_element_type=F32)
        h3 = jnp.dot(h, w3_ref[:, cols], preferred_element_type=F32)
        y += jnp.dot((_silu(h1) * h3).astype(BF16), w2_ref[cols, :], preferred_element_type=F32)
    o_ref[0] = x + gate * y


def _ffn_call(x1, mods, norm_g, w1, w3, w2, n_ctx_row):
    bsz = x1.shape[0]
    tr = 768
    const = lambda b, i: (0, 0)
    resident = pl.Buffered(1)
    return pl.pallas_call(
        functools.partial(_ffn_kernel, tr=tr),
        grid=(bsz, T_ALL // tr),
        in_specs=[pl.BlockSpec((1, tr, D_MODEL), lambda b, i: (b, i, 0)),
                  pl.BlockSpec((1, 6, D_MODEL), lambda b, i: (b, 0, 0)),
                  pl.BlockSpec((1, 6, D_MODEL), lambda b, i: (n_ctx_row, 0, 0)),
                  pl.BlockSpec((1, D_MODEL), const),
                  pl.BlockSpec((D_MODEL, D_FF), const, pipeline_mode=resident),
                  pl.BlockSpec((D_MODEL, D_FF), const, pipeline_mode=resident),
                  pl.BlockSpec((D_FF, D_MODEL), const, pipeline_mode=resident)],
        out_specs=pl.BlockSpec((1, tr, D_MODEL), lambda b, i: (b, i, 0)),
        out_shape=jax.ShapeDtypeStruct((bsz, T_ALL, D_MODEL), F32),
        compiler_params=_cparams(("parallel", "parallel")),
        name="dense_ffn",
    )(x1, mods, mods, norm_g, w1, w3, w2)


ROUTE_TR = 1024


def _router_kernel(x_ref, ml_ref, g_ref, rw_ref, rb_ref, h_ref, e_ref, r_ref, gt_ref, cnt_ref, run_ref, *, tr):
    step = pl.program_id(0) * pl.num_programs(1) + pl.program_id(1)

    @pl.when(step == 0)
    def _():
        run_ref[...] = jnp.zeros_like(run_ref)

    h = _modulated_norm(x_ref[0], g_ref[...], ml_ref[0, 3:4, :], ml_ref[0, 4:5, :])
    h_ref[...] = h
    lane = lax.broadcasted_iota(I32, (tr, LANES), 1)
    h_hi = _bf16_head(h)
    h_lo = (h - h_hi).astype(BF16)
    w = rw_ref[...]
    w_hi = _bf16_head(w)
    w_lo = (w - w_hi).astype(BF16)
    first = jnp.dot(h_hi.astype(BF16), jnp.concatenate([w_hi, w_lo.astype(F32)], axis=1).astype(BF16),
                    preferred_element_type=F32)
    logits = (first[:, 0:LANES] + first[:, LANES:2 * LANES]
              + jnp.dot(h_lo, w_hi.astype(BF16), preferred_element_type=F32)) + rb_ref[...]
    logits = jnp.where(lane < N_EXPERTS, logits, -jnp.inf)
    m1 = jnp.max(logits, axis=-1, keepdims=True)
    e1 = jnp.min(jnp.where(logits == m1, lane, LANES), axis=-1, keepdims=True)
    rest = jnp.where(lane == e1, -jnp.inf, logits)
    m2 = jnp.max(rest, axis=-1, keepdims=True)
    e2 = jnp.min(jnp.where(rest == m2, lane, LANES), axis=-1, keepdims=True)
    ex = jnp.exp(m2 - m1)
    g1 = 1.0 / (1.0 + ex)
    g2 = ex / (1.0 + ex)
    onehot = jnp.logical_or(lane == e1, lane == e2 + N_EXPERTS)
    oh = onehot.astype(F32)
    r_i = lax.broadcasted_iota(I32, (tr, tr), 0)
    c_i = lax.broadcasted_iota(I32, (tr, tr), 1)
    before = jnp.dot((c_i < r_i).astype(BF16), oh.astype(BF16), preferred_element_type=F32)
    tot = jnp.sum(oh, axis=0, keepdims=True)
    tot0_shift = pltpu.roll(tot, N_EXPERTS, 1)
    run = run_ref[...]
    first_half = lax.broadcasted_iota(I32, (1, LANES), 1) < N_EXPERTS
    offs = run + jnp.where(first_half, 0.0, tot0_shift)
    ranks = oh * (before + offs)
    rank1 = jnp.sum(jnp.where(lane < N_EXPERTS, ranks, 0.0), axis=-1, keepdims=True)
    rank2 = jnp.sum(jnp.where(lane >= N_EXPERTS, ranks, 0.0), axis=-1, keepdims=True)
    col = lax.broadcasted_iota(I32, (tr, TOP_K), 1)
    e_ref[...] = jnp.where(col == 0, e1, e2)
    r_ref[...] = jnp.where(col == 0, rank1, rank2).astype(I32)
    gt_ref[...] = jnp.where(col == 0, g1, g2)
    both = tot + jnp.where(first_half, pltpu.roll(tot, LANES - N_EXPERTS, 1), tot0_shift)
    new_run = run + both
    run_ref[...] = new_run
    cnt_ref[...] = new_run.astype(I32)


def _router_call(x1, mods, norm_g, rw_p, rb_p):
    bsz = x1.shape[0]
    tr = ROUTE_TR
    n = bsz * SEQ
    nt = SEQ // tr
    const = lambda b, i: (0, 0)
    tok = lambda b, i: (b * nt + i, 0)
    return pl.pallas_call(
        functools.partial(_router_kernel, tr=tr),
        grid=(bsz, nt),
        in_specs=[pl.BlockSpec((1, tr, D_MODEL), lambda b, i: (b, i, 0)),
                  pl.BlockSpec((1, 6, D_MODEL), lambda b, i: (b, 0, 0)),
                  pl.BlockSpec((1, D_MODEL), const),
                  pl.BlockSpec((D_MODEL, LANES), const),
                  pl.BlockSpec((1, LANES), const)],
        out_specs=[pl.BlockSpec((tr, D_MODEL), tok),
                   pl.BlockSpec((tr, TOP_K), tok), pl.BlockSpec((tr, TOP_K), tok), pl.BlockSpec((tr, TOP_K), tok),
                   pl.BlockSpec((1, LANES), const)],
        out_shape=[jax.ShapeDtypeStruct((n, D_MODEL), F32),
                   jax.ShapeDtypeStruct((n, TOP_K), I32), jax.ShapeDtypeStruct((n, TOP_K), I32),
                   jax.ShapeDtypeStruct((n, TOP_K), F32),
                   jax.ShapeDtypeStruct((1, LANES), I32)],
        scratch_shapes=[pltpu.VMEM((1, LANES), F32)],
        compiler_params=_cparams(("arbitrary", "arbitrary")),
        name="moe_router",
    )(x1, mods, norm_g, rw_p, rb_p)


DISPATCH_TD = 512
SUBLANES = 8
ZERO_BURST = MOE_BLOCK + SUBLANES
N_ZERO_BURSTS = 2 * N_EXPERTS
DMA_ISSUE_UNROLL = 8


def _row_copy(src, src_row, dst, dst_row, sem):
    return pltpu.make_async_copy(src.at[pl.ds(src_row, 1)], dst.at[pl.ds(dst_row, 1)], sem)


def _dispatch_kernel(pad_ref, dest_ref, h_ref, xs_out, zero_ref, sem):
    @pl.when(pl.program_id(0) == 0)
    def _():
        zero_ref[...] = jnp.zeros_like(zero_ref)
        for e in range(N_ZERO_BURSTS):
            start = pl.multiple_of(pad_ref[e], SUBLANES)
            burst = pltpu.make_async_copy(zero_ref, xs_out.at[pl.ds(start, ZERO_BURST)], sem)
            burst.start()
            burst.wait()

    def issue(r, carry):
        for kk in range(TOP_K):
            _row_copy(h_ref, r, xs_out, dest_ref[kk, r], sem).start()
        return carry

    lax.fori_loop(0, DISPATCH_TD, issue, 0, unroll=DMA_ISSUE_UNROLL)

    for kk in range(TOP_K):
        pltpu.make_async_copy(h_ref, xs_out.at[pl.ds(0, DISPATCH_TD)], sem).wait()


def _dispatch_call(pad_start, dest_t, h2, cap):
    n = h2.shape[0]
    grid_spec = pltpu.PrefetchScalarGridSpec(
        num_scalar_prefetch=1,
        grid=(n // DISPATCH_TD,),
        in_specs=[pl.BlockSpec((TOP_K, DISPATCH_TD), lambda i, pad: (0, i), memory_space=pltpu.SMEM),
                  pl.BlockSpec((DISPATCH_TD, D_MODEL), lambda i, pad: (i, 0))],
        out_specs=pl.BlockSpec(memory_space=pl.ANY),
        scratch_shapes=[pltpu.VMEM((ZERO_BURST, D_MODEL), F32), pltpu.SemaphoreType.DMA(())],
    )
    return pl.pallas_call(
        _dispatch_kernel,
        grid_spec=grid_spec,
        out_shape=jax.ShapeDtypeStruct((cap, D_MODEL), F32),
        compiler_params=_cparams(("arbitrary",)),
        name="moe_dispatch",
    )(pad_start, dest_t, h2)


EXPERT_FSPLIT = 2


def _expert_kernel(be_ref, na_ref, xs_ref, w1_ref, w3_ref, w2_ref, ys_ref):
    i = pl.program_id(0)
    f = pl.program_id(1)
    active = i < na_ref[0]

    @pl.when(jnp.logical_and(active, f == 0))
    def _():
        ys_ref[...] = jnp.zeros_like(ys_ref)

    @pl.when(active)
    def _():
        x = xs_ref[...].astype(BF16)
        h1 = jnp.dot(x, w1_ref[0], preferred_element_type=F32)
        h3 = jnp.dot(x, w3_ref[0], preferred_element_type=F32)
        ys_ref[...] += jnp.dot((_silu(h1) * h3).astype(BF16), w2_ref[0], preferred_element_type=F32)

    @pl.when(jnp.logical_not(active))
    def _():
        ys_ref[...] = jnp.zeros_like(ys_ref)


def _expert_call(block_expert, n_active, xs, w1, w3, w2):
    cap = xs.shape[0]
    nb = cap // MOE_BLOCK
    fc = D_FF_EXPERT // EXPERT_FSPLIT
    last_f = EXPERT_FSPLIT - 1

    def blk(i, na):
        return jnp.minimum(i, jnp.maximum(na[0] - 1, 0))

    def fidx(i, f, na):
        return jnp.where(i < na[0], f, last_f)

    grid_spec = pltpu.PrefetchScalarGridSpec(
        num_scalar_prefetch=2,
        grid=(nb, EXPERT_FSPLIT),
        in_specs=[pl.BlockSpec((MOE_BLOCK, D_MODEL), lambda i, f, be, na: (blk(i, na), 0)),
                  pl.BlockSpec((1, D_MODEL, fc), lambda i, f, be, na: (be[blk(i, na)], 0, fidx(i, f, na))),
                  pl.BlockSpec((1, D_MODEL, fc), lambda i, f, be, na: (be[blk(i, na)], 0, fidx(i, f, na))),
                  pl.BlockSpec((1, fc, D_MODEL), lambda i, f, be, na: (be[blk(i, na)], fidx(i, f, na), 0))],
        out_specs=pl.BlockSpec((MOE_BLOCK, D_MODEL), lambda i, f, be, na: (i, 0)),
    )
    return pl.pallas_call(
        _expert_kernel,
        grid_spec=grid_spec,
        out_shape=jax.ShapeDtypeStruct((cap, D_MODEL), F32),
        compiler_params=_cparams(("arbitrary", "arbitrary")),
        name="moe_experts",
    )(block_expert, n_active, xs, w1, w3, w2)


COMBINE_TC = 256


def _combine_kernel(dest_ref, dest_next_ref, x_ref, gt_ref, ml_ref, ys_hbm, o_ref, buf_ref, sem):
    i = pl.program_id(0)
    slot = i % 2

    def gather(d_ref, s):
        def issue(r, carry):
            for kk in range(TOP_K):
                _row_copy(ys_hbm, d_ref[kk, r], buf_ref.at[s, kk], r, sem.at[s]).start()
            return carry

        lax.fori_loop(0, COMBINE_TC, issue, 0, unroll=DMA_ISSUE_UNROLL)

    @pl.when(i == 0)
    def _():
        gather(dest_ref, 0)

    @pl.when(i + 1 < pl.num_programs(0))
    def _():
        gather(dest_next_ref, 1 - slot)

    for kk in range(TOP_K):
        pltpu.make_async_copy(ys_hbm.at[pl.ds(0, COMBINE_TC)], buf_ref.at[slot, kk], sem.at[slot]).wait()
    g = gt_ref[...]
    y = buf_ref[slot, 0] * g[:, 0:1] + buf_ref[slot, 1] * g[:, 1:2]
    o_ref[...] = x_ref[...] + ml_ref[0, 5:6, :] * y


def _combine_call(dest_t, x1_flat, gates, mods, ys):
    n = x1_flat.shape[0]
    tc = COMBINE_TC
    per_batch = SEQ // tc
    last = n // tc - 1
    return pl.pallas_call(
        _combine_kernel,
        grid=(n // tc,),
        in_specs=[pl.BlockSpec((TOP_K, tc), lambda i: (0, i), memory_space=pltpu.SMEM),
                  pl.BlockSpec((TOP_K, tc), lambda i: (0, jnp.minimum(i + 1, last)), memory_space=pltpu.SMEM),
                  pl.BlockSpec((tc, D_MODEL), lambda i: (i, 0)),
                  pl.BlockSpec((tc, TOP_K), lambda i: (i, 0)),
                  pl.BlockSpec((1, 6, D_MODEL), lambda i: (i // per_batch, 0, 0)),
                  pl.BlockSpec(memory_space=pl.ANY)],
        out_specs=pl.BlockSpec((tc, D_MODEL), lambda i: (i, 0)),
        out_shape=jax.ShapeDtypeStruct((n, D_MODEL), F32),
        scratch_shapes=[pltpu.VMEM((2, TOP_K, tc, D_MODEL), F32), pltpu.SemaphoreType.DMA((2,))],
        compiler_params=_cparams(("arbitrary",)),
        name="moe_combine",
    )(dest_t, dest_t, x1_flat, gates, mods, ys)


def _moe_layer(x1, mods, norm_g, router_w, router_b, w1, w3, w2):
    bsz = x1.shape[0]
    n = bsz * SEQ
    cap = (n * TOP_K + MOE_BLOCK - 1) // MOE_BLOCK * MOE_BLOCK + N_EXPERTS * MOE_BLOCK
    rw_p = jnp.zeros((D_MODEL, LANES), F32).at[:, :N_EXPERTS].set(router_w)
    rb_p = jnp.zeros((1, LANES), F32).at[0, :N_EXPERTS].set(router_b)
    h2, e_idx, rank, gates, counts = _router_call(x1, mods, norm_g, rw_p, rb_p)
    counts = counts[0, :N_EXPERTS]
    padded = (counts + MOE_BLOCK - 1) // MOE_BLOCK * MOE_BLOCK
    padded_end = jnp.cumsum(padded)
    base = padded_end - padded
    dest_t = (base[e_idx] + rank).T.astype(I32)
    n_blocks = cap // MOE_BLOCK
    block_start = jnp.arange(n_blocks, dtype=I32) * MOE_BLOCK
    block_expert = jnp.minimum(jnp.sum(padded_end[None, :] <= block_start[:, None], axis=1),
                               N_EXPERTS - 1).astype(I32)
    n_active = (padded_end[-1:] // MOE_BLOCK).astype(I32)
    tail = padded_end[-1] + jnp.arange(N_EXPERTS, dtype=I32) * MOE_BLOCK
    pad_start = jnp.concatenate([base + counts, tail])
    pad_start = (jnp.minimum(pad_start, cap - ZERO_BURST) // SUBLANES * SUBLANES).astype(I32)
    xs = _dispatch_call(pad_start, dest_t, h2, cap)
    ys = _expert_call(block_expert, n_active, xs, w1, w3, w2)
    out = _combine_call(dest_t, x1.reshape(n, D_MODEL), gates, mods, ys)
    return out.reshape(bsz, SEQ, D_MODEL)


def _rope_tables():
    rows = SEQ // GRID_W
    row = jnp.repeat(jnp.arange(rows), GRID_W).astype(F32)
    col = jnp.tile(jnp.arange(GRID_W), rows).astype(F32)
    n_freq = MLA_ROPE // 4
    inv = ROPE_BASE ** (-jnp.arange(n_freq, dtype=F32) / n_freq)
    ang = jnp.concatenate([row[:, None] * inv, col[:, None] * inv], axis=-1)
    cos, sin = jnp.cos(ang), jnp.sin(ang)
    ones = jnp.ones((SEQ, MLA_NOPE), F32)
    pad = jnp.zeros((SEQ, LANES - MLA_QK), F32)
    zn = jnp.zeros((SEQ, MLA_NOPE), F32)
    cos_t = jnp.concatenate([ones, cos, cos, pad], axis=-1)
    sin_t = jnp.concatenate([zn, -sin, sin, pad], axis=-1)
    ident = jnp.concatenate([jnp.ones((CTX_LEN, MLA_QK), F32), jnp.zeros((CTX_LEN, LANES - MLA_QK), F32)], axis=-1)
    zero = jnp.zeros((CTX_LEN, LANES), F32)
    return jnp.concatenate([cos_t, ident], axis=0), jnp.concatenate([sin_t, zero], axis=0)


def _rope_copy_pad(a):
    first_half = a[..., MLA_NOPE:MLA_NOPE + ROPE_HALF]
    zeros = jnp.zeros(a.shape[:-1] + (LANES - MLA_QK - ROPE_HALF,), a.dtype)
    return jnp.concatenate([a, first_half, zeros], axis=-1)


def _gate_cols(ga, d):
    h = MLSTM_HEADS
    return jnp.concatenate([ga[..., (2 + d) * h:(3 + d) * h], ga[..., d * h:(d + 1) * h]], axis=-1)


def _relayout_w_in(w_in):
    zeros = lambda n: jnp.zeros((D_MODEL, n), F32)
    w = MLSTM_WIDTH
    ga = w_in[:, OFF_GA:OFF_CQ]
    gate_tile = lambda d: jnp.concatenate([_gate_cols(ga, d), zeros(LANES - GATE_ROWS)], axis=-1)
    main = jnp.concatenate([
        w_in[:, OFF_CQ:OFF_KR],
        _rope_copy_pad(jnp.concatenate([zeros(MLA_NOPE), w_in[:, OFF_KR:OFF_GM]], axis=-1)),
        w_in[:, OFF_GM:IN_COLS],
        w_in[:, 0:w], w_in[:, 2 * w:3 * w],
        w_in[:, 3 * w:4 * w],
        gate_tile(0), gate_tile(1)], axis=-1)
    keys_t = (w_in[:, w:2 * w] * (MLSTM_HEAD_DIM ** -0.5)).T
    gates_t = jnp.concatenate([_gate_cols(ga, 0), _gate_cols(ga, 1)], axis=-1).T
    return main.astype(BF16), keys_t.astype(BF16), gates_t.astype(BF16)


def _relayout_gate_b(gb):
    per_dir = jnp.stack([_gate_cols(gb, 0), _gate_cols(gb, 1)])
    col_form = jnp.pad(per_dir, ((0, 0), (0, LANES - GATE_ROWS)))[:, None, :]
    row_form = jnp.broadcast_to(per_dir[:, :, None], (2, GATE_ROWS, LANES))
    return col_form, row_form


def _pad_heads(w, width):
    kdim = w.shape[0]
    w = w.reshape(kdim, MLA_HEADS, width)
    return jnp.pad(w, ((0, 0), (0, 0), (0, HEAD_PAD - width))).reshape(kdim, MLA_HEADS * HEAD_PAD)


def _layer_params(l, w_in, w_out, mlstm_gate_b, mlstm_norm_g, mla_cq_g, mla_ckv_g, mla_w_uq, mla_w_ukv,
                  mla_q_g, mla_k_g, gmlp_ln_g, gmlp_ln_b, gmlp_w_s, gmlp_b_s):
    ukv = mla_w_ukv[l].reshape(MLA_KV_RANK, MLA_HEADS, MLA_NOPE + MLA_V)
    pad1 = lambda g: _rope_copy_pad(g)[None, :]
    w_uq = _rope_copy_pad(mla_w_uq[l].reshape(MLA_Q_RANK, MLA_HEADS, MLA_QK)).reshape(MLA_Q_RANK, -1)
    return dict(
        w_in=_relayout_w_in(w_in[l]),
        gate_b=_relayout_gate_b(mlstm_gate_b[l]),
        mlstm_g=mlstm_norm_g[l][None, :],
        cq_g=mla_cq_g[l][None, :], ckv_g=mla_ckv_g[l][None, :],
        w_uq=w_uq.astype(BF16),
        w_k=_pad_heads(ukv[:, :, :MLA_NOPE].reshape(MLA_KV_RANK, -1), MLA_NOPE).astype(BF16),
        w_v=_pad_heads(ukv[:, :, MLA_NOPE:].reshape(MLA_KV_RANK, -1), MLA_V).astype(BF16),
        q_g=pad1(mla_q_g[l]), k_g=pad1(mla_k_g[l]),
        ln_g=gmlp_ln_g[l][None, :], ln_b=gmlp_ln_b[l][None, :],
        w_s=gmlp_w_s[l].astype(BF16),
        b_s=jnp.repeat(gmlp_b_s[l].T, GMLP_GROUP_DIM, axis=1),
        w_out=w_out[l].astype(BF16),
    )


def kernel(x, c, ctx, c_ctx, ada_w, ada_b, norm1_g, norm2_g, w_in, w_out, mlstm_gate_b, mlstm_norm_g, mla_cq_g, mla_ckv_g, mla_w_uq, mla_w_ukv, mla_q_g, mla_k_g, gmlp_ln_g, gmlp_ln_b, gmlp_w_s, gmlp_b_s, ffn_w1, ffn_w3, ffn_w2, moe_router_w, moe_router_b, moe_w1, moe_w3, moe_w2):
    bsz = x.shape[0]
    assert x.shape[1:] == (SEQ, D_MODEL) and ctx.shape[1:] == (CTX_LEN, D_MODEL)
    mod_rows = -(-(bsz + 1) // 8) * 8
    cvec = jnp.zeros((mod_rows, D_MODEL), F32).at[:bsz].set(c).at[bsz].set(c_ctx)
    mods_all = _ada_call(cvec, ada_w.astype(BF16), ada_b[:, None, :]).reshape(DEPTH, mod_rows, 6, D_MODEL)
    rope_tabs = _rope_tables()
    xall = (x, ctx) if DEPTH > 1 else jnp.concatenate([x, ctx], axis=1)
    moe_bf16 = {}
    for l in range(DEPTH):
        last = l == DEPTH - 1
        p = _layer_params(l, w_in, w_out, mlstm_gate_b, mlstm_norm_g, mla_cq_g, mla_ckv_g, mla_w_uq, mla_w_ukv,
                          mla_q_g, mla_k_g, gmlp_ln_g, gmlp_ln_b, gmlp_w_s, gmlp_b_s)
        mods = mods_all[l]
        z, zgc, zkt, zgt = _inproj_call(xall, mods, norm1_g[l][None, :], *p["w_in"], bsz)
        cast_ws = ()
        if (l + 1) % 2 == 1 and l + 1 < DEPTH:
            jn = (l + 1) // 2
            cast_ws = (moe_w1[jn].reshape(-1, D_FF_EXPERT), moe_w3[jn].reshape(-1, D_FF_EXPERT),
                       moe_w2[jn].reshape(-1, D_MODEL))
        a_mix, *cast = _mlstm_call(z, zgc, zkt, zgt, *p["gate_b"], p["mlstm_g"], cast_ws)
        if cast:
            moe_bf16[(l + 1) // 2] = (cast[0].reshape(N_EXPERTS, D_MODEL, D_FF_EXPERT),
                                      cast[1].reshape(N_EXPERTS, D_MODEL, D_FF_EXPERT),
                                      cast[2].reshape(N_EXPERTS, D_FF_EXPERT, D_MODEL))
        q, k, v = _qkv_call(z, p, rope_tabs)
        t_out = SEQ if last else T_ALL
        b_mix = _attn_call(q, k, v, t_out)
        x1 = _mixout_call(xall, z, a_mix, b_mix, mods, p, bsz, t_out, 1024 if last else 768)
        j = l // 2
        if l % 2 == 0:
            assert not last
            xall = _ffn_call(x1, mods, norm2_g[l][None, :], ffn_w1[j].astype(BF16), ffn_w3[j].astype(BF16),
                             ffn_w2[j].astype(BF16), bsz)
        else:
            assert last
            if j not in moe_bf16:
                moe_bf16[j] = (moe_w1[j].astype(BF16), moe_w3[j].astype(BF16), moe_w2[j].astype(BF16))
            xall = _moe_layer(x1, mods, norm2_g[l][None, :], moe_router_w[j], moe_router_b[j], *moe_bf16[j])
    return xall
```

```python
import functools

import jax
import jax.numpy as jnp
from jax import lax
from jax.experimental import pallas as pl
from jax.experimental.pallas import tpu as pltpu

F32 = jnp.float32
BF16 = jnp.bfloat16
I32 = jnp.int32
HIGHEST = lax.Precision.HIGHEST

D_MODEL = 1024
SEQ = 2048
CTX_LEN = 256
T_ALL = SEQ + CTX_LEN
DEPTH = 2
GRID_W = 64
EPS = 1e-6
MLSTM_HEADS = 4
MLSTM_HEAD_DIM = 64
MLSTM_WIDTH = 256
CHUNK = 128
MLA_HEADS = 8
MLA_Q_RANK = 256
MLA_KV_RANK = 128
MLA_NOPE = 64
MLA_ROPE = 32
MLA_V = 64
MLA_QK = 96
MLA_WIDTH = 512
ROPE_BASE = 10000.0
GMLP_GROUPS = 4
GMLP_GROUP_DIM = 64
GMLP_WIDTH = 256
D_FF = 2816
N_EXPERTS = 8
TOP_K = 2
D_FF_EXPERT = 3584
MOE_BLOCK = 512
OFF_GA = 4 * MLSTM_WIDTH
OFF_CQ = OFF_GA + 4 * MLSTM_HEADS
OFF_CKV = OFF_CQ + MLA_Q_RANK
OFF_KR = OFF_CKV + MLA_KV_RANK
OFF_GM = OFF_KR + MLA_ROPE
IN_COLS = OFF_GM + 2 * GMLP_WIDTH

LANES = 128
HEAD_PAD = LANES
ZC_B = 0
ZC_GM = 512
ZC_QV = 1024
ZC_O = 1536
ZC_G = 1792
Z_COLS = ZC_G + 2 * LANES
GATE_ROWS = 8
VMEM_LIMIT = 56 * 1024 * 1024

N_CHUNKS = T_ALL // CHUNK
N_LAT_CHUNKS = SEQ // CHUNK


def _cparams(sem, vmem=VMEM_LIMIT):
    return pltpu.CompilerParams(dimension_semantics=sem, vmem_limit_bytes=vmem)


def _rms(x, g):
    return x * lax.rsqrt(jnp.mean(x * x, axis=-1, keepdims=True) + EPS) * g


def _silu(x):
    return x * jax.nn.sigmoid(x)


def _modulated_norm(x, g, shift, scale):
    return _rms(x, g) * (1.0 + scale) + shift


def _row_mods(ml_ref, mc_ref, first, tile_start, rows):
    row = tile_start + lax.broadcasted_iota(I32, (rows, 1), 0)
    is_ctx = row >= SEQ
    return tuple(jnp.where(is_ctx, mc_ref[0, first + k:first + k + 1, :], ml_ref[0, first + k:first + k + 1, :])
                 for k in range(3))


def _ada_kernel(c_ref, w_ref, b_ref, o_ref):
    s = _silu(c_ref[...]).astype(BF16)
    o_ref[0] = jnp.dot(s, w_ref[0], preferred_element_type=F32) + b_ref[0]


def _ada_call(cvec, ada_w, ada_b):
    rows = cvec.shape[0]
    return pl.pallas_call(
        _ada_kernel,
        grid=(DEPTH, 6),
        in_specs=[pl.BlockSpec((rows, D_MODEL), lambda l, j: (0, 0)),
                  pl.BlockSpec((1, D_MODEL, D_MODEL), lambda l, j: (l, 0, j)),
                  pl.BlockSpec((1, 1, D_MODEL), lambda l, j: (l, 0, j))],
        out_specs=pl.BlockSpec((1, rows, D_MODEL), lambda l, j: (l, 0, j)),
        out_shape=jax.ShapeDtypeStruct((DEPTH, rows, 6 * D_MODEL), F32),
        compiler_params=_cparams(("arbitrary", "arbitrary")),
        name="adaln",
    )(cvec, ada_w, ada_b)


def _stream_specs(src, tr):
    if not isinstance(src, tuple):
        return [pl.BlockSpec((1, tr, D_MODEL), lambda b, i: (b, i, 0))], [src]
    x, ctx = src
    n_full = SEQ // tr
    tail = SEQ - n_full * tr
    assert tail + CTX_LEN == tr and (n_full * tr) % tail == 0
    return ([pl.BlockSpec((1, tr, D_MODEL), lambda b, i: (b, jnp.minimum(i, n_full - 1), 0)),
             pl.BlockSpec((1, tail, D_MODEL), lambda b, i: (b, n_full * tr // tail, 0)),
             pl.BlockSpec((1, CTX_LEN, D_MODEL), lambda b, i: (b, 0, 0))], [x, x, ctx])


def _stream_tile(src_refs, i, tr):
    if len(src_refs) == 1:
        return src_refs[0][0]
    full, tail, ctx = src_refs
    mixed = jnp.concatenate([tail[0], ctx[0]], axis=0)
    return jnp.where(i == SEQ // tr, mixed, full[0])


def _inproj_kernel(*refs, tr, n_src):
    src_refs = refs[:n_src]
    ml_ref, mc_ref, g_ref, w_ref, wk_ref, wg_ref, z_ref, zgc_ref, zk_ref, zg_ref = refs[n_src:]
    shift, scale, _ = _row_mods(ml_ref, mc_ref, 0, pl.program_id(1) * tr, tr)
    x = _stream_tile(src_refs, pl.program_id(1), tr)
    xn = _modulated_norm(x, g_ref[...], shift, scale).astype(BF16)
    z = jnp.dot(xn, w_ref[...], preferred_element_type=F32)
    z_ref[0] = z[:, 0:ZC_G].astype(z_ref.dtype)
    zgc_ref[0] = z[:, ZC_G:Z_COLS]
    nt = (((1,), (1,)), ((), ()))
    zk_ref[0] = lax.dot_general(wk_ref[...], xn, nt, preferred_element_type=F32).astype(zk_ref.dtype)
    zg_ref[0] = lax.dot_general(wg_ref[...], xn, nt, preferred_element_type=F32)


def _inproj_call(src, mods, norm_g, w_main, w_kt, w_gt, n_ctx_row):
    tr = 768
    src_specs, src_ops = _stream_specs(src, tr)
    bsz = src_ops[0].shape[0]
    const = lambda b, i: (0, 0)
    return pl.pallas_call(
        functools.partial(_inproj_kernel, tr=tr, n_src=len(src_ops)),
        grid=(bsz, T_ALL // tr),
        in_specs=src_specs + [
                  pl.BlockSpec((1, 6, D_MODEL), lambda b, i: (b, 0, 0)),
                  pl.BlockSpec((1, 6, D_MODEL), lambda b, i: (n_ctx_row, 0, 0)),
                  pl.BlockSpec((1, D_MODEL), const),
                  pl.BlockSpec((D_MODEL, Z_COLS), const),
                  pl.BlockSpec((MLSTM_WIDTH, D_MODEL), const),
                  pl.BlockSpec((2 * GATE_ROWS, D_MODEL), const)],
        out_specs=[pl.BlockSpec((1, tr, ZC_G), lambda b, i: (b, i, 0)),
                   pl.BlockSpec((1, tr, Z_COLS - ZC_G), lambda b, i: (b, i, 0)),
                   pl.BlockSpec((1, MLSTM_WIDTH, tr), lambda b, i: (b, 0, i)),
                   pl.BlockSpec((1, 2 * GATE_ROWS, tr), lambda b, i: (b, 0, i))],
        out_shape=[jax.ShapeDtypeStruct((bsz, T_ALL, ZC_G), BF16),
                   jax.ShapeDtypeStruct((bsz, T_ALL, Z_COLS - ZC_G), F32),
                   jax.ShapeDtypeStruct((bsz, MLSTM_WIDTH, T_ALL), BF16),
                   jax.ShapeDtypeStruct((bsz, 2 * GATE_ROWS, T_ALL), F32)],
        compiler_params=_cparams(("parallel", "parallel")),
        name="in_proj",
    )(*src_ops, mods, mods, norm_g, w_main, w_kt, w_gt)


def _scan_chunk(d, j):
    fwd_chunk = (j + N_LAT_CHUNKS) % N_CHUNKS
    return jnp.where(d == 0, fwd_chunk, N_CHUNKS - 1 - j)


MLSTM_ROWS_PER_STEP = 8


def _log_sigmoid(x):
    return jnp.minimum(x, 0.0) - jnp.log1p(jnp.exp(-jnp.abs(x)))


def _bf16_head(v):
    bits = lax.bitcast_convert_type(v, jnp.uint32) & jnp.uint32(0xFFFF0000)
    return lax.bitcast_convert_type(bits, F32)


def _bf16_terms(x, axis):
    hi = _bf16_head(x)
    rest = x - hi
    mid = _bf16_head(rest)
    lo = rest - mid
    return jnp.concatenate([hi, mid, lo], axis=axis).astype(BF16)


def _mlstm_kernel(*refs, n_cast):
    ins, rest = refs[:8], refs[8:]
    cast_in, o_ref, cast_out, scratch = rest[:n_cast], rest[n_cast], rest[n_cast + 1:2 * n_cast + 1], rest[2 * n_cast + 1:]
    for src, dst in zip(cast_in, cast_out):
        dst[...] = src[...].astype(dst.dtype)
    _mlstm_step(*ins, o_ref, *scratch)


def _mlstm_step(qv_ref, og_ref, zg_ref, kt_ref, gt_ref, gb_ref, gbt_ref, ng_ref, o_ref, c_ref, m_ref, hf_ref):
    L = CHUNK
    E = MLSTM_HEAD_DIM
    d = pl.program_id(1)
    j = pl.program_id(2)
    fwd = d == 0
    row0 = pl.multiple_of(_scan_chunk(d, j) * L, L)

    @pl.when(j == 0)
    def _():
        c_ref[...] = jnp.zeros_like(c_ref)
        m_ref[...] = jnp.zeros_like(m_ref)

    r_i = lax.broadcasted_iota(I32, (L, L), 0)
    c_i = lax.broadcasted_iota(I32, (L, L), 1)
    prec = jnp.logical_or(jnp.logical_and(fwd, c_i <= r_i), jnp.logical_and(jnp.logical_not(fwd), c_i >= r_i))
    succ = jnp.logical_or(jnp.logical_and(fwd, r_i <= c_i), jnp.logical_and(jnp.logical_not(fwd), r_i >= c_i))
    lane = lax.broadcasted_iota(I32, (L, LANES), 1)
    sub = lax.broadcasted_iota(I32, (LANES, L), 0)
    low_lane = lane < E
    low_sub = sub < E

    row_idx = lax.broadcasted_iota(I32, (L, LANES), 0)
    rows = range(qv_ref.shape[0])
    pairs = range(MLSTM_HEADS // 2)
    h_pairs = {}
    for bb in rows:
        gates_c = zg_ref[bb] + gb_ref[0]
        gates_r = gt_ref[bb] + gbt_ref[0]
        logf_r = _log_sigmoid(gates_r)
        logf_c = jnp.concatenate([logf_r, jnp.zeros((LANES - GATE_ROWS, L), F32)], axis=0).T
        parts_c = jnp.dot(prec.astype(BF16), _bf16_terms(logf_c, 1), preferred_element_type=F32)
        b_c = parts_c[:, 0:LANES] + parts_c[:, LANES:2 * LANES] + parts_c[:, 2 * LANES:3 * LANES]
        c_c = pltpu.roll(gates_c, LANES - MLSTM_HEADS, 1) - b_c
        m_row = m_ref[bb]
        cm = c_c
        for sh in (1, 2, 4, 8, 16, 32, 64):
            prev = jnp.where(fwd, jnp.where(row_idx >= sh, pltpu.roll(cm, sh, 0), -jnp.inf),
                             jnp.where(row_idx < L - sh, pltpu.roll(cm, L - sh, 0), -jnp.inf))
            cm = jnp.maximum(cm, prev)
        m_c = jnp.maximum(cm, m_row)
        a_c = jnp.exp(m_row - m_c)
        floor_c = jnp.exp(-(b_c + m_c))
        m_last = jnp.maximum(jnp.max(c_c, axis=0, keepdims=True), m_row)
        b_last = jnp.where(fwd, b_c[L - 1:L, :], b_c[0:1, :])
        m_ref[bb] = b_last + m_last
        decay_row = jnp.exp(m_row - m_last)
        parts_r = jnp.dot(_bf16_terms(logf_r, 0), succ.astype(BF16), preferred_element_type=F32)
        b_r = parts_r[0:GATE_ROWS] + parts_r[GATE_ROWS:2 * GATE_ROWS] + parts_r[2 * GATE_ROWS:3 * GATE_ROWS]
        c_r = gates_r[MLSTM_HEADS:2 * MLSTM_HEADS, :] - b_r[0:MLSTM_HEADS, :]
        w_r = [jnp.exp(c_r[h:h + 1, :] - m_last[:, h:h + 1]) for h in range(MLSTM_HEADS)]

        for pj in pairs:
            tile = slice(pj * LANES, (pj + 1) * LANES)
            q_t = qv_ref[bb, :, tile]
            v_t = qv_ref[bb, :, MLSTM_WIDTH + pj * LANES:MLSTM_WIDTH + (pj + 1) * LANES]
            kt_t = kt_ref[bb, tile, :]
            c_old = c_ref[bb, pj]
            q_both = jnp.concatenate([jnp.where(low_lane, q_t, 0.0), jnp.where(low_lane, 0.0, q_t)],
                                     axis=0).astype(BF16)
            kc = jnp.concatenate([kt_t.astype(BF16), c_old.astype(BF16)], axis=1)
            qk_qc = jnp.dot(q_both, kc, preferred_element_type=F32)
            hx = []
            upd = []
            for par in range(2):
                h = 2 * pj + par
                own_lane = low_lane if par == 0 else jnp.logical_not(low_lane)
                own_sub = low_sub if par == 0 else jnp.logical_not(low_sub)
                den_lane = E if par == 0 else 0
                v_ext = jnp.where(own_lane, v_t, jnp.where(lane == den_lane, 1.0, 0.0)).astype(BF16)
                sqk = qk_qc[par * L:(par + 1) * L, 0:L]
                q_c = qk_qc[par * L:(par + 1) * L, L:L + LANES]
                dmat = jnp.where(prec, jnp.exp(c_r[h:h + 1, :] - m_c[:, h:h + 1]), 0.0)
                kw_t = jnp.where(own_sub, kt_t, 0.0) * w_r[h]
                pv_upd = jnp.dot(jnp.concatenate([sqk * dmat, kw_t], axis=0).astype(BF16), v_ext,
                                 preferred_element_type=F32)
                nd = pv_upd[0:L] + a_c[:, h:h + 1] * q_c
                den = nd[:, den_lane:den_lane + 1]
                hx.append(nd / jnp.maximum(jnp.abs(den), floor_c[:, h:h + 1]))
                upd.append(pv_upd[L:L + LANES])
            decay = jnp.where(low_sub, decay_row[:, 2 * pj:2 * pj + 1], decay_row[:, 2 * pj + 1:2 * pj + 2])
            c_ref[bb, pj] = decay * c_old + upd[0] + upd[1]
            h_pairs[bb, pj] = jnp.where(low_lane, hx[0], hx[1])

    @pl.when(fwd)
    def _():
        for (bb, pj), h_pair in h_pairs.items():
            hf_ref[bb, pl.ds(row0, L), pj * LANES:(pj + 1) * LANES] = h_pair

    @pl.when(jnp.logical_not(fwd))
    def _():
        for (bb, pj), h_pair in h_pairs.items():
            tile = slice(pj * LANES, (pj + 1) * LANES)
            hs = hf_ref[bb, pl.ds(row0, L), tile] + h_pair
            sq = hs * hs
            ss = jnp.where(low_lane,
                           jnp.sum(jnp.where(low_lane, sq, 0.0), axis=-1, keepdims=True),
                           jnp.sum(jnp.where(low_lane, 0.0, sq), axis=-1, keepdims=True))
            y = hs * lax.rsqrt(ss * (1.0 / E) + EPS) * ng_ref[:, tile]
            o_ref[bb, :, tile] = (jax.nn.sigmoid(og_ref[bb, :, tile].astype(F32)) * y).astype(o_ref.dtype)


def _mlstm_call(z, zgc, zkt, zgt, gate_b, gate_bt, norm_g, cast_ws=()):
    bsz = z.shape[0]
    nb = max(r for r in range(1, MLSTM_ROWS_PER_STEP + 1) if bsz % r == 0)
    last = N_CHUNKS - 1
    chunk = lambda b, d, j: _scan_chunk(d, j)
    steps = (bsz // nb) * 2 * N_CHUNKS
    n_blk = 1 << (steps.bit_length() - 1)
    cast_idx = lambda b, d, j: (jnp.minimum((b * 2 + d) * N_CHUNKS + j, n_blk - 1), 0)
    cast_specs = [pl.BlockSpec((w.shape[0] // n_blk, w.shape[1]), cast_idx) for w in cast_ws]
    assert all(w.shape[0] % (16 * n_blk) == 0 for w in cast_ws)
    return pl.pallas_call(
        functools.partial(_mlstm_kernel, n_cast=len(cast_ws)),
        grid=(bsz // nb, 2, N_CHUNKS),
        in_specs=[pl.BlockSpec((nb, CHUNK, 2 * MLSTM_WIDTH), lambda b, d, j: (b, chunk(b, d, j), ZC_QV // 512)),
                  pl.BlockSpec((nb, CHUNK, MLSTM_WIDTH), lambda b, d, j: (b, chunk(b, d, j), ZC_O // MLSTM_WIDTH)),
                  pl.BlockSpec((nb, CHUNK, LANES), lambda b, d, j: (b, chunk(b, d, j), d)),
                  pl.BlockSpec((nb, MLSTM_WIDTH, CHUNK), lambda b, d, j: (b, 0, chunk(b, d, j))),
                  pl.BlockSpec((nb, GATE_ROWS, CHUNK), lambda b, d, j: (b, d, chunk(b, d, j))),
                  pl.BlockSpec((1, 1, LANES), lambda b, d, j: (d, 0, 0)),
                  pl.BlockSpec((1, GATE_ROWS, LANES), lambda b, d, j: (d, 0, 0)),
                  pl.BlockSpec((1, MLSTM_WIDTH), lambda b, d, j: (0, 0))] + cast_specs,
        out_specs=[pl.BlockSpec((nb, CHUNK, MLSTM_WIDTH),
                                lambda b, d, j: (b, jnp.where(d == 0, last, last - j), 0))] + cast_specs,
        out_shape=[jax.ShapeDtypeStruct((bsz, T_ALL, MLSTM_WIDTH), BF16)]
        + [jax.ShapeDtypeStruct(w.shape, BF16) for w in cast_ws],
        scratch_shapes=[pltpu.VMEM((nb, MLSTM_HEADS // 2, LANES, LANES), F32),
                        pltpu.VMEM((nb, 1, LANES), F32),
                        pltpu.VMEM((nb, T_ALL, MLSTM_WIDTH), F32)],
        compiler_params=_cparams(("arbitrary", "arbitrary", "arbitrary")),
        name="mlstm",
    )(z, z, zgc, zkt, zgt, gate_b, gate_bt, norm_g, *cast_ws)


ROPE_HALF = MLA_ROPE // 2


def _rope(t, cos_t, sin_t):
    return t * cos_t + pltpu.roll(t, LANES - ROPE_HALF, 1) * sin_t


HEAD_PAIR = 2 * HEAD_PAD
ATT_DEN_LANE = 0
assert MLA_NOPE + MLA_V == HEAD_PAD


def _head_rms_scale(raw, ones_ref):
    sq = (raw * raw).astype(BF16)
    ss = jnp.concatenate([jnp.dot(sq[:, j * HEAD_PAIR:(j + 1) * HEAD_PAIR], ones_ref[...],
                                  preferred_element_type=F32) for j in range(MLA_HEADS // 2)], axis=-1)
    return lax.rsqrt(ss * (1.0 / MLA_QK) + EPS)


def _qkv_kernel(z_ref, cqg_ref, ckvg_ref, wq_ref, wkv_ref, qg_ref, kg_ref, ones_ref, ct_ref, st_ref,
                q_ref, k_ref, v_ref, *, tr):
    cqn = _rms(z_ref[0, :, 0:MLA_Q_RANK].astype(F32), cqg_ref[...]).astype(BF16)
    ckvn = _rms(z_ref[0, :, MLA_Q_RANK:MLA_Q_RANK + MLA_KV_RANK].astype(F32), ckvg_ref[...]).astype(BF16)
    k_rope = z_ref[0, :, MLA_Q_RANK + MLA_KV_RANK:MLA_Q_RANK + MLA_KV_RANK + LANES].astype(F32)
    q_raw = jnp.dot(cqn, wq_ref[...], preferred_element_type=F32)
    kv_raw = jnp.dot(ckvn, wkv_ref[...], preferred_element_type=F32)
    lane_all = lax.broadcasted_iota(I32, (tr, MLA_HEADS * HEAD_PAD), 1) % HEAD_PAD
    k_raw = jnp.where(lane_all < MLA_NOPE, kv_raw, jnp.concatenate([k_rope] * MLA_HEADS, axis=-1))
    q_n = q_raw * _head_rms_scale(q_raw, ones_ref)
    k_n = k_raw * _head_rms_scale(k_raw, ones_ref)
    cos_t = ct_ref[...]
    sin_t = st_ref[...]
    lane = lax.broadcasted_iota(I32, (tr, LANES), 1)
    for h in range(MLA_HEADS):
        sl = slice(h * HEAD_PAD, (h + 1) * HEAD_PAD)
        qh = _rope(q_n[:, sl] * qg_ref[...], cos_t, sin_t)
        q_ref[0, :, sl] = (qh * (MLA_QK ** -0.5)).astype(BF16)
        kh = _rope(k_n[:, sl] * kg_ref[...], cos_t, sin_t)
        k_ref[0, :, sl] = kh.astype(BF16)
        v_ref[0, :, sl] = jnp.where(lane >= MLA_NOPE, kv_raw[:, sl],
                                    jnp.where(lane == ATT_DEN_LANE, 1.0, 0.0)).astype(BF16)


def _qkv_call(z, p, rope_tabs):
    bsz = z.shape[0]
    tr = 768
    hw = MLA_HEADS * HEAD_PAD
    const = lambda b, i: (0, 0)
    out = jax.ShapeDtypeStruct((bsz, T_ALL, hw), BF16)
    tab = pl.BlockSpec((tr, LANES), lambda b, i: (i, 0))
    ospec = pl.BlockSpec((1, tr, hw), lambda b, i: (b, i, 0))
    return pl.pallas_call(
        functools.partial(_qkv_kernel, tr=tr),
        grid=(bsz, T_ALL // tr),
        in_specs=[pl.BlockSpec((1, tr, 512), lambda b, i: (b, i, ZC_B // 512)),
                  pl.BlockSpec((1, MLA_Q_RANK), const), pl.BlockSpec((1, MLA_KV_RANK), const),
                  pl.BlockSpec((MLA_Q_RANK, hw), const), pl.BlockSpec((MLA_KV_RANK, hw), const),
                  pl.BlockSpec((1, LANES), const), pl.BlockSpec((1, LANES), const),
                  pl.BlockSpec((HEAD_PAIR, HEAD_PAIR), const),
                  tab, tab],
        out_specs=[ospec, ospec, ospec],
        out_shape=[out, out, out],
        compiler_params=_cparams(("parallel", "parallel")),
        name="mla_qkv",
    )(z, p["cq_g"], p["ckv_g"], p["w_uq"], p["w_kv"], p["q_g"], p["k_g"], _head_ones(), *rope_tabs)


def _head_ones():
    r = jnp.arange(HEAD_PAIR)
    same_head = (r[:, None] // HEAD_PAD) == (r[None, :] // HEAD_PAD)
    real_row = (r[:, None] % HEAD_PAD) < MLA_QK
    return jnp.logical_and(same_head, real_row).astype(BF16)


ATT_TQ = 1024
assert SEQ % ATT_TQ == 0 and CTX_LEN <= ATT_TQ


def _attn_kernel(q_ref, k_ref, v_ref, o_ref):
    i = pl.program_id(1)

    def run(rows, k0):
        def scores(hh):
            sl = slice(hh * HEAD_PAD, (hh + 1) * HEAD_PAD)
            return lax.dot_general(q_ref[0, 0:rows, sl], k_ref[0, k0:T_ALL, sl], (((1,), (1,)), ((), ())),
                                   preferred_element_type=F32)

        s = scores(0)
        for hh in range(MLA_HEADS):
            s_next = scores(hh + 1) if hh + 1 < MLA_HEADS else None
            p = jnp.exp(s - jnp.max(s, axis=-1, keepdims=True)).astype(BF16)
            oe = jnp.dot(p, v_ref[0, k0:T_ALL, hh * HEAD_PAD:(hh + 1) * HEAD_PAD], preferred_element_type=F32)
            o_ref[0, 0:rows, hh * MLA_V:(hh + 1) * MLA_V] = (
                oe[:, MLA_NOPE:MLA_NOPE + MLA_V] / oe[:, ATT_DEN_LANE:ATT_DEN_LANE + 1]).astype(o_ref.dtype)
            s = s_next

    @pl.when(i < SEQ // ATT_TQ)
    def _():
        run(ATT_TQ, 0)

    @pl.when(i >= SEQ // ATT_TQ)
    def _():
        run(CTX_LEN, SEQ)
        o_ref[0, CTX_LEN:ATT_TQ, :] = jnp.zeros((ATT_TQ - CTX_LEN, MLA_WIDTH), o_ref.dtype)


def _attn_call(q, k, v, t_out):
    bsz = q.shape[0]
    w = MLA_HEADS * HEAD_PAD
    return pl.pallas_call(
        _attn_kernel,
        grid=(bsz, pl.cdiv(t_out, ATT_TQ)),
        in_specs=[pl.BlockSpec((1, ATT_TQ, w), lambda b, i: (b, i, 0)),
                  pl.BlockSpec((1, T_ALL, w), lambda b, i: (b, 0, 0)),
                  pl.BlockSpec((1, T_ALL, w), lambda b, i: (b, 0, 0))],
        out_specs=pl.BlockSpec((1, ATT_TQ, MLA_WIDTH), lambda b, i: (b, i, 0)),
        out_shape=jax.ShapeDtypeStruct((bsz, t_out, MLA_WIDTH), BF16),
        compiler_params=_cparams(("parallel", "arbitrary")),
        name="mla_attn",
    )(q, k, v)


def _mixout_kernel(*refs, tr, n_src):
    src_refs = refs[:n_src]
    zg_ref, a_ref, b_ref, ml_ref, mc_ref, lng_ref, lnb_ref, ws_ref, bs_ref, wo_ref, o_ref, cm_ref = refs[n_src:]
    gd = GMLP_GROUP_DIM
    act = jax.nn.gelu(zg_ref[0].astype(F32))
    u = act[:, 0:GMLP_WIDTH]
    vv = act[:, GMLP_WIDTH:2 * GMLP_WIDTH]
    mu = jnp.mean(vv, axis=-1, keepdims=True)
    var = jnp.mean(jnp.square(vv - mu), axis=-1, keepdims=True)
    vn = ((vv - mu) * lax.rsqrt(var + EPS) * lng_ref[...] + lnb_ref[...]).astype(BF16)
    for c in range(tr // CHUNK):
        rows = slice(c * CHUNK, (c + 1) * CHUNK)
        for g in range(GMLP_GROUPS):
            cols = slice(g * gd, (g + 1) * gd)
            sv = jnp.dot(ws_ref[g], vn[rows, cols], preferred_element_type=F32) + bs_ref[:, cols]
            cm_ref[rows, cols] = u[rows, cols] * sv
    y = jnp.dot(a_ref[0].astype(BF16), wo_ref[0:MLSTM_WIDTH, :], preferred_element_type=F32)
    y += jnp.dot(b_ref[0].astype(BF16), wo_ref[MLSTM_WIDTH:MLSTM_WIDTH + MLA_WIDTH, :],
                 preferred_element_type=F32)
    y += jnp.dot(cm_ref[...].astype(BF16), wo_ref[MLSTM_WIDTH + MLA_WIDTH:, :], preferred_element_type=F32)
    _, _, gate = _row_mods(ml_ref, mc_ref, 0, pl.program_id(1) * tr, tr)
    o_ref[0] = _stream_tile(src_refs, pl.program_id(1), tr) + gate * y


def _mixout_call(src, z, a_mix, b_mix, mods, p, n_ctx_row, t_out, tr):
    src_specs, src_ops = _stream_specs(src, tr)
    bsz = src_ops[0].shape[0]
    const = lambda b, i: (0, 0)
    return pl.pallas_call(
        functools.partial(_mixout_kernel, tr=tr, n_src=len(src_ops)),
        grid=(bsz, t_out // tr),
        in_specs=src_specs + [
                  pl.BlockSpec((1, tr, 2 * GMLP_WIDTH), lambda b, i: (b, i, ZC_GM // (2 * GMLP_WIDTH))),
                  pl.BlockSpec((1, tr, MLSTM_WIDTH), lambda b, i: (b, i, 0)),
                  pl.BlockSpec((1, tr, MLA_WIDTH), lambda b, i: (b, i, 0)),
                  pl.BlockSpec((1, 6, D_MODEL), lambda b, i: (b, 0, 0)),
                  pl.BlockSpec((1, 6, D_MODEL), lambda b, i: (n_ctx_row, 0, 0)),
                  pl.BlockSpec((1, GMLP_WIDTH), const), pl.BlockSpec((1, GMLP_WIDTH), const),
                  pl.BlockSpec((GMLP_GROUPS, CHUNK, CHUNK), lambda b, i: (0, 0, 0)),
                  pl.BlockSpec((CHUNK, GMLP_WIDTH), const),
                  pl.BlockSpec((D_MODEL, D_MODEL), const)],
        out_specs=pl.BlockSpec((1, tr, D_MODEL), lambda b, i: (b, i, 0)),
        out_shape=jax.ShapeDtypeStruct((bsz, t_out, D_MODEL), F32),
        scratch_shapes=[pltpu.VMEM((tr, GMLP_WIDTH), F32)],
        compiler_params=_cparams(("parallel", "parallel")),
        name="mix_out",
    )(*src_ops, z, a_mix, b_mix, mods, mods, p["ln_g"], p["ln_b"], p["w_s"], p["b_s"], p["w_out"])


FFN_SPLIT = 2


def _ffn_kernel(x_ref, ml_ref, mc_ref, g_ref, w1_ref, w3_ref, w2_ref, o_ref, *, tr):
    shift, scale, gate = _row_mods(ml_ref, mc_ref, 3, pl.program_id(1) * tr, tr)
    x = x_ref[0]
    h = _modulated_norm(x, g_ref[...], shift, scale).astype(BF16)
    fc = D_FF // FFN_SPLIT
    y = jnp.zeros((tr, D_MODEL), F32)
    for f in range(FFN_SPLIT):
        cols = slice(f * fc, (f + 1) * fc)
        h1 = jnp.dot(h, w1_ref[:, cols], preferred_element_type=F32)
        h3 = jnp.dot(h, w3_ref[:, cols], preferred_element_type=F32)
        y += jnp.dot((_silu(h1) * h3).astype(BF16), w2_ref[cols, :], preferred_element_type=F32)
    o_ref[0] = x + gate * y


def _ffn_call(x1, mods, norm_g, w1, w3, w2, n_ctx_row):
    bsz = x1.shape[0]
    tr = 768
    const = lambda b, i: (0, 0)
    resident = pl.Buffered(1)
    return pl.pallas_call(
        functools.partial(_ffn_kernel, tr=tr),
        grid=(bsz, T_ALL // tr),
        in_specs=[pl.BlockSpec((1, tr, D_MODEL), lambda b, i: (b, i, 0)),
                  pl.BlockSpec((1, 6, D_MODEL), lambda b, i: (b, 0, 0)),
                  pl.BlockSpec((1, 6, D_MODEL), lambda b, i: (n_ctx_row, 0, 0)),
                  pl.BlockSpec((1, D_MODEL), const),
                  pl.BlockSpec((D_MODEL, D_FF), const, pipeline_mode=resident),
                  pl.BlockSpec((D_MODEL, D_FF), const, pipeline_mode=resident),
                  pl.BlockSpec((D_FF, D_MODEL), const, pipeline_mode=resident)],
        out_specs=pl.BlockSpec((1, tr, D_MODEL), lambda b, i: (b, i, 0)),
        out_shape=jax.ShapeDtypeStruct((bsz, T_ALL, D_MODEL), F32),
        compiler_params=_cparams(("parallel", "parallel")),
        name="dense_ffn",
    )(x1, mods, mods, norm_g, w1, w3, w2)


ROUTE_TR = 1024


def _router_kernel(x_ref, ml_ref, g_ref, rw_ref, rb_ref, h_ref, e_ref, r_ref, gt_ref, cnt_ref, run_ref, *, tr):
    step = pl.program_id(0) * pl.num_programs(1) + pl.program_id(1)

    @pl.when(step == 0)
    def _():
        run_ref[...] = jnp.zeros_like(run_ref)

    h = _modulated_norm(x_ref[0], g_ref[...], ml_ref[0, 3:4, :], ml_ref[0, 4:5, :])
    h_ref[...] = h
    lane = lax.broadcasted_iota(I32, (tr, LANES), 1)
    h_hi = _bf16_head(h)
    h_lo = (h - h_hi).astype(BF16)
    w = rw_ref[...]
    w_hi = _bf16_head(w)
    w_lo = (w - w_hi).astype(BF16)
    first = jnp.dot(h_hi.astype(BF16), jnp.concatenate([w_hi, w_lo.astype(F32)], axis=1).astype(BF16),
                    preferred_element_type=F32)
    logits = (first[:, 0:LANES] + first[:, LANES:2 * LANES]
              + jnp.dot(h_lo, w_hi.astype(BF16), preferred_element_type=F32)) + rb_ref[...]
    logits = jnp.where(lane < N_EXPERTS, logits, -jnp.inf)
    m1 = jnp.max(logits, axis=-1, keepdims=True)
    e1 = jnp.min(jnp.where(logits == m1, lane, LANES), axis=-1, keepdims=True)
    rest = jnp.where(lane == e1, -jnp.inf, logits)
    m2 = jnp.max(rest, axis=-1, keepdims=True)
    e2 = jnp.min(jnp.where(rest == m2, lane, LANES), axis=-1, keepdims=True)
    ex = jnp.exp(m2 - m1)
    g1 = 1.0 / (1.0 + ex)
    g2 = ex / (1.0 + ex)
    onehot = jnp.logical_or(lane == e1, lane == e2 + N_EXPERTS)
    oh = onehot.astype(F32)
    r_i = lax.broadcasted_iota(I32, (tr, tr), 0)
    c_i = lax.broadcasted_iota(I32, (tr, tr), 1)
    before = jnp.dot((c_i < r_i).astype(BF16), oh.astype(BF16), preferred_element_type=F32)
    tot = jnp.sum(oh, axis=0, keepdims=True)
    tot0_shift = pltpu.roll(tot, N_EXPERTS, 1)
    run = run_ref[...]
    first_half = lax.broadcasted_iota(I32, (1, LANES), 1) < N_EXPERTS
    offs = run + jnp.where(first_half, 0.0, tot0_shift)
    ranks = oh * (before + offs)
    rank1 = jnp.sum(jnp.where(lane < N_EXPERTS, ranks, 0.0), axis=-1, keepdims=True)
    rank2 = jnp.sum(jnp.where(lane >= N_EXPERTS, ranks, 0.0), axis=-1, keepdims=True)
    col = lax.broadcasted_iota(I32, (tr, TOP_K), 1)
    e_ref[...] = jnp.where(col == 0, e1, e2)
    r_ref[...] = jnp.where(col == 0, rank1, rank2).astype(I32)
    gt_ref[...] = jnp.where(col == 0, g1, g2)
    both = tot + jnp.where(first_half, pltpu.roll(tot, LANES - N_EXPERTS, 1), tot0_shift)
    new_run = run + both
    run_ref[...] = new_run
    cnt_ref[...] = new_run.astype(I32)


def _router_call(x1, mods, norm_g, rw_p, rb_p):
    bsz = x1.shape[0]
    tr = ROUTE_TR
    n = bsz * SEQ
    nt = SEQ // tr
    const = lambda b, i: (0, 0)
    tok = lambda b, i: (b * nt + i, 0)
    return pl.pallas_call(
        functools.partial(_router_kernel, tr=tr),
        grid=(bsz, nt),
        in_specs=[pl.BlockSpec((1, tr, D_MODEL), lambda b, i: (b, i, 0)),
                  pl.BlockSpec((1, 6, D_MODEL), lambda b, i: (b, 0, 0)),
                  pl.BlockSpec((1, D_MODEL), const),
                  pl.BlockSpec((D_MODEL, LANES), const),
                  pl.BlockSpec((1, LANES), const)],
        out_specs=[pl.BlockSpec((tr, D_MODEL), tok),
                   pl.BlockSpec((tr, TOP_K), tok), pl.BlockSpec((tr, TOP_K), tok), pl.BlockSpec((tr, TOP_K), tok),
                   pl.BlockSpec((1, LANES), const)],
        out_shape=[jax.ShapeDtypeStruct((n, D_MODEL), F32),
                   jax.ShapeDtypeStruct((n, TOP_K), I32), jax.ShapeDtypeStruct((n, TOP_K), I32),
                   jax.ShapeDtypeStruct((n, TOP_K), F32),
                   jax.ShapeDtypeStruct((1, LANES), I32)],
        scratch_shapes=[pltpu.VMEM((1, LANES), F32)],
        compiler_params=_cparams(("arbitrary", "arbitrary")),
        name="moe_router",
    )(x1, mods, norm_g, rw_p, rb_p)


DISPATCH_TD = 512
SUBLANES = 8
ZERO_BURST = MOE_BLOCK + SUBLANES
N_ZERO_BURSTS = 2 * N_EXPERTS
DMA_ISSUE_UNROLL = 8


def _row_copy(src, src_row, dst, dst_row, sem):
    return pltpu.make_async_copy(src.at[pl.ds(src_row, 1)], dst.at[pl.ds(dst_row, 1)], sem)


def _dispatch_kernel(pad_ref, dest_ref, h_ref, xs_out, zero_ref, sem):
    @pl.when(pl.program_id(0) == 0)
    def _():
        zero_ref[...] = jnp.zeros_like(zero_ref)
        for e in range(N_ZERO_BURSTS):
            start = pl.multiple_of(pad_ref[e], SUBLANES)
            burst = pltpu.make_async_copy(zero_ref, xs_out.at[pl.ds(start, ZERO_BURST)], sem)
            burst.start()
            burst.wait()

    def issue(r, carry):
        for kk in range(TOP_K):
            _row_copy(h_ref, r, xs_out, dest_ref[kk, r], sem).start()
        return carry

    lax.fori_loop(0, DISPATCH_TD, issue, 0, unroll=DMA_ISSUE_UNROLL)

    for kk in range(TOP_K):
        pltpu.make_async_copy(h_ref, xs_out.at[pl.ds(0, DISPATCH_TD)], sem).wait()


def _dispatch_call(pad_start, dest_t, h2, cap):
    n = h2.shape[0]
    grid_spec = pltpu.PrefetchScalarGridSpec(
        num_scalar_prefetch=1,
        grid=(n // DISPATCH_TD,),
        in_specs=[pl.BlockSpec((TOP_K, DISPATCH_TD), lambda i, pad: (0, i), memory_space=pltpu.SMEM),
                  pl.BlockSpec((DISPATCH_TD, D_MODEL), lambda i, pad: (i, 0))],
        out_specs=pl.BlockSpec(memory_space=pl.ANY),
        scratch_shapes=[pltpu.VMEM((ZERO_BURST, D_MODEL), F32), pltpu.SemaphoreType.DMA(())],
    )
    return pl.pallas_call(
        _dispatch_kernel,
        grid_spec=grid_spec,
        out_shape=jax.ShapeDtypeStruct((cap, D_MODEL), F32),
        compiler_params=_cparams(("arbitrary",)),
        name="moe_dispatch",
    )(pad_start, dest_t, h2)


EXPERT_FSPLIT = 2


def _expert_kernel(be_ref, na_ref, xs_ref, w1_ref, w3_ref, w2_ref, ys_ref):
    i = pl.program_id(0)
    f = pl.program_id(1)
    active = i < na_ref[0]

    @pl.when(jnp.logical_and(active, f == 0))
    def _():
        ys_ref[...] = jnp.zeros_like(ys_ref)

    @pl.when(active)
    def _():
        x = xs_ref[...].astype(BF16)
        h1 = jnp.dot(x, w1_ref[0], preferred_element_type=F32)
        h3 = jnp.dot(x, w3_ref[0], preferred_element_type=F32)
        ys_ref[...] += jnp.dot((_silu(h1) * h3).astype(BF16), w2_ref[0], preferred_element_type=F32)

    @pl.when(jnp.logical_not(active))
    def _():
        ys_ref[...] = jnp.zeros_like(ys_ref)


def _expert_call(block_expert, n_active, xs, w1, w3, w2):
    cap = xs.shape[0]
    nb = cap // MOE_BLOCK
    fc = D_FF_EXPERT // EXPERT_FSPLIT
    last_f = EXPERT_FSPLIT - 1

    def blk(i, na):
        return jnp.minimum(i, jnp.maximum(na[0] - 1, 0))

    def fidx(i, f, na):
        return jnp.where(i < na[0], f, last_f)

    grid_spec = pltpu.PrefetchScalarGridSpec(
        num_scalar_prefetch=2,
        grid=(nb, EXPERT_FSPLIT),
        in_specs=[pl.BlockSpec((MOE_BLOCK, D_MODEL), lambda i, f, be, na: (blk(i, na), 0)),
                  pl.BlockSpec((1, D_MODEL, fc), lambda i, f, be, na: (be[blk(i, na)], 0, fidx(i, f, na))),
                  pl.BlockSpec((1, D_MODEL, fc), lambda i, f, be, na: (be[blk(i, na)], 0, fidx(i, f, na))),
                  pl.BlockSpec((1, fc, D_MODEL), lambda i, f, be, na: (be[blk(i, na)], fidx(i, f, na), 0))],
        out_specs=pl.BlockSpec((MOE_BLOCK, D_MODEL), lambda i, f, be, na: (i, 0)),
    )
    return pl.pallas_call(
        _expert_kernel,
        grid_spec=grid_spec,
        out_shape=jax.ShapeDtypeStruct((cap, D_MODEL), F32),
        compiler_params=_cparams(("arbitrary", "arbitrary")),
        name="moe_experts",
    )(block_expert, n_active, xs, w1, w3, w2)


COMBINE_TC = 256


def _combine_kernel(dest_ref, dest_next_ref, x_ref, gt_ref, ml_ref, ys_hbm, o_ref, buf_ref, sem):
    i = pl.program_id(0)
    slot = i % 2

    def gather(d_ref, s):
        def issue(r, carry):
            for kk in range(TOP_K):
                _row_copy(ys_hbm, d_ref[kk, r], buf_ref.at[s, kk], r, sem.at[s]).start()
            return carry

        lax.fori_loop(0, COMBINE_TC, issue, 0, unroll=DMA_ISSUE_UNROLL)

    @pl.when(i == 0)
    def _():
        gather(dest_ref, 0)

    @pl.when(i + 1 < pl.num_programs(0))
    def _():
        gather(dest_next_ref, 1 - slot)

    for kk in range(TOP_K):
        pltpu.make_async_copy(ys_hbm.at[pl.ds(0, COMBINE_TC)], buf_ref.at[slot, kk], sem.at[slot]).wait()
    g = gt_ref[...]
    y = buf_ref[slot, 0] * g[:, 0:1] + buf_ref[slot, 1] * g[:, 1:2]
    o_ref[...] = x_ref[...] + ml_ref[0, 5:6, :] * y


def _combine_call(dest_t, x1_flat, gates, mods, ys):
    n = x1_flat.shape[0]
    tc = COMBINE_TC
    per_batch = SEQ // tc
    last = n // tc - 1
    return pl.pallas_call(
        _combine_kernel,
        grid=(n // tc,),
        in_specs=[pl.BlockSpec((TOP_K, tc), lambda i: (0, i), memory_space=pltpu.SMEM),
                  pl.BlockSpec((TOP_K, tc), lambda i: (0, jnp.minimum(i + 1, last)), memory_space=pltpu.SMEM),
                  pl.BlockSpec((tc, D_MODEL), lambda i: (i, 0)),
                  pl.BlockSpec((tc, TOP_K), lambda i: (i, 0)),
                  pl.BlockSpec((1, 6, D_MODEL), lambda i: (i // per_batch, 0, 0)),
                  pl.BlockSpec(memory_space=pl.ANY)],
        out_specs=pl.BlockSpec((tc, D_MODEL), lambda i: (i, 0)),
        out_shape=jax.ShapeDtypeStruct((n, D_MODEL), F32),
        scratch_shapes=[pltpu.VMEM((2, TOP_K, tc, D_MODEL), F32), pltpu.SemaphoreType.DMA((2,))],
        compiler_params=_cparams(("arbitrary",)),
        name="moe_combine",
    )(dest_t, dest_t, x1_flat, gates, mods, ys)


def _moe_layer(x1, mods, norm_g, router_w, router_b, w1, w3, w2):
    bsz = x1.shape[0]
    n = bsz * SEQ
    cap = (n * TOP_K + MOE_BLOCK - 1) // MOE_BLOCK * MOE_BLOCK + N_EXPERTS * MOE_BLOCK
    rw_p = jnp.zeros((D_MODEL, LANES), F32).at[:, :N_EXPERTS].set(router_w)
    rb_p = jnp.zeros((1, LANES), F32).at[0, :N_EXPERTS].set(router_b)
    h2, e_idx, rank, gates, counts = _router_call(x1, mods, norm_g, rw_p, rb_p)
    counts = counts[0, :N_EXPERTS]
    padded = (counts + MOE_BLOCK - 1) // MOE_BLOCK * MOE_BLOCK
    padded_end = jnp.cumsum(padded)
    base = padded_end - padded
    dest_t = (base[e_idx] + rank).T.astype(I32)
    n_blocks = cap // MOE_BLOCK
    block_start = jnp.arange(n_blocks, dtype=I32) * MOE_BLOCK
    block_expert = jnp.minimum(jnp.sum(padded_end[None, :] <= block_start[:, None], axis=1),
                               N_EXPERTS - 1).astype(I32)
    n_active = (padded_end[-1:] // MOE_BLOCK).astype(I32)
    tail = padded_end[-1] + jnp.arange(N_EXPERTS, dtype=I32) * MOE_BLOCK
    pad_start = jnp.concatenate([base + counts, tail])
    pad_start = (jnp.minimum(pad_start, cap - ZERO_BURST) // SUBLANES * SUBLANES).astype(I32)
    xs = _dispatch_call(pad_start, dest_t, h2, cap)
    ys = _expert_call(block_expert, n_active, xs, w1, w3, w2)
    out = _combine_call(dest_t, x1.reshape(n, D_MODEL), gates, mods, ys)
    return out.reshape(bsz, SEQ, D_MODEL)


def _rope_tables():
    rows = SEQ // GRID_W
    row = jnp.repeat(jnp.arange(rows), GRID_W).astype(F32)
    col = jnp.tile(jnp.arange(GRID_W), rows).astype(F32)
    n_freq = MLA_ROPE // 4
    inv = ROPE_BASE ** (-jnp.arange(n_freq, dtype=F32) / n_freq)
    ang = jnp.concatenate([row[:, None] * inv, col[:, None] * inv], axis=-1)
    cos, sin = jnp.cos(ang), jnp.sin(ang)
    ones = jnp.ones((SEQ, MLA_NOPE), F32)
    pad = jnp.zeros((SEQ, LANES - MLA_QK), F32)
    zn = jnp.zeros((SEQ, MLA_NOPE), F32)
    cos_t = jnp.concatenate([ones, cos, cos, pad], axis=-1)
    sin_t = jnp.concatenate([zn, -sin, sin, pad], axis=-1)
    ident = jnp.concatenate([jnp.ones((CTX_LEN, MLA_QK), F32), jnp.zeros((CTX_LEN, LANES - MLA_QK), F32)], axis=-1)
    zero = jnp.zeros((CTX_LEN, LANES), F32)
    return jnp.concatenate([cos_t, ident], axis=0), jnp.concatenate([sin_t, zero], axis=0)


def _rope_copy_pad(a):
    first_half = a[..., MLA_NOPE:MLA_NOPE + ROPE_HALF]
    zeros = jnp.zeros(a.shape[:-1] + (LANES - MLA_QK - ROPE_HALF,), a.dtype)
    return jnp.concatenate([a, first_half, zeros], axis=-1)


def _gate_cols(ga, d):
    h = MLSTM_HEADS
    return jnp.concatenate([ga[..., (2 + d) * h:(3 + d) * h], ga[..., d * h:(d + 1) * h]], axis=-1)


def _relayout_w_in(w_in):
    zeros = lambda n: jnp.zeros((D_MODEL, n), F32)
    w = MLSTM_WIDTH
    ga = w_in[:, OFF_GA:OFF_CQ]
    gate_tile = lambda d: jnp.concatenate([_gate_cols(ga, d), zeros(LANES - GATE_ROWS)], axis=-1)
    main = jnp.concatenate([
        w_in[:, OFF_CQ:OFF_KR],
        _rope_copy_pad(jnp.concatenate([zeros(MLA_NOPE), w_in[:, OFF_KR:OFF_GM]], axis=-1)),
        w_in[:, OFF_GM:IN_COLS],
        w_in[:, 0:w], w_in[:, 2 * w:3 * w],
        w_in[:, 3 * w:4 * w],
        gate_tile(0), gate_tile(1)], axis=-1)
    keys_t = (w_in[:, w:2 * w] * (MLSTM_HEAD_DIM ** -0.5)).T
    gates_t = jnp.concatenate([_gate_cols(ga, 0), _gate_cols(ga, 1)], axis=-1).T
    return main.astype(BF16), keys_t.astype(BF16), gates_t.astype(BF16)


def _relayout_gate_b(gb):
    per_dir = jnp.stack([_gate_cols(gb, 0), _gate_cols(gb, 1)])
    col_form = jnp.pad(per_dir, ((0, 0), (0, LANES - GATE_ROWS)))[:, None, :]
    row_form = jnp.broadcast_to(per_dir[:, :, None], (2, GATE_ROWS, LANES))
    return col_form, row_form


def _layer_params(l, w_in, w_out, mlstm_gate_b, mlstm_norm_g, mla_cq_g, mla_ckv_g, mla_w_uq, mla_w_ukv,
                  mla_q_g, mla_k_g, gmlp_ln_g, gmlp_ln_b, gmlp_w_s, gmlp_b_s):
    pad1 = lambda g: _rope_copy_pad(g)[None, :]
    w_uq = _rope_copy_pad(mla_w_uq[l].reshape(MLA_Q_RANK, MLA_HEADS, MLA_QK)).reshape(MLA_Q_RANK, -1)
    return dict(
        w_in=_relayout_w_in(w_in[l]),
        gate_b=_relayout_gate_b(mlstm_gate_b[l]),
        mlstm_g=mlstm_norm_g[l][None, :],
        cq_g=mla_cq_g[l][None, :], ckv_g=mla_ckv_g[l][None, :],
        w_uq=w_uq.astype(BF16),
        w_kv=mla_w_ukv[l].astype(BF16),
        q_g=pad1(mla_q_g[l]), k_g=pad1(mla_k_g[l]),
        ln_g=gmlp_ln_g[l][None, :], ln_b=gmlp_ln_b[l][None, :],
        w_s=gmlp_w_s[l].astype(BF16),
        b_s=jnp.repeat(gmlp_b_s[l].T, GMLP_GROUP_DIM, axis=1),
        w_out=w_out[l].astype(BF16),
    )


def kernel(x, c, ctx, c_ctx, ada_w, ada_b, norm1_g, norm2_g, w_in, w_out, mlstm_gate_b, mlstm_norm_g, mla_cq_g, mla_ckv_g, mla_w_uq, mla_w_ukv, mla_q_g, mla_k_g, gmlp_ln_g, gmlp_ln_b, gmlp_w_s, gmlp_b_s, ffn_w1, ffn_w3, ffn_w2, moe_router_w, moe_router_b, moe_w1, moe_w3, moe_w2):
    bsz = x.shape[0]
    assert x.shape[1:] == (SEQ, D_MODEL) and ctx.shape[1:] == (CTX_LEN, D_MODEL)
    mod_rows = -(-(bsz + 1) // 8) * 8
    cvec = jnp.zeros((mod_rows, D_MODEL), F32).at[:bsz].set(c).at[bsz].set(c_ctx)
    mods_all = _ada_call(cvec, ada_w.astype(BF16), ada_b[:, None, :]).reshape(DEPTH, mod_rows, 6, D_MODEL)
    rope_tabs = _rope_tables()
    xall = (x, ctx) if DEPTH > 1 else jnp.concatenate([x, ctx], axis=1)
    moe_bf16 = {}
    for l in range(DEPTH):
        last = l == DEPTH - 1
        p = _layer_params(l, w_in, w_out, mlstm_gate_b, mlstm_norm_g, mla_cq_g, mla_ckv_g, mla_w_uq, mla_w_ukv,
                          mla_q_g, mla_k_g, gmlp_ln_g, gmlp_ln_b, gmlp_w_s, gmlp_b_s)
        mods = mods_all[l]
        z, zgc, zkt, zgt = _inproj_call(xall, mods, norm1_g[l][None, :], *p["w_in"], bsz)
        cast_ws = ()
        if (l + 1) % 2 == 1 and l + 1 < DEPTH:
            jn = (l + 1) // 2
            cast_ws = (moe_w1[jn].reshape(-1, D_FF_EXPERT), moe_w3[jn].reshape(-1, D_FF_EXPERT),
                       moe_w2[jn].reshape(-1, D_MODEL))
        a_mix, *cast = _mlstm_call(z, zgc, zkt, zgt, *p["gate_b"], p["mlstm_g"], cast_ws)
        if cast:
            moe_bf16[(l + 1) // 2] = (cast[0].reshape(N_EXPERTS, D_MODEL, D_FF_EXPERT),
                                      cast[1].reshape(N_EXPERTS, D_MODEL, D_FF_EXPERT),
                                      cast[2].reshape(N_EXPERTS, D_FF_EXPERT, D_MODEL))
        q, k, v = _qkv_call(z, p, rope_tabs)
        t_out = SEQ if last else T_ALL
        b_mix = _attn_call(q, k, v, t_out)
        x1 = _mixout_call(xall, z, a_mix, b_mix, mods, p, bsz, t_out, 1024 if last else 768)
        j = l // 2
        if l % 2 == 0:
            assert not last
            xall = _ffn_call(x1, mods, norm2_g[l][None, :], ffn_w1[j].astype(BF16), ffn_w3[j].astype(BF16),
                             ffn_w2[j].astype(BF16), bsz)
        else:
            assert last
            if j not in moe_bf16:
                moe_bf16[j] = (moe_w1[j].astype(BF16), moe_w3[j].astype(BF16), moe_w2[j].astype(BF16))
            xall = _moe_layer(x1, mods, norm2_g[l][None, :], moe_router_w[j], moe_router_b[j], *moe_bf16[j])
    return xall
```

```python
import functools

import jax
import jax.numpy as jnp
from jax import lax
from jax.experimental import pallas as pl
from jax.experimental.pallas import tpu as pltpu

F32 = jnp.float32
BF16 = jnp.bfloat16
I32 = jnp.int32
HIGHEST = lax.Precision.HIGHEST

D_MODEL = 1024
SEQ = 2048
CTX_LEN = 256
T_ALL = SEQ + CTX_LEN
DEPTH = 2
GRID_W = 64
EPS = 1e-6
MLSTM_HEADS = 4
MLSTM_HEAD_DIM = 64
MLSTM_WIDTH = 256
CHUNK = 128
MLA_HEADS = 8
MLA_Q_RANK = 256
MLA_KV_RANK = 128
MLA_NOPE = 64
MLA_ROPE = 32
MLA_V = 64
MLA_QK = 96
MLA_WIDTH = 512
ROPE_BASE = 10000.0
GMLP_GROUPS = 4
GMLP_GROUP_DIM = 64
GMLP_WIDTH = 256
D_FF = 2816
N_EXPERTS = 8
TOP_K = 2
D_FF_EXPERT = 3584
MOE_BLOCK = 512
OFF_GA = 4 * MLSTM_WIDTH
OFF_CQ = OFF_GA + 4 * MLSTM_HEADS
OFF_CKV = OFF_CQ + MLA_Q_RANK
OFF_KR = OFF_CKV + MLA_KV_RANK
OFF_GM = OFF_KR + MLA_ROPE
IN_COLS = OFF_GM + 2 * GMLP_WIDTH

LANES = 128
HEAD_PAD = LANES
ZC_B = 0
ZC_GM = 512
ZC_QV = 1024
ZC_O = 1536
ZC_G = 1792
Z_COLS = ZC_G + 2 * LANES
GATE_ROWS = 8
VMEM_LIMIT = 56 * 1024 * 1024

N_CHUNKS = T_ALL // CHUNK
N_LAT_CHUNKS = SEQ // CHUNK


def _cparams(sem, vmem=VMEM_LIMIT):
    return pltpu.CompilerParams(dimension_semantics=sem, vmem_limit_bytes=vmem)


def _rms(x, g):
    return x * lax.rsqrt(jnp.mean(x * x, axis=-1, keepdims=True) + EPS) * g


def _silu(x):
    return x * jax.nn.sigmoid(x)


def _modulated_norm(x, g, shift, scale):
    return _rms(x, g) * (1.0 + scale) + shift


def _row_mods(ml_ref, mc_ref, first, tile_start, rows):
    row = tile_start + lax.broadcasted_iota(I32, (rows, 1), 0)
    is_ctx = row >= SEQ
    return tuple(jnp.where(is_ctx, mc_ref[0, first + k:first + k + 1, :], ml_ref[0, first + k:first + k + 1, :])
                 for k in range(3))


def _ada_kernel(c_ref, w_ref, b_ref, o_ref):
    s = _silu(c_ref[...]).astype(BF16)
    o_ref[0] = jnp.dot(s, w_ref[0], preferred_element_type=F32) + b_ref[0]


def _ada_call(cvec, ada_w, ada_b):
    rows = cvec.shape[0]
    return pl.pallas_call(
        _ada_kernel,
        grid=(DEPTH, 6),
        in_specs=[pl.BlockSpec((rows, D_MODEL), lambda l, j: (0, 0)),
                  pl.BlockSpec((1, D_MODEL, D_MODEL), lambda l, j: (l, 0, j)),
                  pl.BlockSpec((1, 1, D_MODEL), lambda l, j: (l, 0, j))],
        out_specs=pl.BlockSpec((1, rows, D_MODEL), lambda l, j: (l, 0, j)),
        out_shape=jax.ShapeDtypeStruct((DEPTH, rows, 6 * D_MODEL), F32),
        compiler_params=_cparams(("arbitrary", "arbitrary")),
        name="adaln",
    )(cvec, ada_w, ada_b)


def _stream_specs(src, tr):
    if not isinstance(src, tuple):
        return [pl.BlockSpec((1, tr, D_MODEL), lambda b, i: (b, i, 0))], [src]
    x, ctx = src
    n_full = SEQ // tr
    tail = SEQ - n_full * tr
    assert tail + CTX_LEN == tr and (n_full * tr) % tail == 0
    return ([pl.BlockSpec((1, tr, D_MODEL), lambda b, i: (b, jnp.minimum(i, n_full - 1), 0)),
             pl.BlockSpec((1, tail, D_MODEL), lambda b, i: (b, n_full * tr // tail, 0)),
             pl.BlockSpec((1, CTX_LEN, D_MODEL), lambda b, i: (b, 0, 0))], [x, x, ctx])


def _stream_tile(src_refs, i, tr):
    if len(src_refs) == 1:
        return src_refs[0][0]
    full, tail, ctx = src_refs
    mixed = jnp.concatenate([tail[0], ctx[0]], axis=0)
    return jnp.where(i == SEQ // tr, mixed, full[0])


def _inproj_kernel(*refs, tr, n_src):
    src_refs = refs[:n_src]
    ml_ref, mc_ref, g_ref, w_ref, wk_ref, wg_ref, z_ref, zgc_ref, zk_ref, zg_ref = refs[n_src:]
    shift, scale, _ = _row_mods(ml_ref, mc_ref, 0, pl.program_id(1) * tr, tr)
    x = _stream_tile(src_refs, pl.program_id(1), tr)
    xn = _modulated_norm(x, g_ref[...], shift, scale).astype(BF16)
    z = jnp.dot(xn, w_ref[...], preferred_element_type=F32)
    z_ref[0] = z[:, 0:ZC_G].astype(z_ref.dtype)
    zgc_ref[0] = z[:, ZC_G:Z_COLS]
    nt = (((1,), (1,)), ((), ()))
    zk_ref[0] = lax.dot_general(wk_ref[...], xn, nt, preferred_element_type=F32).astype(zk_ref.dtype)
    zg_ref[0] = lax.dot_general(wg_ref[...], xn, nt, preferred_element_type=F32)


def _inproj_call(src, mods, norm_g, w_main, w_kt, w_gt, n_ctx_row):
    tr = 768
    src_specs, src_ops = _stream_specs(src, tr)
    bsz = src_ops[0].shape[0]
    const = lambda b, i: (0, 0)
    return pl.pallas_call(
        functools.partial(_inproj_kernel, tr=tr, n_src=len(src_ops)),
        grid=(bsz, T_ALL // tr),
        in_specs=src_specs + [
                  pl.BlockSpec((1, 6, D_MODEL), lambda b, i: (b, 0, 0)),
                  pl.BlockSpec((1, 6, D_MODEL), lambda b, i: (n_ctx_row, 0, 0)),
                  pl.BlockSpec((1, D_MODEL), const),
                  pl.BlockSpec((D_MODEL, Z_COLS), const),
                  pl.BlockSpec((MLSTM_WIDTH, D_MODEL), const),
                  pl.BlockSpec((2 * GATE_ROWS, D_MODEL), const)],
        out_specs=[pl.BlockSpec((1, tr, ZC_G), lambda b, i: (b, i, 0)),
                   pl.BlockSpec((1, tr, Z_COLS - ZC_G), lambda b, i: (b, i, 0)),
                   pl.BlockSpec((1, MLSTM_WIDTH, tr), lambda b, i: (b, 0, i)),
                   pl.BlockSpec((1, 2 * GATE_ROWS, tr), lambda b, i: (b, 0, i))],
        out_shape=[jax.ShapeDtypeStruct((bsz, T_ALL, ZC_G), BF16),
                   jax.ShapeDtypeStruct((bsz, T_ALL, Z_COLS - ZC_G), F32),
                   jax.ShapeDtypeStruct((bsz, MLSTM_WIDTH, T_ALL), BF16),
                   jax.ShapeDtypeStruct((bsz, 2 * GATE_ROWS, T_ALL), F32)],
        compiler_params=_cparams(("parallel", "parallel")),
        name="in_proj",
    )(*src_ops, mods, mods, norm_g, w_main, w_kt, w_gt)


def _scan_chunk(d, j):
    fwd_chunk = (j + N_LAT_CHUNKS) % N_CHUNKS
    return jnp.where(d == 0, fwd_chunk, N_CHUNKS - 1 - j)


MLSTM_ROWS_PER_STEP = 8


def _log_sigmoid(x):
    return jnp.minimum(x, 0.0) - jnp.log1p(jnp.exp(-jnp.abs(x)))


def _bf16_head(v):
    bits = lax.bitcast_convert_type(v, jnp.uint32) & jnp.uint32(0xFFFF0000)
    return lax.bitcast_convert_type(bits, F32)


def _bf16_terms(x, axis):
    hi = _bf16_head(x)
    rest = x - hi
    mid = _bf16_head(rest)
    lo = rest - mid
    return jnp.concatenate([hi, mid, lo], axis=axis).astype(BF16)


def _mlstm_kernel(*refs, n_cast):
    ins, rest = refs[:8], refs[8:]
    cast_in, o_ref, cast_out, scratch = rest[:n_cast], rest[n_cast], rest[n_cast + 1:2 * n_cast + 1], rest[2 * n_cast + 1:]
    for src, dst in zip(cast_in, cast_out):
        dst[...] = src[...].astype(dst.dtype)
    _mlstm_step(*ins, o_ref, *scratch)


def _mlstm_step(qv_ref, og_ref, zg_ref, kt_ref, gt_ref, gb_ref, gbt_ref, ng_ref, o_ref, c_ref, m_ref, hf_ref):
    L = CHUNK
    E = MLSTM_HEAD_DIM
    d = pl.program_id(1)
    j = pl.program_id(2)
    fwd = d == 0
    row0 = pl.multiple_of(_scan_chunk(d, j) * L, L)

    @pl.when(j == 0)
    def _():
        c_ref[...] = jnp.zeros_like(c_ref)
        m_ref[...] = jnp.zeros_like(m_ref)

    r_i = lax.broadcasted_iota(I32, (L, L), 0)
    c_i = lax.broadcasted_iota(I32, (L, L), 1)
    prec = jnp.logical_or(jnp.logical_and(fwd, c_i <= r_i), jnp.logical_and(jnp.logical_not(fwd), c_i >= r_i))
    succ = jnp.logical_or(jnp.logical_and(fwd, r_i <= c_i), jnp.logical_and(jnp.logical_not(fwd), r_i >= c_i))
    lane = lax.broadcasted_iota(I32, (L, LANES), 1)
    sub = lax.broadcasted_iota(I32, (LANES, L), 0)
    low_lane = lane < E
    low_sub = sub < E

    row_idx = lax.broadcasted_iota(I32, (L, LANES), 0)
    rows = range(qv_ref.shape[0])
    pairs = range(MLSTM_HEADS // 2)
    h_pairs = {}
    for bb in rows:
        gates_c = zg_ref[bb] + gb_ref[0]
        gates_r = gt_ref[bb] + gbt_ref[0]
        logf_r = _log_sigmoid(gates_r)
        logf_c = jnp.concatenate([logf_r, jnp.zeros((LANES - GATE_ROWS, L), F32)], axis=0).T
        parts_c = jnp.dot(prec.astype(BF16), _bf16_terms(logf_c, 1), preferred_element_type=F32)
        b_c = parts_c[:, 0:LANES] + parts_c[:, LANES:2 * LANES] + parts_c[:, 2 * LANES:3 * LANES]
        c_c = pltpu.roll(gates_c, LANES - MLSTM_HEADS, 1) - b_c
        m_row = m_ref[bb]
        cm = c_c
        for sh in (1, 2, 4, 8, 16, 32, 64):
            prev = jnp.where(fwd, jnp.where(row_idx >= sh, pltpu.roll(cm, sh, 0), -jnp.inf),
                             jnp.where(row_idx < L - sh, pltpu.roll(cm, L - sh, 0), -jnp.inf))
            cm = jnp.maximum(cm, prev)
        m_c = jnp.maximum(cm, m_row)
        a_c = jnp.exp(m_row - m_c)
        floor_c = jnp.exp(-(b_c + m_c))
        m_last = jnp.maximum(jnp.max(c_c, axis=0, keepdims=True), m_row)
        b_last = jnp.where(fwd, b_c[L - 1:L, :], b_c[0:1, :])
        m_ref[bb] = b_last + m_last
        decay_row = jnp.exp(m_row - m_last)
        parts_r = jnp.dot(_bf16_terms(logf_r, 0), succ.astype(BF16), preferred_element_type=F32)
        b_r = parts_r[0:GATE_ROWS] + parts_r[GATE_ROWS:2 * GATE_ROWS] + parts_r[2 * GATE_ROWS:3 * GATE_ROWS]
        c_r = gates_r[MLSTM_HEADS:2 * MLSTM_HEADS, :] - b_r[0:MLSTM_HEADS, :]
        w_r = [jnp.exp(c_r[h:h + 1, :] - m_last[:, h:h + 1]) for h in range(MLSTM_HEADS)]

        for pj in pairs:
            tile = slice(pj * LANES, (pj + 1) * LANES)
            q_t = qv_ref[bb, :, tile]
            v_t = qv_ref[bb, :, MLSTM_WIDTH + pj * LANES:MLSTM_WIDTH + (pj + 1) * LANES]
            kt_t = kt_ref[bb, tile, :]
            c_old = c_ref[bb, pj]
            q_both = jnp.concatenate([jnp.where(low_lane, q_t, 0.0), jnp.where(low_lane, 0.0, q_t)],
                                     axis=0).astype(BF16)
            kc = jnp.concatenate([kt_t.astype(BF16), c_old.astype(BF16)], axis=1)
            qk_qc = jnp.dot(q_both, kc, preferred_element_type=F32)
            hx = []
            upd = []
            for par in range(2):
                h = 2 * pj + par
                own_lane = low_lane if par == 0 else jnp.logical_not(low_lane)
                own_sub = low_sub if par == 0 else jnp.logical_not(low_sub)
                den_lane = E if par == 0 else 0
                v_ext = jnp.where(own_lane, v_t, jnp.where(lane == den_lane, 1.0, 0.0)).astype(BF16)
                sqk = qk_qc[par * L:(par + 1) * L, 0:L]
                q_c = qk_qc[par * L:(par + 1) * L, L:L + LANES]
                dmat = jnp.where(prec, jnp.exp(c_r[h:h + 1, :] - m_c[:, h:h + 1]), 0.0)
                kw_t = jnp.where(own_sub, kt_t, 0.0) * w_r[h]
                pv_upd = jnp.dot(jnp.concatenate([sqk * dmat, kw_t], axis=0).astype(BF16), v_ext,
                                 preferred_element_type=F32)
                nd = pv_upd[0:L] + a_c[:, h:h + 1] * q_c
                den = nd[:, den_lane:den_lane + 1]
                hx.append(nd / jnp.maximum(jnp.abs(den), floor_c[:, h:h + 1]))
                upd.append(pv_upd[L:L + LANES])
            decay = jnp.where(low_sub, decay_row[:, 2 * pj:2 * pj + 1], decay_row[:, 2 * pj + 1:2 * pj + 2])
            c_ref[bb, pj] = decay * c_old + upd[0] + upd[1]
            h_pairs[bb, pj] = jnp.where(low_lane, hx[0], hx[1])

    @pl.when(fwd)
    def _():
        for (bb, pj), h_pair in h_pairs.items():
            hf_ref[bb, pl.ds(row0, L), pj * LANES:(pj + 1) * LANES] = h_pair

    @pl.when(jnp.logical_not(fwd))
    def _():
        for (bb, pj), h_pair in h_pairs.items():
            tile = slice(pj * LANES, (pj + 1) * LANES)
            hs = hf_ref[bb, pl.ds(row0, L), tile] + h_pair
            sq = hs * hs
            ss = jnp.where(low_lane,
                           jnp.sum(jnp.where(low_lane, sq, 0.0), axis=-1, keepdims=True),
                           jnp.sum(jnp.where(low_lane, 0.0, sq), axis=-1, keepdims=True))
            y = hs * lax.rsqrt(ss * (1.0 / E) + EPS) * ng_ref[:, tile]
            o_ref[bb, :, tile] = (jax.nn.sigmoid(og_ref[bb, :, tile].astype(F32)) * y).astype(o_ref.dtype)


def _mlstm_call(z, zgc, zkt, zgt, gate_b, gate_bt, norm_g, cast_ws=()):
    bsz = z.shape[0]
    nb = max(r for r in range(1, MLSTM_ROWS_PER_STEP + 1) if bsz % r == 0)
    last = N_CHUNKS - 1
    chunk = lambda b, d, j: _scan_chunk(d, j)
    steps = (bsz // nb) * 2 * N_CHUNKS
    n_blk = 1 << (steps.bit_length() - 1)
    cast_idx = lambda b, d, j: (jnp.minimum((b * 2 + d) * N_CHUNKS + j, n_blk - 1), 0)
    cast_specs = [pl.BlockSpec((w.shape[0] // n_blk, w.shape[1]), cast_idx) for w in cast_ws]
    assert all(w.shape[0] % (16 * n_blk) == 0 for w in cast_ws)
    return pl.pallas_call(
        functools.partial(_mlstm_kernel, n_cast=len(cast_ws)),
        grid=(bsz // nb, 2, N_CHUNKS),
        in_specs=[pl.BlockSpec((nb, CHUNK, 2 * MLSTM_WIDTH), lambda b, d, j: (b, chunk(b, d, j), ZC_QV // 512)),
                  pl.BlockSpec((nb, CHUNK, MLSTM_WIDTH), lambda b, d, j: (b, chunk(b, d, j), ZC_O // MLSTM_WIDTH)),
                  pl.BlockSpec((nb, CHUNK, LANES), lambda b, d, j: (b, chunk(b, d, j), d)),
                  pl.BlockSpec((nb, MLSTM_WIDTH, CHUNK), lambda b, d, j: (b, 0, chunk(b, d, j))),
                  pl.BlockSpec((nb, GATE_ROWS, CHUNK), lambda b, d, j: (b, d, chunk(b, d, j))),
                  pl.BlockSpec((1, 1, LANES), lambda b, d, j: (d, 0, 0)),
                  pl.BlockSpec((1, GATE_ROWS, LANES), lambda b, d, j: (d, 0, 0)),
                  pl.BlockSpec((1, MLSTM_WIDTH), lambda b, d, j: (0, 0))] + cast_specs,
        out_specs=[pl.BlockSpec((nb, CHUNK, MLSTM_WIDTH),
                                lambda b, d, j: (b, jnp.where(d == 0, last, last - j), 0))] + cast_specs,
        out_shape=[jax.ShapeDtypeStruct((bsz, T_ALL, MLSTM_WIDTH), BF16)]
        + [jax.ShapeDtypeStruct(w.shape, BF16) for w in cast_ws],
        scratch_shapes=[pltpu.VMEM((nb, MLSTM_HEADS // 2, LANES, LANES), F32),
                        pltpu.VMEM((nb, 1, LANES), F32),
                        pltpu.VMEM((nb, T_ALL, MLSTM_WIDTH), F32)],
        compiler_params=_cparams(("arbitrary", "arbitrary", "arbitrary")),
        name="mlstm",
    )(z, z, zgc, zkt, zgt, gate_b, gate_bt, norm_g, *cast_ws)


ROPE_HALF = MLA_ROPE // 2


def _rope(t, cos_t, sin_t):
    return t * cos_t + pltpu.roll(t, LANES - ROPE_HALF, 1) * sin_t


HEAD_PAIR = 2 * HEAD_PAD
ATT_DEN_LANE = 0
assert MLA_NOPE + MLA_V == HEAD_PAD


def _head_rms_scale(raw, ones_ref):
    sq = (raw * raw).astype(BF16)
    ss = jnp.concatenate([jnp.dot(sq[:, j * HEAD_PAIR:(j + 1) * HEAD_PAIR], ones_ref[...],
                                  preferred_element_type=F32) for j in range(MLA_HEADS // 2)], axis=-1)
    return lax.rsqrt(ss * (1.0 / MLA_QK) + EPS)


def _qkv_kernel(z_ref, cqg_ref, ckvg_ref, wq_ref, wkv_ref, ones_ref, qc_ref, qs_ref, kc_ref, ks_ref,
                q_ref, k_ref, v_ref, *, tr):
    cqn = _rms(z_ref[0, :, 0:MLA_Q_RANK].astype(F32), cqg_ref[...]).astype(BF16)
    ckvn = _rms(z_ref[0, :, MLA_Q_RANK:MLA_Q_RANK + MLA_KV_RANK].astype(F32), ckvg_ref[...]).astype(BF16)
    k_rope = z_ref[0, :, MLA_Q_RANK + MLA_KV_RANK:MLA_Q_RANK + MLA_KV_RANK + LANES].astype(F32)
    q_raw = jnp.dot(cqn, wq_ref[...], preferred_element_type=F32)
    kv_raw = jnp.dot(ckvn, wkv_ref[...], preferred_element_type=F32)
    lane_all = lax.broadcasted_iota(I32, (tr, MLA_HEADS * HEAD_PAD), 1) % HEAD_PAD
    k_raw = jnp.where(lane_all < MLA_NOPE, kv_raw, jnp.concatenate([k_rope] * MLA_HEADS, axis=-1))
    q_n = q_raw * _head_rms_scale(q_raw, ones_ref)
    k_n = k_raw * _head_rms_scale(k_raw, ones_ref)
    q_cos, q_sin, k_cos, k_sin = qc_ref[...], qs_ref[...], kc_ref[...], ks_ref[...]
    lane = lax.broadcasted_iota(I32, (tr, LANES), 1)
    for h in range(MLA_HEADS):
        sl = slice(h * HEAD_PAD, (h + 1) * HEAD_PAD)
        q_ref[0, :, sl] = _rope(q_n[:, sl], q_cos, q_sin).astype(BF16)
        k_ref[0, :, sl] = _rope(k_n[:, sl], k_cos, k_sin).astype(BF16)
        v_ref[0, :, sl] = jnp.where(lane >= MLA_NOPE, kv_raw[:, sl],
                                    jnp.where(lane == ATT_DEN_LANE, 1.0, 0.0)).astype(BF16)


def _qkv_call(z, p, rope_tabs):
    bsz = z.shape[0]
    tr = 768
    hw = MLA_HEADS * HEAD_PAD
    const = lambda b, i: (0, 0)
    out = jax.ShapeDtypeStruct((bsz, T_ALL, hw), BF16)
    tab = pl.BlockSpec((tr, LANES), lambda b, i: (i, 0))
    ospec = pl.BlockSpec((1, tr, hw), lambda b, i: (b, i, 0))
    return pl.pallas_call(
        functools.partial(_qkv_kernel, tr=tr),
        grid=(bsz, T_ALL // tr),
        in_specs=[pl.BlockSpec((1, tr, 512), lambda b, i: (b, i, ZC_B // 512)),
                  pl.BlockSpec((1, MLA_Q_RANK), const), pl.BlockSpec((1, MLA_KV_RANK), const),
                  pl.BlockSpec((MLA_Q_RANK, hw), const), pl.BlockSpec((MLA_KV_RANK, hw), const),
                  pl.BlockSpec((HEAD_PAIR, HEAD_PAIR), const),
                  tab, tab, tab, tab],
        out_specs=[ospec, ospec, ospec],
        out_shape=[out, out, out],
        compiler_params=_cparams(("parallel", "parallel")),
        name="mla_qkv",
    )(z, p["cq_g"], p["ckv_g"], p["w_uq"], p["w_kv"], _head_ones(),
      *_gained_rope_tables(rope_tabs, p["q_g"], MLA_QK ** -0.5), *_gained_rope_tables(rope_tabs, p["k_g"], 1.0))


def _gained_rope_tables(rope_tabs, gain, scale):
    cos_t, sin_t = rope_tabs
    partner_gain = jnp.roll(gain, -ROPE_HALF, axis=1)
    return cos_t * (gain * scale), sin_t * (partner_gain * scale)


def _head_ones():
    r = jnp.arange(HEAD_PAIR)
    same_head = (r[:, None] // HEAD_PAD) == (r[None, :] // HEAD_PAD)
    real_row = (r[:, None] % HEAD_PAD) < MLA_QK
    return jnp.logical_and(same_head, real_row).astype(BF16)


ATT_TQ = 1024
assert SEQ % ATT_TQ == 0 and CTX_LEN <= ATT_TQ


def _attn_kernel(q_ref, k_ref, v_ref, o_ref):
    i = pl.program_id(1)

    def run(rows, k0):
        def scores(hh):
            sl = slice(hh * HEAD_PAD, (hh + 1) * HEAD_PAD)
            return lax.dot_general(q_ref[0, 0:rows, sl], k_ref[0, k0:T_ALL, sl], (((1,), (1,)), ((), ())),
                                   preferred_element_type=F32)

        s = scores(0)
        for hh in range(MLA_HEADS):
            s_next = scores(hh + 1) if hh + 1 < MLA_HEADS else None
            p = jnp.exp(s - jnp.max(s, axis=-1, keepdims=True)).astype(BF16)
            oe = jnp.dot(p, v_ref[0, k0:T_ALL, hh * HEAD_PAD:(hh + 1) * HEAD_PAD], preferred_element_type=F32)
            o_ref[0, 0:rows, hh * MLA_V:(hh + 1) * MLA_V] = (
                oe[:, MLA_NOPE:MLA_NOPE + MLA_V] / oe[:, ATT_DEN_LANE:ATT_DEN_LANE + 1]).astype(o_ref.dtype)
            s = s_next

    @pl.when(i < SEQ // ATT_TQ)
    def _():
        run(ATT_TQ, 0)

    @pl.when(i >= SEQ // ATT_TQ)
    def _():
        run(CTX_LEN, SEQ)
        o_ref[0, CTX_LEN:ATT_TQ, :] = jnp.zeros((ATT_TQ - CTX_LEN, MLA_WIDTH), o_ref.dtype)


def _attn_call(q, k, v, t_out):
    bsz = q.shape[0]
    w = MLA_HEADS * HEAD_PAD
    return pl.pallas_call(
        _attn_kernel,
        grid=(bsz, pl.cdiv(t_out, ATT_TQ)),
        in_specs=[pl.BlockSpec((1, ATT_TQ, w), lambda b, i: (b, i, 0)),
                  pl.BlockSpec((1, T_ALL, w), lambda b, i: (b, 0, 0)),
                  pl.BlockSpec((1, T_ALL, w), lambda b, i: (b, 0, 0))],
        out_specs=pl.BlockSpec((1, ATT_TQ, MLA_WIDTH), lambda b, i: (b, i, 0)),
        out_shape=jax.ShapeDtypeStruct((bsz, t_out, MLA_WIDTH), BF16),
        compiler_params=_cparams(("parallel", "arbitrary")),
        name="mla_attn",
    )(q, k, v)


def _mixout_kernel(*refs, tr, n_src):
    src_refs = refs[:n_src]
    zg_ref, a_ref, b_ref, ml_ref, mc_ref, lng_ref, lnb_ref, ws_ref, bs_ref, wo_ref, o_ref, cm_ref = refs[n_src:]
    gd = GMLP_GROUP_DIM
    act = jax.nn.gelu(zg_ref[0].astype(F32))
    u = act[:, 0:GMLP_WIDTH]
    vv = act[:, GMLP_WIDTH:2 * GMLP_WIDTH]
    mu = jnp.mean(vv, axis=-1, keepdims=True)
    var = jnp.mean(jnp.square(vv - mu), axis=-1, keepdims=True)
    vn = ((vv - mu) * lax.rsqrt(var + EPS) * lng_ref[...] + lnb_ref[...]).astype(BF16)
    col_group = lax.broadcasted_iota(I32, (CHUNK, GMLP_WIDTH), 1) // gd
    for c in range(tr // CHUNK):
        rows = slice(c * CHUNK, (c + 1) * CHUNK)
        mixed = jnp.dot(ws_ref[...], vn[rows, :], preferred_element_type=F32)
        sv = mixed[0:CHUNK]
        for g in range(1, GMLP_GROUPS):
            sv = jnp.where(col_group == g, mixed[g * CHUNK:(g + 1) * CHUNK], sv)
        cm_ref[rows, :] = u[rows, :] * (sv + bs_ref[...])
    y = jnp.dot(a_ref[0].astype(BF16), wo_ref[0:MLSTM_WIDTH, :], preferred_element_type=F32)
    y += jnp.dot(b_ref[0].astype(BF16), wo_ref[MLSTM_WIDTH:MLSTM_WIDTH + MLA_WIDTH, :],
                 preferred_element_type=F32)
    y += jnp.dot(cm_ref[...].astype(BF16), wo_ref[MLSTM_WIDTH + MLA_WIDTH:, :], preferred_element_type=F32)
    _, _, gate = _row_mods(ml_ref, mc_ref, 0, pl.program_id(1) * tr, tr)
    o_ref[0] = _stream_tile(src_refs, pl.program_id(1), tr) + gate * y


def _mixout_call(src, z, a_mix, b_mix, mods, p, n_ctx_row, t_out, tr):
    src_specs, src_ops = _stream_specs(src, tr)
    bsz = src_ops[0].shape[0]
    const = lambda b, i: (0, 0)
    return pl.pallas_call(
        functools.partial(_mixout_kernel, tr=tr, n_src=len(src_ops)),
        grid=(bsz, t_out // tr),
        in_specs=src_specs + [
                  pl.BlockSpec((1, tr, 2 * GMLP_WIDTH), lambda b, i: (b, i, ZC_GM // (2 * GMLP_WIDTH))),
                  pl.BlockSpec((1, tr, MLSTM_WIDTH), lambda b, i: (b, i, 0)),
                  pl.BlockSpec((1, tr, MLA_WIDTH), lambda b, i: (b, i, 0)),
                  pl.BlockSpec((1, 6, D_MODEL), lambda b, i: (b, 0, 0)),
                  pl.BlockSpec((1, 6, D_MODEL), lambda b, i: (n_ctx_row, 0, 0)),
                  pl.BlockSpec((1, GMLP_WIDTH), const), pl.BlockSpec((1, GMLP_WIDTH), const),
                  pl.BlockSpec((GMLP_GROUPS * CHUNK, CHUNK), const),
                  pl.BlockSpec((CHUNK, GMLP_WIDTH), const),
                  pl.BlockSpec((D_MODEL, D_MODEL), const)],
        out_specs=pl.BlockSpec((1, tr, D_MODEL), lambda b, i: (b, i, 0)),
        out_shape=jax.ShapeDtypeStruct((bsz, t_out, D_MODEL), F32),
        scratch_shapes=[pltpu.VMEM((tr, GMLP_WIDTH), F32)],
        compiler_params=_cparams(("parallel", "parallel")),
        name="mix_out",
    )(*src_ops, z, a_mix, b_mix, mods, mods, p["ln_g"], p["ln_b"], p["w_s"], p["b_s"], p["w_out"])


FFN_SPLIT = 2


def _ffn_kernel(x_ref, ml_ref, mc_ref, g_ref, w1_ref, w3_ref, w2_ref, o_ref, *, tr):
    shift, scale, gate = _row_mods(ml_ref, mc_ref, 3, pl.program_id(1) * tr, tr)
    x = x_ref[0]
    h = _modulated_norm(x, g_ref[...], shift, scale).astype(BF16)
    fc = D_FF // FFN_SPLIT
    y = jnp.zeros((tr, D_MODEL), F32)
    for f in range(FFN_SPLIT):
        cols = slice(f * fc, (f + 1) * fc)
        h1 = jnp.dot(h, w1_ref[:, cols], preferred_element_type=F32)
        h3 = jnp.dot(h, w3_ref[:, cols], preferred_element_type=F32)
        y += jnp.dot((_silu(h1) * h3).astype(BF16), w2_ref[cols, :], preferred_element_type=F32)
    o_ref[0] = x + gate * y


def _ffn_call(x1, mods, norm_g, w1, w3, w2, n_ctx_row):
    bsz = x1.shape[0]
    tr = 768
    const = lambda b, i: (0, 0)
    resident = pl.Buffered(1)
    return pl.pallas_call(
        functools.partial(_ffn_kernel, tr=tr),
        grid=(bsz, T_ALL // tr),
        in_specs=[pl.BlockSpec((1, tr, D_MODEL), lambda b, i: (b, i, 0)),
                  pl.BlockSpec((1, 6, D_MODEL), lambda b, i: (b, 0, 0)),
                  pl.BlockSpec((1, 6, D_MODEL), lambda b, i: (n_ctx_row, 0, 0)),
                  pl.BlockSpec((1, D_MODEL), const),
                  pl.BlockSpec((D_MODEL, D_FF), const, pipeline_mode=resident),
                  pl.BlockSpec((D_MODEL, D_FF), const, pipeline_mode=resident),
                  pl.BlockSpec((D_FF, D_MODEL), const, pipeline_mode=resident)],
        out_specs=pl.BlockSpec((1, tr, D_MODEL), lambda b, i: (b, i, 0)),
        out_shape=jax.ShapeDtypeStruct((bsz, T_ALL, D_MODEL), F32),
        compiler_params=_cparams(("parallel", "parallel")),
        name="dense_ffn",
    )(x1, mods, mods, norm_g, w1, w3, w2)


ROUTE_TR = 1024


def _router_kernel(x_ref, ml_ref, g_ref, rw_ref, rb_ref, h_ref, e_ref, r_ref, gt_ref, cnt_ref, run_ref, *, tr):
    step = pl.program_id(0) * pl.num_programs(1) + pl.program_id(1)

    @pl.when(step == 0)
    def _():
        run_ref[...] = jnp.zeros_like(run_ref)

    h = _modulated_norm(x_ref[0], g_ref[...], ml_ref[0, 3:4, :], ml_ref[0, 4:5, :])
    h_ref[...] = h
    lane = lax.broadcasted_iota(I32, (tr, LANES), 1)
    h_hi = _bf16_head(h)
    h_lo = (h - h_hi).astype(BF16)
    w = rw_ref[...]
    w_hi = _bf16_head(w)
    w_lo = (w - w_hi).astype(BF16)
    first = jnp.dot(h_hi.astype(BF16), jnp.concatenate([w_hi, w_lo.astype(F32)], axis=1).astype(BF16),
                    preferred_element_type=F32)
    logits = (first[:, 0:LANES] + first[:, LANES:2 * LANES]
              + jnp.dot(h_lo, w_hi.astype(BF16), preferred_element_type=F32)) + rb_ref[...]
    logits = jnp.where(lane < N_EXPERTS, logits, -jnp.inf)
    m1 = jnp.max(logits, axis=-1, keepdims=True)
    e1 = jnp.min(jnp.where(logits == m1, lane, LANES), axis=-1, keepdims=True)
    rest = jnp.where(lane == e1, -jnp.inf, logits)
    m2 = jnp.max(rest, axis=-1, keepdims=True)
    e2 = jnp.min(jnp.where(rest == m2, lane, LANES), axis=-1, keepdims=True)
    ex = jnp.exp(m2 - m1)
    g1 = 1.0 / (1.0 + ex)
    g2 = ex / (1.0 + ex)
    onehot = jnp.logical_or(lane == e1, lane == e2 + N_EXPERTS)
    oh = onehot.astype(F32)
    r_i = lax.broadcasted_iota(I32, (tr, tr), 0)
    c_i = lax.broadcasted_iota(I32, (tr, tr), 1)
    before = jnp.dot((c_i < r_i).astype(BF16), oh.astype(BF16), preferred_element_type=F32)
    tot = jnp.sum(oh, axis=0, keepdims=True)
    tot0_shift = pltpu.roll(tot, N_EXPERTS, 1)
    run = run_ref[...]
    first_half = lax.broadcasted_iota(I32, (1, LANES), 1) < N_EXPERTS
    offs = run + jnp.where(first_half, 0.0, tot0_shift)
    ranks = oh * (before + offs)
    rank1 = jnp.sum(jnp.where(lane < N_EXPERTS, ranks, 0.0), axis=-1, keepdims=True)
    rank2 = jnp.sum(jnp.where(lane >= N_EXPERTS, ranks, 0.0), axis=-1, keepdims=True)
    col = lax.broadcasted_iota(I32, (tr, TOP_K), 1)
    e_ref[...] = jnp.where(col == 0, e1, e2)
    r_ref[...] = jnp.where(col == 0, rank1, rank2).astype(I32)
    gt_ref[...] = jnp.where(col == 0, g1, g2)
    both = tot + jnp.where(first_half, pltpu.roll(tot, LANES - N_EXPERTS, 1), tot0_shift)
    new_run = run + both
    run_ref[...] = new_run
    cnt_ref[...] = new_run.astype(I32)


def _router_call(x1, mods, norm_g, rw_p, rb_p):
    bsz = x1.shape[0]
    tr = ROUTE_TR
    n = bsz * SEQ
    nt = SEQ // tr
    const = lambda b, i: (0, 0)
    tok = lambda b, i: (b * nt + i, 0)
    return pl.pallas_call(
        functools.partial(_router_kernel, tr=tr),
        grid=(bsz, nt),
        in_specs=[pl.BlockSpec((1, tr, D_MODEL), lambda b, i: (b, i, 0)),
                  pl.BlockSpec((1, 6, D_MODEL), lambda b, i: (b, 0, 0)),
                  pl.BlockSpec((1, D_MODEL), const),
                  pl.BlockSpec((D_MODEL, LANES), const),
                  pl.BlockSpec((1, LANES), const)],
        out_specs=[pl.BlockSpec((tr, D_MODEL), tok),
                   pl.BlockSpec((tr, TOP_K), tok), pl.BlockSpec((tr, TOP_K), tok), pl.BlockSpec((tr, TOP_K), tok),
                   pl.BlockSpec((1, LANES), const)],
        out_shape=[jax.ShapeDtypeStruct((n, D_MODEL), F32),
                   jax.ShapeDtypeStruct((n, TOP_K), I32), jax.ShapeDtypeStruct((n, TOP_K), I32),
                   jax.ShapeDtypeStruct((n, TOP_K), F32),
                   jax.ShapeDtypeStruct((1, LANES), I32)],
        scratch_shapes=[pltpu.VMEM((1, LANES), F32)],
        compiler_params=_cparams(("arbitrary", "arbitrary")),
        name="moe_router",
    )(x1, mods, norm_g, rw_p, rb_p)


DISPATCH_TD = 512
SUBLANES = 8
ZERO_BURST = MOE_BLOCK + SUBLANES
N_ZERO_BURSTS = 2 * N_EXPERTS
DMA_ISSUE_UNROLL = 8


def _row_copy(src, src_row, dst, dst_row, sem):
    return pltpu.make_async_copy(src.at[pl.ds(src_row, 1)], dst.at[pl.ds(dst_row, 1)], sem)


def _dispatch_kernel(pad_ref, dest_ref, h_ref, xs_out, zero_ref, sem):
    @pl.when(pl.program_id(0) == 0)
    def _():
        zero_ref[...] = jnp.zeros_like(zero_ref)
        for e in range(N_ZERO_BURSTS):
            start = pl.multiple_of(pad_ref[e], SUBLANES)
            burst = pltpu.make_async_copy(zero_ref, xs_out.at[pl.ds(start, ZERO_BURST)], sem)
            burst.start()
            burst.wait()

    def issue(r, carry):
        for kk in range(TOP_K):
            _row_copy(h_ref, r, xs_out, dest_ref[kk, r], sem).start(priority=kk)
        return carry

    lax.fori_loop(0, DISPATCH_TD, issue, 0, unroll=DMA_ISSUE_UNROLL)

    for kk in range(TOP_K):
        pltpu.make_async_copy(h_ref, xs_out.at[pl.ds(0, DISPATCH_TD)], sem).wait()


def _dispatch_call(pad_start, dest_t, h2, cap):
    n = h2.shape[0]
    grid_spec = pltpu.PrefetchScalarGridSpec(
        num_scalar_prefetch=1,
        grid=(n // DISPATCH_TD,),
        in_specs=[pl.BlockSpec((TOP_K, DISPATCH_TD), lambda i, pad: (0, i), memory_space=pltpu.SMEM),
                  pl.BlockSpec((DISPATCH_TD, D_MODEL), lambda i, pad: (i, 0))],
        out_specs=pl.BlockSpec(memory_space=pl.ANY),
        scratch_shapes=[pltpu.VMEM((ZERO_BURST, D_MODEL), F32), pltpu.SemaphoreType.DMA(())],
    )
    return pl.pallas_call(
        _dispatch_kernel,
        grid_spec=grid_spec,
        out_shape=jax.ShapeDtypeStruct((cap, D_MODEL), F32),
        compiler_params=_cparams(("arbitrary",)),
        name="moe_dispatch",
    )(pad_start, dest_t, h2)


EXPERT_FSPLIT = 2


def _expert_kernel(be_ref, na_ref, xs_ref, w1_ref, w3_ref, w2_ref, ys_ref):
    i = pl.program_id(0)
    f = pl.program_id(1)
    active = i < na_ref[0]

    @pl.when(jnp.logical_and(active, f == 0))
    def _():
        ys_ref[...] = jnp.zeros_like(ys_ref)

    @pl.when(active)
    def _():
        x = xs_ref[...].astype(BF16)
        h1 = jnp.dot(x, w1_ref[0], preferred_element_type=F32)
        h3 = jnp.dot(x, w3_ref[0], preferred_element_type=F32)
        ys_ref[...] += jnp.dot((_silu(h1) * h3).astype(BF16), w2_ref[0], preferred_element_type=F32)

    @pl.when(jnp.logical_not(active))
    def _():
        ys_ref[...] = jnp.zeros_like(ys_ref)


def _expert_call(block_expert, n_active, xs, w1, w3, w2):
    cap = xs.shape[0]
    nb = cap // MOE_BLOCK
    fc = D_FF_EXPERT // EXPERT_FSPLIT
    last_f = EXPERT_FSPLIT - 1

    def blk(i, na):
        return jnp.minimum(i, jnp.maximum(na[0] - 1, 0))

    def fidx(i, f, na):
        return jnp.where(i < na[0], f, last_f)

    grid_spec = pltpu.PrefetchScalarGridSpec(
        num_scalar_prefetch=2,
        grid=(nb, EXPERT_FSPLIT),
        in_specs=[pl.BlockSpec((MOE_BLOCK, D_MODEL), lambda i, f, be, na: (blk(i, na), 0)),
                  pl.BlockSpec((1, D_MODEL, fc), lambda i, f, be, na: (be[blk(i, na)], 0, fidx(i, f, na))),
                  pl.BlockSpec((1, D_MODEL, fc), lambda i, f, be, na: (be[blk(i, na)], 0, fidx(i, f, na))),
                  pl.BlockSpec((1, fc, D_MODEL), lambda i, f, be, na: (be[blk(i, na)], fidx(i, f, na), 0))],
        out_specs=pl.BlockSpec((MOE_BLOCK, D_MODEL), lambda i, f, be, na: (i, 0)),
    )
    return pl.pallas_call(
        _expert_kernel,
        grid_spec=grid_spec,
        out_shape=jax.ShapeDtypeStruct((cap, D_MODEL), F32),
        compiler_params=_cparams(("arbitrary", "arbitrary")),
        name="moe_experts",
    )(block_expert, n_active, xs, w1, w3, w2)


COMBINE_TC = 256


def _combine_kernel(dest_ref, dest_next_ref, x_ref, gt_ref, ml_ref, ys_hbm, o_ref, buf_ref, sem):
    i = pl.program_id(0)
    slot = i % 2

    def gather(d_ref, s):
        def issue(r, carry):
            for kk in range(TOP_K):
                _row_copy(ys_hbm, d_ref[kk, r], buf_ref.at[s, kk], r, sem.at[s]).start(priority=kk)
            return carry

        lax.fori_loop(0, COMBINE_TC, issue, 0, unroll=DMA_ISSUE_UNROLL)

    @pl.when(i == 0)
    def _():
        gather(dest_ref, 0)

    @pl.when(i + 1 < pl.num_programs(0))
    def _():
        gather(dest_next_ref, 1 - slot)

    for kk in range(TOP_K):
        pltpu.make_async_copy(ys_hbm.at[pl.ds(0, COMBINE_TC)], buf_ref.at[slot, kk], sem.at[slot]).wait()
    g = gt_ref[...]
    y = buf_ref[slot, 0] * g[:, 0:1] + buf_ref[slot, 1] * g[:, 1:2]
    o_ref[...] = x_ref[...] + ml_ref[0, 5:6, :] * y


def _combine_call(dest_t, x1_flat, gates, mods, ys):
    n = x1_flat.shape[0]
    tc = COMBINE_TC
    per_batch = SEQ // tc
    last = n // tc - 1
    return pl.pallas_call(
        _combine_kernel,
        grid=(n // tc,),
        in_specs=[pl.BlockSpec((TOP_K, tc), lambda i: (0, i), memory_space=pltpu.SMEM),
                  pl.BlockSpec((TOP_K, tc), lambda i: (0, jnp.minimum(i + 1, last)), memory_space=pltpu.SMEM),
                  pl.BlockSpec((tc, D_MODEL), lambda i: (i, 0)),
                  pl.BlockSpec((tc, TOP_K), lambda i: (i, 0)),
                  pl.BlockSpec((1, 6, D_MODEL), lambda i: (i // per_batch, 0, 0)),
                  pl.BlockSpec(memory_space=pl.ANY)],
        out_specs=pl.BlockSpec((tc, D_MODEL), lambda i: (i, 0)),
        out_shape=jax.ShapeDtypeStruct((n, D_MODEL), F32),
        scratch_shapes=[pltpu.VMEM((2, TOP_K, tc, D_MODEL), F32), pltpu.SemaphoreType.DMA((2,))],
        compiler_params=_cparams(("arbitrary",)),
        name="moe_combine",
    )(dest_t, dest_t, x1_flat, gates, mods, ys)


def _moe_layer(x1, mods, norm_g, router_w, router_b, w1, w3, w2):
    bsz = x1.shape[0]
    n = bsz * SEQ
    cap = (n * TOP_K + MOE_BLOCK - 1) // MOE_BLOCK * MOE_BLOCK + N_EXPERTS * MOE_BLOCK
    rw_p = jnp.zeros((D_MODEL, LANES), F32).at[:, :N_EXPERTS].set(router_w)
    rb_p = jnp.zeros((1, LANES), F32).at[0, :N_EXPERTS].set(router_b)
    h2, e_idx, rank, gates, counts = _router_call(x1, mods, norm_g, rw_p, rb_p)
    counts = counts[0, :N_EXPERTS]
    padded = (counts + MOE_BLOCK - 1) // MOE_BLOCK * MOE_BLOCK
    padded_end = jnp.cumsum(padded)
    base = padded_end - padded
    dest_t = (base[e_idx] + rank).T.astype(I32)
    n_blocks = cap // MOE_BLOCK
    block_start = jnp.arange(n_blocks, dtype=I32) * MOE_BLOCK
    block_expert = jnp.minimum(jnp.sum(padded_end[None, :] <= block_start[:, None], axis=1),
                               N_EXPERTS - 1).astype(I32)
    n_active = (padded_end[-1:] // MOE_BLOCK).astype(I32)
    tail = padded_end[-1] + jnp.arange(N_EXPERTS, dtype=I32) * MOE_BLOCK
    pad_start = jnp.concatenate([base + counts, tail])
    pad_start = (jnp.minimum(pad_start, cap - ZERO_BURST) // SUBLANES * SUBLANES).astype(I32)
    xs = _dispatch_call(pad_start, dest_t, h2, cap)
    ys = _expert_call(block_expert, n_active, xs, w1, w3, w2)
    out = _combine_call(dest_t, x1.reshape(n, D_MODEL), gates, mods, ys)
    return out.reshape(bsz, SEQ, D_MODEL)


def _rope_tables():
    rows = SEQ // GRID_W
    row = jnp.repeat(jnp.arange(rows), GRID_W).astype(F32)
    col = jnp.tile(jnp.arange(GRID_W), rows).astype(F32)
    n_freq = MLA_ROPE // 4
    inv = ROPE_BASE ** (-jnp.arange(n_freq, dtype=F32) / n_freq)
    ang = jnp.concatenate([row[:, None] * inv, col[:, None] * inv], axis=-1)
    cos, sin = jnp.cos(ang), jnp.sin(ang)
    ones = jnp.ones((SEQ, MLA_NOPE), F32)
    pad = jnp.zeros((SEQ, LANES - MLA_QK), F32)
    zn = jnp.zeros((SEQ, MLA_NOPE), F32)
    cos_t = jnp.concatenate([ones, cos, cos, pad], axis=-1)
    sin_t = jnp.concatenate([zn, -sin, sin, pad], axis=-1)
    ident = jnp.concatenate([jnp.ones((CTX_LEN, MLA_QK), F32), jnp.zeros((CTX_LEN, LANES - MLA_QK), F32)], axis=-1)
    zero = jnp.zeros((CTX_LEN, LANES), F32)
    return jnp.concatenate([cos_t, ident], axis=0), jnp.concatenate([sin_t, zero], axis=0)


def _rope_copy_pad(a):
    first_half = a[..., MLA_NOPE:MLA_NOPE + ROPE_HALF]
    zeros = jnp.zeros(a.shape[:-1] + (LANES - MLA_QK - ROPE_HALF,), a.dtype)
    return jnp.concatenate([a, first_half, zeros], axis=-1)


def _gate_cols(ga, d):
    h = MLSTM_HEADS
    return jnp.concatenate([ga[..., (2 + d) * h:(3 + d) * h], ga[..., d * h:(d + 1) * h]], axis=-1)


def _relayout_w_in(w_in):
    zeros = lambda n: jnp.zeros((D_MODEL, n), F32)
    w = MLSTM_WIDTH
    ga = w_in[:, OFF_GA:OFF_CQ]
    gate_tile = lambda d: jnp.concatenate([_gate_cols(ga, d), zeros(LANES - GATE_ROWS)], axis=-1)
    main = jnp.concatenate([
        w_in[:, OFF_CQ:OFF_KR],
        _rope_copy_pad(jnp.concatenate([zeros(MLA_NOPE), w_in[:, OFF_KR:OFF_GM]], axis=-1)),
        w_in[:, OFF_GM:IN_COLS],
        w_in[:, 0:w], w_in[:, 2 * w:3 * w],
        w_in[:, 3 * w:4 * w],
        gate_tile(0), gate_tile(1)], axis=-1)
    keys_t = (w_in[:, w:2 * w] * (MLSTM_HEAD_DIM ** -0.5)).T
    gates_t = jnp.concatenate([_gate_cols(ga, 0), _gate_cols(ga, 1)], axis=-1).T
    return main.astype(BF16), keys_t.astype(BF16), gates_t.astype(BF16)


def _relayout_gate_b(gb):
    per_dir = jnp.stack([_gate_cols(gb, 0), _gate_cols(gb, 1)])
    col_form = jnp.pad(per_dir, ((0, 0), (0, LANES - GATE_ROWS)))[:, None, :]
    row_form = jnp.broadcast_to(per_dir[:, :, None], (2, GATE_ROWS, LANES))
    return col_form, row_form


def _layer_params(l, w_in, w_out, mlstm_gate_b, mlstm_norm_g, mla_cq_g, mla_ckv_g, mla_w_uq, mla_w_ukv,
                  mla_q_g, mla_k_g, gmlp_ln_g, gmlp_ln_b, gmlp_w_s, gmlp_b_s):
    pad1 = lambda g: _rope_copy_pad(g)[None, :]
    w_uq = _rope_copy_pad(mla_w_uq[l].reshape(MLA_Q_RANK, MLA_HEADS, MLA_QK)).reshape(MLA_Q_RANK, -1)
    return dict(
        w_in=_relayout_w_in(w_in[l]),
        gate_b=_relayout_gate_b(mlstm_gate_b[l]),
        mlstm_g=mlstm_norm_g[l][None, :],
        cq_g=mla_cq_g[l][None, :], ckv_g=mla_ckv_g[l][None, :],
        w_uq=w_uq.astype(BF16),
        w_kv=mla_w_ukv[l].astype(BF16),
        q_g=pad1(mla_q_g[l]), k_g=pad1(mla_k_g[l]),
        ln_g=gmlp_ln_g[l][None, :], ln_b=gmlp_ln_b[l][None, :],
        w_s=gmlp_w_s[l].reshape(GMLP_GROUPS * CHUNK, CHUNK).astype(BF16),
        b_s=jnp.repeat(gmlp_b_s[l].T, GMLP_GROUP_DIM, axis=1),
        w_out=w_out[l].astype(BF16),
    )


def kernel(x, c, ctx, c_ctx, ada_w, ada_b, norm1_g, norm2_g, w_in, w_out, mlstm_gate_b, mlstm_norm_g, mla_cq_g, mla_ckv_g, mla_w_uq, mla_w_ukv, mla_q_g, mla_k_g, gmlp_ln_g, gmlp_ln_b, gmlp_w_s, gmlp_b_s, ffn_w1, ffn_w3, ffn_w2, moe_router_w, moe_router_b, moe_w1, moe_w3, moe_w2):
    bsz = x.shape[0]
    assert x.shape[1:] == (SEQ, D_MODEL) and ctx.shape[1:] == (CTX_LEN, D_MODEL)
    mod_rows = -(-(bsz + 1) // 8) * 8
    cvec = jnp.zeros((mod_rows, D_MODEL), F32).at[:bsz].set(c).at[bsz].set(c_ctx)
    mods_all = _ada_call(cvec, ada_w.astype(BF16), ada_b[:, None, :]).reshape(DEPTH, mod_rows, 6, D_MODEL)
    rope_tabs = _rope_tables()
    xall = (x, ctx) if DEPTH > 1 else jnp.concatenate([x, ctx], axis=1)
    moe_bf16 = {}
    for l in range(DEPTH):
        last = l == DEPTH - 1
        p = _layer_params(l, w_in, w_out, mlstm_gate_b, mlstm_norm_g, mla_cq_g, mla_ckv_g, mla_w_uq, mla_w_ukv,
                          mla_q_g, mla_k_g, gmlp_ln_g, gmlp_ln_b, gmlp_w_s, gmlp_b_s)
        mods = mods_all[l]
        z, zgc, zkt, zgt = _inproj_call(xall, mods, norm1_g[l][None, :], *p["w_in"], bsz)
        cast_ws = ()
        if (l + 1) % 2 == 1 and l + 1 < DEPTH:
            jn = (l + 1) // 2
            cast_ws = (moe_w1[jn].reshape(-1, D_FF_EXPERT), moe_w3[jn].reshape(-1, D_FF_EXPERT),
                       moe_w2[jn].reshape(-1, D_MODEL))
        a_mix, *cast = _mlstm_call(z, zgc, zkt, zgt, *p["gate_b"], p["mlstm_g"], cast_ws)
        if cast:
            moe_bf16[(l + 1) // 2] = (cast[0].reshape(N_EXPERTS, D_MODEL, D_FF_EXPERT),
                                      cast[1].reshape(N_EXPERTS, D_MODEL, D_FF_EXPERT),
                                      cast[2].reshape(N_EXPERTS, D_FF_EXPERT, D_MODEL))
        q, k, v = _qkv_call(z, p, rope_tabs)
        t_out = SEQ if last else T_ALL
        b_mix = _attn_call(q, k, v, t_out)
        x1 = _mixout_call(xall, z, a_mix, b_mix, mods, p, bsz, t_out, 1024 if last else 768)
        j = l // 2
        if l % 2 == 0:
            assert not last
            xall = _ffn_call(x1, mods, norm2_g[l][None, :], ffn_w1[j].astype(BF16), ffn_w3[j].astype(BF16),
                             ffn_w2[j].astype(BF16), bsz)
        else:
            assert last
            if j not in moe_bf16:
                moe_bf16[j] = (moe_w1[j].astype(BF16), moe_w3[j].astype(BF16), moe_w2[j].astype(BF16))
            xall = _moe_layer(x1, mods, norm2_g[l][None, :], moe_router_w[j], moe_router_b[j], *moe_bf16[j])
    return xall
```

```python
import functools

import jax
import jax.numpy as jnp
from jax import lax
from jax.experimental import pallas as pl
from jax.experimental.pallas import tpu as pltpu

F32 = jnp.float32
BF16 = jnp.bfloat16
I32 = jnp.int32
HIGHEST = lax.Precision.HIGHEST

D_MODEL = 1024
SEQ = 2048
CTX_LEN = 256
T_ALL = SEQ + CTX_LEN
DEPTH = 2
GRID_W = 64
EPS = 1e-6
MLSTM_HEADS = 4
MLSTM_HEAD_DIM = 64
MLSTM_WIDTH = 256
CHUNK = 128
MLA_HEADS = 8
MLA_Q_RANK = 256
MLA_KV_RANK = 128
MLA_NOPE = 64
MLA_ROPE = 32
MLA_V = 64
MLA_QK = 96
MLA_WIDTH = 512
ROPE_BASE = 10000.0
GMLP_GROUPS = 4
GMLP_GROUP_DIM = 64
GMLP_WIDTH = 256
D_FF = 2816
N_EXPERTS = 8
TOP_K = 2
D_FF_EXPERT = 3584
MOE_BLOCK = 512
OFF_GA = 4 * MLSTM_WIDTH
OFF_CQ = OFF_GA + 4 * MLSTM_HEADS
OFF_CKV = OFF_CQ + MLA_Q_RANK
OFF_KR = OFF_CKV + MLA_KV_RANK
OFF_GM = OFF_KR + MLA_ROPE
IN_COLS = OFF_GM + 2 * GMLP_WIDTH

LANES = 128
HEAD_PAD = LANES
ZC_B = 0
ZC_GM = 512
ZC_QV = 1024
ZC_O = 1536
ZC_G = 1792
Z_COLS = ZC_G + 2 * LANES
GATE_ROWS = 8
VMEM_LIMIT = 56 * 1024 * 1024

N_CHUNKS = T_ALL // CHUNK
N_LAT_CHUNKS = SEQ // CHUNK


def _cparams(sem, vmem=VMEM_LIMIT):
    return pltpu.CompilerParams(dimension_semantics=sem, vmem_limit_bytes=vmem)


def _rms(x, g):
    return x * lax.rsqrt(jnp.mean(x * x, axis=-1, keepdims=True) + EPS) * g


def _silu(x):
    return x * jax.nn.sigmoid(x)


def _modulated_norm(x, g, shift, scale):
    return _rms(x, g) * (1.0 + scale) + shift


def _row_mods(ml_ref, mc_ref, first, tile_start, rows):
    row = tile_start + lax.broadcasted_iota(I32, (rows, 1), 0)
    is_ctx = row >= SEQ
    return tuple(jnp.where(is_ctx, mc_ref[0, first + k:first + k + 1, :], ml_ref[0, first + k:first + k + 1, :])
                 for k in range(3))


def _ada_kernel(c_ref, w_ref, b_ref, o_ref):
    s = _silu(c_ref[...]).astype(BF16)
    o_ref[0] = jnp.dot(s, w_ref[0], preferred_element_type=F32) + b_ref[0]


def _ada_call(cvec, ada_w, ada_b):
    rows = cvec.shape[0]
    return pl.pallas_call(
        _ada_kernel,
        grid=(DEPTH, 6),
        in_specs=[pl.BlockSpec((rows, D_MODEL), lambda l, j: (0, 0)),
                  pl.BlockSpec((1, D_MODEL, D_MODEL), lambda l, j: (l, 0, j)),
                  pl.BlockSpec((1, 1, D_MODEL), lambda l, j: (l, 0, j))],
        out_specs=pl.BlockSpec((1, rows, D_MODEL), lambda l, j: (l, 0, j)),
        out_shape=jax.ShapeDtypeStruct((DEPTH, rows, 6 * D_MODEL), F32),
        compiler_params=_cparams(("arbitrary", "arbitrary")),
        name="adaln",
    )(cvec, ada_w, ada_b)


def _stream_specs(src, tr):
    if not isinstance(src, tuple):
        return [pl.BlockSpec((1, tr, D_MODEL), lambda b, i: (b, i, 0))], [src]
    x, ctx = src
    n_full = SEQ // tr
    tail = SEQ - n_full * tr
    assert tail + CTX_LEN == tr and (n_full * tr) % tail == 0
    return ([pl.BlockSpec((1, tr, D_MODEL), lambda b, i: (b, jnp.minimum(i, n_full - 1), 0)),
             pl.BlockSpec((1, tail, D_MODEL), lambda b, i: (b, n_full * tr // tail, 0)),
             pl.BlockSpec((1, CTX_LEN, D_MODEL), lambda b, i: (b, 0, 0))], [x, x, ctx])


def _stream_tile(src_refs, i, tr):
    if len(src_refs) == 1:
        return src_refs[0][0]
    full, tail, ctx = src_refs
    mixed = jnp.concatenate([tail[0], ctx[0]], axis=0)
    return jnp.where(i == SEQ // tr, mixed, full[0])


def _inproj_kernel(*refs, tr, n_src):
    src_refs = refs[:n_src]
    ml_ref, mc_ref, g_ref, w_ref, wk_ref, wg_ref, z_ref, zgc_ref, zk_ref, zg_ref = refs[n_src:]
    shift, scale, _ = _row_mods(ml_ref, mc_ref, 0, pl.program_id(1) * tr, tr)
    x = _stream_tile(src_refs, pl.program_id(1), tr)
    xn = _modulated_norm(x, g_ref[...], shift, scale).astype(BF16)
    z = jnp.dot(xn, w_ref[...], preferred_element_type=F32)
    z_ref[0] = z[:, 0:ZC_G].astype(z_ref.dtype)
    zgc_ref[0] = z[:, ZC_G:Z_COLS]
    nt = (((1,), (1,)), ((), ()))
    zk_ref[0] = lax.dot_general(wk_ref[...], xn, nt, preferred_element_type=F32).astype(zk_ref.dtype)
    zg_ref[0] = lax.dot_general(wg_ref[...], xn, nt, preferred_element_type=F32)


def _inproj_call(src, mods, norm_g, w_main, w_kt, w_gt, n_ctx_row):
    tr = 768
    src_specs, src_ops = _stream_specs(src, tr)
    bsz = src_ops[0].shape[0]
    const = lambda b, i: (0, 0)
    return pl.pallas_call(
        functools.partial(_inproj_kernel, tr=tr, n_src=len(src_ops)),
        grid=(bsz, T_ALL // tr),
        in_specs=src_specs + [
                  pl.BlockSpec((1, 6, D_MODEL), lambda b, i: (b, 0, 0)),
                  pl.BlockSpec((1, 6, D_MODEL), lambda b, i: (n_ctx_row, 0, 0)),
                  pl.BlockSpec((1, D_MODEL), const),
                  pl.BlockSpec((D_MODEL, Z_COLS), const),
                  pl.BlockSpec((MLSTM_WIDTH, D_MODEL), const),
                  pl.BlockSpec((2 * GATE_ROWS, D_MODEL), const)],
        out_specs=[pl.BlockSpec((1, tr, ZC_G), lambda b, i: (b, i, 0)),
                   pl.BlockSpec((1, tr, Z_COLS - ZC_G), lambda b, i: (b, i, 0)),
                   pl.BlockSpec((1, MLSTM_WIDTH, tr), lambda b, i: (b, 0, i)),
                   pl.BlockSpec((1, 2 * GATE_ROWS, tr), lambda b, i: (b, 0, i))],
        out_shape=[jax.ShapeDtypeStruct((bsz, T_ALL, ZC_G), BF16),
                   jax.ShapeDtypeStruct((bsz, T_ALL, Z_COLS - ZC_G), F32),
                   jax.ShapeDtypeStruct((bsz, MLSTM_WIDTH, T_ALL), BF16),
                   jax.ShapeDtypeStruct((bsz, 2 * GATE_ROWS, T_ALL), F32)],
        compiler_params=_cparams(("parallel", "parallel")),
        name="in_proj",
    )(*src_ops, mods, mods, norm_g, w_main, w_kt, w_gt)


def _scan_chunk(d, j):
    fwd_chunk = (j + N_LAT_CHUNKS) % N_CHUNKS
    return jnp.where(d == 0, fwd_chunk, N_CHUNKS - 1 - j)


MLSTM_ROWS_PER_STEP = 8


def _log_sigmoid(x):
    return jnp.minimum(x, 0.0) - jnp.log1p(jnp.exp(-jnp.abs(x)))


def _bf16_head(v):
    bits = lax.bitcast_convert_type(v, jnp.uint32) & jnp.uint32(0xFFFF0000)
    return lax.bitcast_convert_type(bits, F32)


def _bf16_terms(x, axis):
    hi = _bf16_head(x)
    rest = x - hi
    mid = _bf16_head(rest)
    lo = rest - mid
    return jnp.concatenate([hi, mid, lo], axis=axis).astype(BF16)


def _mlstm_kernel(*refs, n_cast):
    ins, rest = refs[:8], refs[8:]
    cast_in, o_ref, cast_out, scratch = rest[:n_cast], rest[n_cast], rest[n_cast + 1:2 * n_cast + 1], rest[2 * n_cast + 1:]
    for src, dst in zip(cast_in, cast_out):
        dst[...] = src[...].astype(dst.dtype)
    _mlstm_step(*ins, o_ref, *scratch)


def _mlstm_step(qv_ref, og_ref, zg_ref, kt_ref, gt_ref, gb_ref, gbt_ref, ng_ref, o_ref, c_ref, m_ref, hf_ref):
    L = CHUNK
    E = MLSTM_HEAD_DIM
    d = pl.program_id(1)
    j = pl.program_id(2)
    fwd = d == 0
    row0 = pl.multiple_of(_scan_chunk(d, j) * L, L)

    @pl.when(j == 0)
    def _():
        c_ref[...] = jnp.zeros_like(c_ref)
        m_ref[...] = jnp.zeros_like(m_ref)

    r_i = lax.broadcasted_iota(I32, (L, L), 0)
    c_i = lax.broadcasted_iota(I32, (L, L), 1)
    prec = jnp.logical_or(jnp.logical_and(fwd, c_i <= r_i), jnp.logical_and(jnp.logical_not(fwd), c_i >= r_i))
    succ = jnp.logical_or(jnp.logical_and(fwd, r_i <= c_i), jnp.logical_and(jnp.logical_not(fwd), r_i >= c_i))
    lane = lax.broadcasted_iota(I32, (L, LANES), 1)
    sub = lax.broadcasted_iota(I32, (LANES, L), 0)
    low_lane = lane < E
    low_sub = sub < E

    row_idx = lax.broadcasted_iota(I32, (L, LANES), 0)
    rows = range(qv_ref.shape[0])
    pairs = range(MLSTM_HEADS // 2)
    h_pairs = {}
    for bb in rows:
        gates_c = zg_ref[bb] + gb_ref[0]
        gates_r = gt_ref[bb] + gbt_ref[0]
        logf_r = _log_sigmoid(gates_r)
        logf_c = jnp.concatenate([logf_r, jnp.zeros((LANES - GATE_ROWS, L), F32)], axis=0).T
        parts_c = jnp.dot(prec.astype(BF16), _bf16_terms(logf_c, 1), preferred_element_type=F32)
        b_c = parts_c[:, 0:LANES] + parts_c[:, LANES:2 * LANES] + parts_c[:, 2 * LANES:3 * LANES]
        c_c = pltpu.roll(gates_c, LANES - MLSTM_HEADS, 1) - b_c
        m_row = m_ref[bb]
        cm = c_c
        for sh in (1, 2, 4, 8, 16, 32, 64):
            prev = jnp.where(fwd, jnp.where(row_idx >= sh, pltpu.roll(cm, sh, 0), -jnp.inf),
                             jnp.where(row_idx < L - sh, pltpu.roll(cm, L - sh, 0), -jnp.inf))
            cm = jnp.maximum(cm, prev)
        m_c = jnp.maximum(cm, m_row)
        a_c = jnp.exp(m_row - m_c)
        floor_c = jnp.exp(-(b_c + m_c))
        m_last = jnp.maximum(jnp.max(c_c, axis=0, keepdims=True), m_row)
        b_last = jnp.where(fwd, b_c[L - 1:L, :], b_c[0:1, :])
        m_ref[bb] = b_last + m_last
        decay_row = jnp.exp(m_row - m_last)
        parts_r = jnp.dot(_bf16_terms(logf_r, 0), succ.astype(BF16), preferred_element_type=F32)
        b_r = parts_r[0:GATE_ROWS] + parts_r[GATE_ROWS:2 * GATE_ROWS] + parts_r[2 * GATE_ROWS:3 * GATE_ROWS]
        c_r = gates_r[MLSTM_HEADS:2 * MLSTM_HEADS, :] - b_r[0:MLSTM_HEADS, :]
        w_r = [jnp.exp(c_r[h:h + 1, :] - m_last[:, h:h + 1]) for h in range(MLSTM_HEADS)]

        for pj in pairs:
            tile = slice(pj * LANES, (pj + 1) * LANES)
            q_t = qv_ref[bb, :, tile]
            v_t = qv_ref[bb, :, MLSTM_WIDTH + pj * LANES:MLSTM_WIDTH + (pj + 1) * LANES]
            kt_t = kt_ref[bb, tile, :]
            c_old = c_ref[bb, pj]
            q_both = jnp.concatenate([jnp.where(low_lane, q_t, 0.0), jnp.where(low_lane, 0.0, q_t)],
                                     axis=0).astype(BF16)
            kc = jnp.concatenate([kt_t.astype(BF16), c_old.astype(BF16)], axis=1)
            qk_qc = jnp.dot(q_both, kc, preferred_element_type=F32)
            hx = []
            upd = []
            for par in range(2):
                h = 2 * pj + par
                own_lane = low_lane if par == 0 else jnp.logical_not(low_lane)
                own_sub = low_sub if par == 0 else jnp.logical_not(low_sub)
                den_lane = E if par == 0 else 0
                v_ext = jnp.where(own_lane, v_t, jnp.where(lane == den_lane, 1.0, 0.0)).astype(BF16)
                sqk = qk_qc[par * L:(par + 1) * L, 0:L]
                q_c = qk_qc[par * L:(par + 1) * L, L:L + LANES]
                dmat = jnp.where(prec, jnp.exp(c_r[h:h + 1, :] - m_c[:, h:h + 1]), 0.0)
                kw_t = jnp.where(own_sub, kt_t, 0.0) * w_r[h]
                pv_upd = jnp.dot(jnp.concatenate([sqk * dmat, kw_t], axis=0).astype(BF16), v_ext,
                                 preferred_element_type=F32)
                nd = pv_upd[0:L] + a_c[:, h:h + 1] * q_c
                den = nd[:, den_lane:den_lane + 1]
                hx.append(nd / jnp.maximum(jnp.abs(den), floor_c[:, h:h + 1]))
                upd.append(pv_upd[L:L + LANES])
            decay = jnp.where(low_sub, decay_row[:, 2 * pj:2 * pj + 1], decay_row[:, 2 * pj + 1:2 * pj + 2])
            c_ref[bb, pj] = decay * c_old + upd[0] + upd[1]
            h_pairs[bb, pj] = jnp.where(low_lane, hx[0], hx[1])

    @pl.when(fwd)
    def _():
        for (bb, pj), h_pair in h_pairs.items():
            hf_ref[bb, pl.ds(row0, L), pj * LANES:(pj + 1) * LANES] = h_pair

    @pl.when(jnp.logical_not(fwd))
    def _():
        for (bb, pj), h_pair in h_pairs.items():
            tile = slice(pj * LANES, (pj + 1) * LANES)
            hs = hf_ref[bb, pl.ds(row0, L), tile] + h_pair
            sq = hs * hs
            ss = jnp.where(low_lane,
                           jnp.sum(jnp.where(low_lane, sq, 0.0), axis=-1, keepdims=True),
                           jnp.sum(jnp.where(low_lane, 0.0, sq), axis=-1, keepdims=True))
            y = hs * lax.rsqrt(ss * (1.0 / E) + EPS) * ng_ref[:, tile]
            o_ref[bb, :, tile] = (jax.nn.sigmoid(og_ref[bb, :, tile].astype(F32)) * y).astype(o_ref.dtype)


def _mlstm_call(z, zgc, zkt, zgt, gate_b, gate_bt, norm_g, cast_ws=()):
    bsz = z.shape[0]
    nb = max(r for r in range(1, MLSTM_ROWS_PER_STEP + 1) if bsz % r == 0)
    last = N_CHUNKS - 1
    chunk = lambda b, d, j: _scan_chunk(d, j)
    steps = (bsz // nb) * 2 * N_CHUNKS
    n_blk = 1 << (steps.bit_length() - 1)
    cast_idx = lambda b, d, j: (jnp.minimum((b * 2 + d) * N_CHUNKS + j, n_blk - 1), 0)
    cast_specs = [pl.BlockSpec((w.shape[0] // n_blk, w.shape[1]), cast_idx) for w in cast_ws]
    assert all(w.shape[0] % (16 * n_blk) == 0 for w in cast_ws)
    return pl.pallas_call(
        functools.partial(_mlstm_kernel, n_cast=len(cast_ws)),
        grid=(bsz // nb, 2, N_CHUNKS),
        in_specs=[pl.BlockSpec((nb, CHUNK, 2 * MLSTM_WIDTH), lambda b, d, j: (b, chunk(b, d, j), ZC_QV // 512)),
                  pl.BlockSpec((nb, CHUNK, MLSTM_WIDTH), lambda b, d, j: (b, chunk(b, d, j), ZC_O // MLSTM_WIDTH)),
                  pl.BlockSpec((nb, CHUNK, LANES), lambda b, d, j: (b, chunk(b, d, j), d)),
                  pl.BlockSpec((nb, MLSTM_WIDTH, CHUNK), lambda b, d, j: (b, 0, chunk(b, d, j))),
                  pl.BlockSpec((nb, GATE_ROWS, CHUNK), lambda b, d, j: (b, d, chunk(b, d, j))),
                  pl.BlockSpec((1, 1, LANES), lambda b, d, j: (d, 0, 0)),
                  pl.BlockSpec((1, GATE_ROWS, LANES), lambda b, d, j: (d, 0, 0)),
                  pl.BlockSpec((1, MLSTM_WIDTH), lambda b, d, j: (0, 0))] + cast_specs,
        out_specs=[pl.BlockSpec((nb, CHUNK, MLSTM_WIDTH),
                                lambda b, d, j: (b, jnp.where(d == 0, last, last - j), 0))] + cast_specs,
        out_shape=[jax.ShapeDtypeStruct((bsz, T_ALL, MLSTM_WIDTH), BF16)]
        + [jax.ShapeDtypeStruct(w.shape, BF16) for w in cast_ws],
        scratch_shapes=[pltpu.VMEM((nb, MLSTM_HEADS // 2, LANES, LANES), F32),
                        pltpu.VMEM((nb, 1, LANES), F32),
                        pltpu.VMEM((nb, T_ALL, MLSTM_WIDTH), F32)],
        compiler_params=_cparams(("arbitrary", "arbitrary", "arbitrary")),
        name="mlstm",
    )(z, z, zgc, zkt, zgt, gate_b, gate_bt, norm_g, *cast_ws)


ROPE_HALF = MLA_ROPE // 2


def _rope(t, cos_t, sin_t):
    return t * cos_t + pltpu.roll(t, LANES - ROPE_HALF, 1) * sin_t


HEAD_PAIR = 2 * HEAD_PAD
ATT_DEN_LANE = 0
assert MLA_NOPE + MLA_V == HEAD_PAD


def _head_rms_scale(raw, ones_ref):
    sq = (raw * raw).astype(BF16)
    ss = jnp.concatenate([jnp.dot(sq[:, j * HEAD_PAIR:(j + 1) * HEAD_PAIR], ones_ref[...],
                                  preferred_element_type=F32) for j in range(MLA_HEADS // 2)], axis=-1)
    return lax.rsqrt(ss * (1.0 / MLA_QK) + EPS)


def _qkv_kernel(z_ref, cqg_ref, ckvg_ref, wq_ref, wkv_ref, ones_ref, qc_ref, qs_ref, kc_ref, ks_ref,
                q_ref, k_ref, v_ref, *, tr):
    cqn = _rms(z_ref[0, :, 0:MLA_Q_RANK].astype(F32), cqg_ref[...]).astype(BF16)
    ckvn = _rms(z_ref[0, :, MLA_Q_RANK:MLA_Q_RANK + MLA_KV_RANK].astype(F32), ckvg_ref[...]).astype(BF16)
    k_rope = z_ref[0, :, MLA_Q_RANK + MLA_KV_RANK:MLA_Q_RANK + MLA_KV_RANK + LANES].astype(F32)
    q_raw = jnp.dot(cqn, wq_ref[...], preferred_element_type=F32)
    kv_raw = jnp.dot(ckvn, wkv_ref[...], preferred_element_type=F32)
    lane_all = lax.broadcasted_iota(I32, (tr, MLA_HEADS * HEAD_PAD), 1) % HEAD_PAD
    k_raw = jnp.where(lane_all < MLA_NOPE, kv_raw, jnp.concatenate([k_rope] * MLA_HEADS, axis=-1))
    q_n = q_raw * _head_rms_scale(q_raw, ones_ref)
    k_n = k_raw * _head_rms_scale(k_raw, ones_ref)
    q_cos, q_sin, k_cos, k_sin = qc_ref[...], qs_ref[...], kc_ref[...], ks_ref[...]
    lane = lax.broadcasted_iota(I32, (tr, LANES), 1)
    for h in range(MLA_HEADS):
        sl = slice(h * HEAD_PAD, (h + 1) * HEAD_PAD)
        q_ref[0, :, sl] = _rope(q_n[:, sl], q_cos, q_sin).astype(BF16)
        k_ref[0, :, sl] = _rope(k_n[:, sl], k_cos, k_sin).astype(BF16)
        v_ref[0, :, sl] = jnp.where(lane >= MLA_NOPE, kv_raw[:, sl],
                                    jnp.where(lane == ATT_DEN_LANE, 1.0, 0.0)).astype(BF16)


def _qkv_call(z, p, rope_tabs):
    bsz = z.shape[0]
    tr = 768
    hw = MLA_HEADS * HEAD_PAD
    const = lambda b, i: (0, 0)
    out = jax.ShapeDtypeStruct((bsz, T_ALL, hw), BF16)
    tab = pl.BlockSpec((tr, LANES), lambda b, i: (i, 0))
    ospec = pl.BlockSpec((1, tr, hw), lambda b, i: (b, i, 0))
    return pl.pallas_call(
        functools.partial(_qkv_kernel, tr=tr),
        grid=(bsz, T_ALL // tr),
        in_specs=[pl.BlockSpec((1, tr, 512), lambda b, i: (b, i, ZC_B // 512)),
                  pl.BlockSpec((1, MLA_Q_RANK), const), pl.BlockSpec((1, MLA_KV_RANK), const),
                  pl.BlockSpec((MLA_Q_RANK, hw), const), pl.BlockSpec((MLA_KV_RANK, hw), const),
                  pl.BlockSpec((HEAD_PAIR, HEAD_PAIR), const),
                  tab, tab, tab, tab],
        out_specs=[ospec, ospec, ospec],
        out_shape=[out, out, out],
        compiler_params=_cparams(("parallel", "parallel")),
        name="mla_qkv",
    )(z, p["cq_g"], p["ckv_g"], p["w_uq"], p["w_kv"], _head_ones(),
      *_gained_rope_tables(rope_tabs, p["q_g"], MLA_QK ** -0.5), *_gained_rope_tables(rope_tabs, p["k_g"], 1.0))


def _gained_rope_tables(rope_tabs, gain, scale):
    cos_t, sin_t = rope_tabs
    partner_gain = jnp.roll(gain, -ROPE_HALF, axis=1)
    return cos_t * (gain * scale), sin_t * (partner_gain * scale)


def _head_ones():
    r = jnp.arange(HEAD_PAIR)
    same_head = (r[:, None] // HEAD_PAD) == (r[None, :] // HEAD_PAD)
    real_row = (r[:, None] % HEAD_PAD) < MLA_QK
    return jnp.logical_and(same_head, real_row).astype(BF16)


ATT_TQ = 1024
assert SEQ % ATT_TQ == 0 and CTX_LEN <= ATT_TQ


def _attn_kernel(q_ref, k_ref, v_ref, o_ref):
    i = pl.program_id(1)

    def run(rows, k0):
        def scores(hh):
            sl = slice(hh * HEAD_PAD, (hh + 1) * HEAD_PAD)
            return lax.dot_general(q_ref[0, 0:rows, sl], k_ref[0, k0:T_ALL, sl], (((1,), (1,)), ((), ())),
                                   preferred_element_type=F32)

        s = scores(0)
        for hh in range(MLA_HEADS):
            s_next = scores(hh + 1) if hh + 1 < MLA_HEADS else None
            p = jnp.exp(s - jnp.max(s, axis=-1, keepdims=True)).astype(BF16)
            oe = jnp.dot(p, v_ref[0, k0:T_ALL, hh * HEAD_PAD:(hh + 1) * HEAD_PAD], preferred_element_type=F32)
            o_ref[0, 0:rows, hh * MLA_V:(hh + 1) * MLA_V] = (
                oe[:, MLA_NOPE:MLA_NOPE + MLA_V] / oe[:, ATT_DEN_LANE:ATT_DEN_LANE + 1]).astype(o_ref.dtype)
            s = s_next

    @pl.when(i < SEQ // ATT_TQ)
    def _():
        run(ATT_TQ, 0)

    @pl.when(i >= SEQ // ATT_TQ)
    def _():
        run(CTX_LEN, SEQ)
        o_ref[0, CTX_LEN:ATT_TQ, :] = jnp.zeros((ATT_TQ - CTX_LEN, MLA_WIDTH), o_ref.dtype)


def _attn_call(q, k, v, t_out):
    bsz = q.shape[0]
    w = MLA_HEADS * HEAD_PAD
    return pl.pallas_call(
        _attn_kernel,
        grid=(bsz, pl.cdiv(t_out, ATT_TQ)),
        in_specs=[pl.BlockSpec((1, ATT_TQ, w), lambda b, i: (b, i, 0)),
                  pl.BlockSpec((1, T_ALL, w), lambda b, i: (b, 0, 0)),
                  pl.BlockSpec((1, T_ALL, w), lambda b, i: (b, 0, 0))],
        out_specs=pl.BlockSpec((1, ATT_TQ, MLA_WIDTH), lambda b, i: (b, i, 0)),
        out_shape=jax.ShapeDtypeStruct((bsz, t_out, MLA_WIDTH), BF16),
        compiler_params=_cparams(("parallel", "arbitrary")),
        name="mla_attn",
    )(q, k, v)


def _mixout_kernel(*refs, tr, n_src):
    src_refs = refs[:n_src]
    zg_ref, a_ref, b_ref, ml_ref, mc_ref, lng_ref, lnb_ref, ws_ref, bs_ref, wo_ref, o_ref, cm_ref = refs[n_src:]
    gd = GMLP_GROUP_DIM
    act = jax.nn.gelu(zg_ref[0].astype(F32))
    u = act[:, 0:GMLP_WIDTH]
    vv = act[:, GMLP_WIDTH:2 * GMLP_WIDTH]
    mu = jnp.mean(vv, axis=-1, keepdims=True)
    var = jnp.mean(jnp.square(vv - mu), axis=-1, keepdims=True)
    vn = ((vv - mu) * lax.rsqrt(var + EPS) * lng_ref[...] + lnb_ref[...]).astype(BF16)
    col_group = lax.broadcasted_iota(I32, (CHUNK, GMLP_WIDTH), 1) // gd
    for c in range(tr // CHUNK):
        rows = slice(c * CHUNK, (c + 1) * CHUNK)
        mixed = jnp.dot(ws_ref[...], vn[rows, :], preferred_element_type=F32)
        sv = mixed[0:CHUNK]
        for g in range(1, GMLP_GROUPS):
            sv = jnp.where(col_group == g, mixed[g * CHUNK:(g + 1) * CHUNK], sv)
        cm_ref[rows, :] = u[rows, :] * (sv + bs_ref[...])
    y = jnp.dot(a_ref[0].astype(BF16), wo_ref[0:MLSTM_WIDTH, :], preferred_element_type=F32)
    y += jnp.dot(b_ref[0].astype(BF16), wo_ref[MLSTM_WIDTH:MLSTM_WIDTH + MLA_WIDTH, :],
                 preferred_element_type=F32)
    y += jnp.dot(cm_ref[...].astype(BF16), wo_ref[MLSTM_WIDTH + MLA_WIDTH:, :], preferred_element_type=F32)
    _, _, gate = _row_mods(ml_ref, mc_ref, 0, pl.program_id(1) * tr, tr)
    o_ref[0] = _stream_tile(src_refs, pl.program_id(1), tr) + gate * y


def _mixout_call(src, z, a_mix, b_mix, mods, p, n_ctx_row, t_out, tr):
    src_specs, src_ops = _stream_specs(src, tr)
    bsz = src_ops[0].shape[0]
    const = lambda b, i: (0, 0)
    return pl.pallas_call(
        functools.partial(_mixout_kernel, tr=tr, n_src=len(src_ops)),
        grid=(bsz, t_out // tr),
        in_specs=src_specs + [
                  pl.BlockSpec((1, tr, 2 * GMLP_WIDTH), lambda b, i: (b, i, ZC_GM // (2 * GMLP_WIDTH))),
                  pl.BlockSpec((1, tr, MLSTM_WIDTH), lambda b, i: (b, i, 0)),
                  pl.BlockSpec((1, tr, MLA_WIDTH), lambda b, i: (b, i, 0)),
                  pl.BlockSpec((1, 6, D_MODEL), lambda b, i: (b, 0, 0)),
                  pl.BlockSpec((1, 6, D_MODEL), lambda b, i: (n_ctx_row, 0, 0)),
                  pl.BlockSpec((1, GMLP_WIDTH), const), pl.BlockSpec((1, GMLP_WIDTH), const),
                  pl.BlockSpec((GMLP_GROUPS * CHUNK, CHUNK), const),
                  pl.BlockSpec((CHUNK, GMLP_WIDTH), const),
                  pl.BlockSpec((D_MODEL, D_MODEL), const)],
        out_specs=pl.BlockSpec((1, tr, D_MODEL), lambda b, i: (b, i, 0)),
        out_shape=jax.ShapeDtypeStruct((bsz, t_out, D_MODEL), F32),
        scratch_shapes=[pltpu.VMEM((tr, GMLP_WIDTH), F32)],
        compiler_params=_cparams(("parallel", "parallel")),
        name="mix_out",
    )(*src_ops, z, a_mix, b_mix, mods, mods, p["ln_g"], p["ln_b"], p["w_s"], p["b_s"], p["w_out"])


FFN_SPLIT = 2


def _ffn_kernel(x_ref, ml_ref, mc_ref, g_ref, w1_ref, w3_ref, w2_ref, o_ref, *, tr):
    shift, scale, gate = _row_mods(ml_ref, mc_ref, 3, pl.program_id(1) * tr, tr)
    x = x_ref[0]
    h = _modulated_norm(x, g_ref[...], shift, scale).astype(BF16)
    fc = D_FF // FFN_SPLIT
    y = jnp.zeros((tr, D_MODEL), F32)
    for f in range(FFN_SPLIT):
        cols = slice(f * fc, (f + 1) * fc)
        h1 = jnp.dot(h, w1_ref[:, cols], preferred_element_type=F32)
        h3 = jnp.dot(h, w3_ref[:, cols], preferred_element_type=F32)
        y += jnp.dot((_silu(h1) * h3).astype(BF16), w2_ref[cols, :], preferred_element_type=F32)
    o_ref[0] = x + gate * y


def _ffn_call(x1, mods, norm_g, w1, w3, w2, n_ctx_row):
    bsz = x1.shape[0]
    tr = 768
    const = lambda b, i: (0, 0)
    resident = pl.Buffered(1)
    return pl.pallas_call(
        functools.partial(_ffn_kernel, tr=tr),
        grid=(bsz, T_ALL // tr),
        in_specs=[pl.BlockSpec((1, tr, D_MODEL), lambda b, i: (b, i, 0)),
                  pl.BlockSpec((1, 6, D_MODEL), lambda b, i: (b, 0, 0)),
                  pl.BlockSpec((1, 6, D_MODEL), lambda b, i: (n_ctx_row, 0, 0)),
                  pl.BlockSpec((1, D_MODEL), const),
                  pl.BlockSpec((D_MODEL, D_FF), const, pipeline_mode=resident),
                  pl.BlockSpec((D_MODEL, D_FF), const, pipeline_mode=resident),
                  pl.BlockSpec((D_FF, D_MODEL), const, pipeline_mode=resident)],
        out_specs=pl.BlockSpec((1, tr, D_MODEL), lambda b, i: (b, i, 0)),
        out_shape=jax.ShapeDtypeStruct((bsz, T_ALL, D_MODEL), F32),
        compiler_params=_cparams(("parallel", "parallel")),
        name="dense_ffn",
    )(x1, mods, mods, norm_g, w1, w3, w2)


ROUTE_TR = 1024


def _router_kernel(x_ref, ml_ref, g_ref, rw_ref, rb_ref, h_ref, e_ref, r_ref, gt_ref, cnt_ref, run_ref, *, tr):
    step = pl.program_id(0) * pl.num_programs(1) + pl.program_id(1)

    @pl.when(step == 0)
    def _():
        run_ref[...] = jnp.zeros_like(run_ref)

    h = _modulated_norm(x_ref[0], g_ref[...], ml_ref[0, 3:4, :], ml_ref[0, 4:5, :])
    h_ref[...] = h
    lane = lax.broadcasted_iota(I32, (tr, LANES), 1)
    h_hi = _bf16_head(h)
    h_lo = (h - h_hi).astype(BF16)
    w = rw_ref[...]
    w_hi = _bf16_head(w)
    w_lo = (w - w_hi).astype(BF16)
    first = jnp.dot(h_hi.astype(BF16), jnp.concatenate([w_hi, w_lo.astype(F32)], axis=1).astype(BF16),
                    preferred_element_type=F32)
    logits = (first[:, 0:LANES] + first[:, LANES:2 * LANES]
              + jnp.dot(h_lo, w_hi.astype(BF16), preferred_element_type=F32)) + rb_ref[...]
    logits = jnp.where(lane < N_EXPERTS, logits, -jnp.inf)
    m1 = jnp.max(logits, axis=-1, keepdims=True)
    e1 = jnp.min(jnp.where(logits == m1, lane, LANES), axis=-1, keepdims=True)
    rest = jnp.where(lane == e1, -jnp.inf, logits)
    m2 = jnp.max(rest, axis=-1, keepdims=True)
    e2 = jnp.min(jnp.where(rest == m2, lane, LANES), axis=-1, keepdims=True)
    ex = jnp.exp(m2 - m1)
    g1 = 1.0 / (1.0 + ex)
    g2 = ex / (1.0 + ex)
    onehot = jnp.logical_or(lane == e1, lane == e2 + N_EXPERTS)
    oh = onehot.astype(F32)
    r_i = lax.broadcasted_iota(I32, (tr, tr), 0)
    c_i = lax.broadcasted_iota(I32, (tr, tr), 1)
    before = jnp.dot((c_i < r_i).astype(BF16), oh.astype(BF16), preferred_element_type=F32)
    tot = jnp.sum(oh, axis=0, keepdims=True)
    tot0_shift = pltpu.roll(tot, N_EXPERTS, 1)
    run = run_ref[...]
    first_half = lax.broadcasted_iota(I32, (1, LANES), 1) < N_EXPERTS
    offs = run + jnp.where(first_half, 0.0, tot0_shift)
    ranks = oh * (before + offs)
    rank1 = jnp.sum(jnp.where(lane < N_EXPERTS, ranks, 0.0), axis=-1, keepdims=True)
    rank2 = jnp.sum(jnp.where(lane >= N_EXPERTS, ranks, 0.0), axis=-1, keepdims=True)
    col = lax.broadcasted_iota(I32, (tr, TOP_K), 1)
    e_ref[...] = jnp.where(col == 0, e1, e2)
    r_ref[...] = jnp.where(col == 0, rank1, rank2).astype(I32)
    gt_ref[...] = jnp.where(col == 0, g1, g2)
    both = tot + jnp.where(first_half, pltpu.roll(tot, LANES - N_EXPERTS, 1), tot0_shift)
    new_run = run + both
    run_ref[...] = new_run
    cnt_ref[...] = new_run.astype(I32)


def _router_call(x1, mods, norm_g, rw_p, rb_p):
    bsz = x1.shape[0]
    tr = ROUTE_TR
    n = bsz * SEQ
    nt = SEQ // tr
    const = lambda b, i: (0, 0)
    tok = lambda b, i: (b * nt + i, 0)
    return pl.pallas_call(
        functools.partial(_router_kernel, tr=tr),
        grid=(bsz, nt),
        in_specs=[pl.BlockSpec((1, tr, D_MODEL), lambda b, i: (b, i, 0)),
                  pl.BlockSpec((1, 6, D_MODEL), lambda b, i: (b, 0, 0)),
                  pl.BlockSpec((1, D_MODEL), const),
                  pl.BlockSpec((D_MODEL, LANES), const),
                  pl.BlockSpec((1, LANES), const)],
        out_specs=[pl.BlockSpec((tr, D_MODEL), tok),
                   pl.BlockSpec((tr, TOP_K), tok), pl.BlockSpec((tr, TOP_K), tok), pl.BlockSpec((tr, TOP_K), tok),
                   pl.BlockSpec((1, LANES), const)],
        out_shape=[jax.ShapeDtypeStruct((n, D_MODEL), F32),
                   jax.ShapeDtypeStruct((n, TOP_K), I32), jax.ShapeDtypeStruct((n, TOP_K), I32),
                   jax.ShapeDtypeStruct((n, TOP_K), F32),
                   jax.ShapeDtypeStruct((1, LANES), I32)],
        scratch_shapes=[pltpu.VMEM((1, LANES), F32)],
        compiler_params=_cparams(("arbitrary", "arbitrary")),
        name="moe_router",
    )(x1, mods, norm_g, rw_p, rb_p)


DISPATCH_TD = 1024
SUBLANES = 8
ZERO_BURST = MOE_BLOCK + SUBLANES
N_ZERO_BURSTS = 2 * N_EXPERTS
DMA_ISSUE_UNROLL = 8


def _row_copy(src, src_row, dst, dst_row, sem):
    return pltpu.make_async_copy(src.at[pl.ds(src_row, 1)], dst.at[pl.ds(dst_row, 1)], sem)


def _dispatch_kernel(pad_ref, dest_ref, h_ref, xs_out, zero_ref, sem):
    @pl.when(pl.program_id(0) == 0)
    def _():
        zero_ref[...] = jnp.zeros_like(zero_ref)
        for e in range(N_ZERO_BURSTS):
            start = pl.multiple_of(pad_ref[e], SUBLANES)
            burst = pltpu.make_async_copy(zero_ref, xs_out.at[pl.ds(start, ZERO_BURST)], sem)
            burst.start()
            burst.wait()

    def issue(r, carry):
        for kk in range(TOP_K):
            _row_copy(h_ref, r, xs_out, dest_ref[kk, r], sem).start()
        return carry

    lax.fori_loop(0, DISPATCH_TD, issue, 0, unroll=DMA_ISSUE_UNROLL)

    for kk in range(TOP_K):
        pltpu.make_async_copy(h_ref, xs_out.at[pl.ds(0, DISPATCH_TD)], sem).wait()


def _dispatch_call(pad_start, dest_t, h2, cap):
    n = h2.shape[0]
    grid_spec = pltpu.PrefetchScalarGridSpec(
        num_scalar_prefetch=1,
        grid=(n // DISPATCH_TD,),
        in_specs=[pl.BlockSpec((TOP_K, DISPATCH_TD), lambda i, pad: (0, i), memory_space=pltpu.SMEM),
                  pl.BlockSpec((DISPATCH_TD, D_MODEL), lambda i, pad: (i, 0))],
        out_specs=pl.BlockSpec(memory_space=pl.ANY),
        scratch_shapes=[pltpu.VMEM((ZERO_BURST, D_MODEL), F32), pltpu.SemaphoreType.DMA(())],
    )
    return pl.pallas_call(
        _dispatch_kernel,
        grid_spec=grid_spec,
        out_shape=jax.ShapeDtypeStruct((cap, D_MODEL), F32),
        compiler_params=_cparams(("arbitrary",)),
        name="moe_dispatch",
    )(pad_start, dest_t, h2)


EXPERT_FSPLIT = 2


def _expert_kernel(be_ref, na_ref, xs_ref, w1_ref, w3_ref, w2_ref, ys_ref):
    i = pl.program_id(0)
    f = pl.program_id(1)
    active = i < na_ref[0]

    @pl.when(jnp.logical_and(active, f == 0))
    def _():
        ys_ref[...] = jnp.zeros_like(ys_ref)

    @pl.when(active)
    def _():
        x = xs_ref[...].astype(BF16)
        h1 = jnp.dot(x, w1_ref[0], preferred_element_type=F32)
        h3 = jnp.dot(x, w3_ref[0], preferred_element_type=F32)
        ys_ref[...] += jnp.dot((_silu(h1) * h3).astype(BF16), w2_ref[0], preferred_element_type=F32)

    @pl.when(jnp.logical_not(active))
    def _():
        ys_ref[...] = jnp.zeros_like(ys_ref)


def _expert_call(block_expert, n_active, xs, w1, w3, w2):
    cap = xs.shape[0]
    nb = cap // MOE_BLOCK
    fc = D_FF_EXPERT // EXPERT_FSPLIT
    last_f = EXPERT_FSPLIT - 1

    def blk(i, na):
        return jnp.minimum(i, jnp.maximum(na[0] - 1, 0))

    def fidx(i, f, na):
        return jnp.where(i < na[0], f, last_f)

    grid_spec = pltpu.PrefetchScalarGridSpec(
        num_scalar_prefetch=2,
        grid=(nb, EXPERT_FSPLIT),
        in_specs=[pl.BlockSpec((MOE_BLOCK, D_MODEL), lambda i, f, be, na: (blk(i, na), 0)),
                  pl.BlockSpec((1, D_MODEL, fc), lambda i, f, be, na: (be[blk(i, na)], 0, fidx(i, f, na))),
                  pl.BlockSpec((1, D_MODEL, fc), lambda i, f, be, na: (be[blk(i, na)], 0, fidx(i, f, na))),
                  pl.BlockSpec((1, fc, D_MODEL), lambda i, f, be, na: (be[blk(i, na)], fidx(i, f, na), 0))],
        out_specs=pl.BlockSpec((MOE_BLOCK, D_MODEL), lambda i, f, be, na: (i, 0)),
    )
    return pl.pallas_call(
        _expert_kernel,
        grid_spec=grid_spec,
        out_shape=jax.ShapeDtypeStruct((cap, D_MODEL), F32),
        compiler_params=_cparams(("arbitrary", "arbitrary")),
        name="moe_experts",
    )(block_expert, n_active, xs, w1, w3, w2)


COMBINE_TC = 512


def _combine_kernel(dest_ref, dest_next_ref, x_ref, gt_ref, ml_ref, ys_hbm, o_ref, buf_ref, sem):
    i = pl.program_id(0)
    slot = i % 2

    def gather(d_ref, s):
        def issue(r, carry):
            for kk in range(TOP_K):
                _row_copy(ys_hbm, d_ref[kk, r], buf_ref.at[s, kk], r, sem.at[s]).start()
            return carry

        lax.fori_loop(0, COMBINE_TC, issue, 0, unroll=DMA_ISSUE_UNROLL)

    @pl.when(i == 0)
    def _():
        gather(dest_ref, 0)

    @pl.when(i + 1 < pl.num_programs(0))
    def _():
        gather(dest_next_ref, 1 - slot)

    for kk in range(TOP_K):
        pltpu.make_async_copy(ys_hbm.at[pl.ds(0, COMBINE_TC)], buf_ref.at[slot, kk], sem.at[slot]).wait()
    g = gt_ref[...]
    y = buf_ref[slot, 0] * g[:, 0:1] + buf_ref[slot, 1] * g[:, 1:2]
    o_ref[...] = x_ref[...] + ml_ref[0, 5:6, :] * y


def _combine_call(dest_t, x1_flat, gates, mods, ys):
    n = x1_flat.shape[0]
    tc = COMBINE_TC
    per_batch = SEQ // tc
    last = n // tc - 1
    return pl.pallas_call(
        _combine_kernel,
        grid=(n // tc,),
        in_specs=[pl.BlockSpec((TOP_K, tc), lambda i: (0, i), memory_space=pltpu.SMEM),
                  pl.BlockSpec((TOP_K, tc), lambda i: (0, jnp.minimum(i + 1, last)), memory_space=pltpu.SMEM),
                  pl.BlockSpec((tc, D_MODEL), lambda i: (i, 0)),
                  pl.BlockSpec((tc, TOP_K), lambda i: (i, 0)),
                  pl.BlockSpec((1, 6, D_MODEL), lambda i: (i // per_batch, 0, 0)),
                  pl.BlockSpec(memory_space=pl.ANY)],
        out_specs=pl.BlockSpec((tc, D_MODEL), lambda i: (i, 0)),
        out_shape=jax.ShapeDtypeStruct((n, D_MODEL), F32),
        scratch_shapes=[pltpu.VMEM((2, TOP_K, tc, D_MODEL), F32), pltpu.SemaphoreType.DMA((2,))],
        compiler_params=_cparams(("arbitrary",)),
        name="moe_combine",
    )(dest_t, dest_t, x1_flat, gates, mods, ys)


def _moe_layer(x1, mods, norm_g, router_w, router_b, w1, w3, w2):
    bsz = x1.shape[0]
    n = bsz * SEQ
    cap = (n * TOP_K + MOE_BLOCK - 1) // MOE_BLOCK * MOE_BLOCK + N_EXPERTS * MOE_BLOCK
    rw_p = jnp.zeros((D_MODEL, LANES), F32).at[:, :N_EXPERTS].set(router_w)
    rb_p = jnp.zeros((1, LANES), F32).at[0, :N_EXPERTS].set(router_b)
    h2, e_idx, rank, gates, counts = _router_call(x1, mods, norm_g, rw_p, rb_p)
    counts = counts[0, :N_EXPERTS]
    padded = (counts + MOE_BLOCK - 1) // MOE_BLOCK * MOE_BLOCK
    padded_end = jnp.cumsum(padded)
    base = padded_end - padded
    dest_t = (base[e_idx] + rank).T.astype(I32)
    n_blocks = cap // MOE_BLOCK
    block_start = jnp.arange(n_blocks, dtype=I32) * MOE_BLOCK
    block_expert = jnp.minimum(jnp.sum(padded_end[None, :] <= block_start[:, None], axis=1),
                               N_EXPERTS - 1).astype(I32)
    n_active = (padded_end[-1:] // MOE_BLOCK).astype(I32)
    tail = padded_end[-1] + jnp.arange(N_EXPERTS, dtype=I32) * MOE_BLOCK
    pad_start = jnp.concatenate([base + counts, tail])
    pad_start = (jnp.minimum(pad_start, cap - ZERO_BURST) // SUBLANES * SUBLANES).astype(I32)
    xs = _dispatch_call(pad_start, dest_t, h2, cap)
    ys = _expert_call(block_expert, n_active, xs, w1, w3, w2)
    out = _combine_call(dest_t, x1.reshape(n, D_MODEL), gates, mods, ys)
    return out.reshape(bsz, SEQ, D_MODEL)


def _rope_tables():
    rows = SEQ // GRID_W
    row = jnp.repeat(jnp.arange(rows), GRID_W).astype(F32)
    col = jnp.tile(jnp.arange(GRID_W), rows).astype(F32)
    n_freq = MLA_ROPE // 4
    inv = ROPE_BASE ** (-jnp.arange(n_freq, dtype=F32) / n_freq)
    ang = jnp.concatenate([row[:, None] * inv, col[:, None] * inv], axis=-1)
    cos, sin = jnp.cos(ang), jnp.sin(ang)
    ones = jnp.ones((SEQ, MLA_NOPE), F32)
    pad = jnp.zeros((SEQ, LANES - MLA_QK), F32)
    zn = jnp.zeros((SEQ, MLA_NOPE), F32)
    cos_t = jnp.concatenate([ones, cos, cos, pad], axis=-1)
    sin_t = jnp.concatenate([zn, -sin, sin, pad], axis=-1)
    ident = jnp.concatenate([jnp.ones((CTX_LEN, MLA_QK), F32), jnp.zeros((CTX_LEN, LANES - MLA_QK), F32)], axis=-1)
    zero = jnp.zeros((CTX_LEN, LANES), F32)
    return jnp.concatenate([cos_t, ident], axis=0), jnp.concatenate([sin_t, zero], axis=0)


def _rope_copy_pad(a):
    first_half = a[..., MLA_NOPE:MLA_NOPE + ROPE_HALF]
    zeros = jnp.zeros(a.shape[:-1] + (LANES - MLA_QK - ROPE_HALF,), a.dtype)
    return jnp.concatenate([a, first_half, zeros], axis=-1)


def _gate_cols(ga, d):
    h = MLSTM_HEADS
    return jnp.concatenate([ga[..., (2 + d) * h:(3 + d) * h], ga[..., d * h:(d + 1) * h]], axis=-1)


def _relayout_w_in(w_in):
    zeros = lambda n: jnp.zeros((D_MODEL, n), F32)
    w = MLSTM_WIDTH
    ga = w_in[:, OFF_GA:OFF_CQ]
    gate_tile = lambda d: jnp.concatenate([_gate_cols(ga, d), zeros(LANES - GATE_ROWS)], axis=-1)
    main = jnp.concatenate([
        w_in[:, OFF_CQ:OFF_KR],
        _rope_copy_pad(jnp.concatenate([zeros(MLA_NOPE), w_in[:, OFF_KR:OFF_GM]], axis=-1)),
        w_in[:, OFF_GM:IN_COLS],
        w_in[:, 0:w], w_in[:, 2 * w:3 * w],
        w_in[:, 3 * w:4 * w],
        gate_tile(0), gate_tile(1)], axis=-1)
    keys_t = (w_in[:, w:2 * w] * (MLSTM_HEAD_DIM ** -0.5)).T
    gates_t = jnp.concatenate([_gate_cols(ga, 0), _gate_cols(ga, 1)], axis=-1).T
    return main.astype(BF16), keys_t.astype(BF16), gates_t.astype(BF16)


def _relayout_gate_b(gb):
    per_dir = jnp.stack([_gate_cols(gb, 0), _gate_cols(gb, 1)])
    col_form = jnp.pad(per_dir, ((0, 0), (0, LANES - GATE_ROWS)))[:, None, :]
    row_form = jnp.broadcast_to(per_dir[:, :, None], (2, GATE_ROWS, LANES))
    return col_form, row_form


def _layer_params(l, w_in, w_out, mlstm_gate_b, mlstm_norm_g, mla_cq_g, mla_ckv_g, mla_w_uq, mla_w_ukv,
                  mla_q_g, mla_k_g, gmlp_ln_g, gmlp_ln_b, gmlp_w_s, gmlp_b_s):
    pad1 = lambda g: _rope_copy_pad(g)[None, :]
    w_uq = _rope_copy_pad(mla_w_uq[l].reshape(MLA_Q_RANK, MLA_HEADS, MLA_QK)).reshape(MLA_Q_RANK, -1)
    return dict(
        w_in=_relayout_w_in(w_in[l]),
        gate_b=_relayout_gate_b(mlstm_gate_b[l]),
        mlstm_g=mlstm_norm_g[l][None, :],
        cq_g=mla_cq_g[l][None, :], ckv_g=mla_ckv_g[l][None, :],
        w_uq=w_uq.astype(BF16),
        w_kv=mla_w_ukv[l].astype(BF16),
        q_g=pad1(mla_q_g[l]), k_g=pad1(mla_k_g[l]),
        ln_g=gmlp_ln_g[l][None, :], ln_b=gmlp_ln_b[l][None, :],
        w_s=gmlp_w_s[l].reshape(GMLP_GROUPS * CHUNK, CHUNK).astype(BF16),
        b_s=jnp.repeat(gmlp_b_s[l].T, GMLP_GROUP_DIM, axis=1),
        w_out=w_out[l].astype(BF16),
    )


def kernel(x, c, ctx, c_ctx, ada_w, ada_b, norm1_g, norm2_g, w_in, w_out, mlstm_gate_b, mlstm_norm_g, mla_cq_g, mla_ckv_g, mla_w_uq, mla_w_ukv, mla_q_g, mla_k_g, gmlp_ln_g, gmlp_ln_b, gmlp_w_s, gmlp_b_s, ffn_w1, ffn_w3, ffn_w2, moe_router_w, moe_router_b, moe_w1, moe_w3, moe_w2):
    bsz = x.shape[0]
    assert x.shape[1:] == (SEQ, D_MODEL) and ctx.shape[1:] == (CTX_LEN, D_MODEL)
    mod_rows = -(-(bsz + 1) // 8) * 8
    cvec = jnp.zeros((mod_rows, D_MODEL), F32).at[:bsz].set(c).at[bsz].set(c_ctx)
    mods_all = _ada_call(cvec, ada_w.astype(BF16), ada_b[:, None, :]).reshape(DEPTH, mod_rows, 6, D_MODEL)
    rope_tabs = _rope_tables()
    xall = (x, ctx) if DEPTH > 1 else jnp.concatenate([x, ctx], axis=1)
    moe_bf16 = {}
    for l in range(DEPTH):
        last = l == DEPTH - 1
        p = _layer_params(l, w_in, w_out, mlstm_gate_b, mlstm_norm_g, mla_cq_g, mla_ckv_g, mla_w_uq, mla_w_ukv,
                          mla_q_g, mla_k_g, gmlp_ln_g, gmlp_ln_b, gmlp_w_s, gmlp_b_s)
        mods = mods_all[l]
        z, zgc, zkt, zgt = _inproj_call(xall, mods, norm1_g[l][None, :], *p["w_in"], bsz)
        cast_ws = ()
        if (l + 1) % 2 == 1 and l + 1 < DEPTH:
            jn = (l + 1) // 2
            cast_ws = (moe_w1[jn].reshape(-1, D_FF_EXPERT), moe_w3[jn].reshape(-1, D_FF_EXPERT),
                       moe_w2[jn].reshape(-1, D_MODEL))
        a_mix, *cast = _mlstm_call(z, zgc, zkt, zgt, *p["gate_b"], p["mlstm_g"], cast_ws)
        if cast:
            moe_bf16[(l + 1) // 2] = (cast[0].reshape(N_EXPERTS, D_MODEL, D_FF_EXPERT),
                                      cast[1].reshape(N_EXPERTS, D_MODEL, D_FF_EXPERT),
                                      cast[2].reshape(N_EXPERTS, D_FF_EXPERT, D_MODEL))
        q, k, v = _qkv_call(z, p, rope_tabs)
        t_out = SEQ if last else T_ALL
        b_mix = _attn_call(q, k, v, t_out)
        x1 = _mixout_call(xall, z, a_mix, b_mix, mods, p, bsz, t_out, 1024 if last else 768)
        j = l // 2
        if l % 2 == 0:
            assert not last
            xall = _ffn_call(x1, mods, norm2_g[l][None, :], ffn_w1[j].astype(BF16), ffn_w3[j].astype(BF16),
                             ffn_w2[j].astype(BF16), bsz)
        else:
            assert last
            if j not in moe_bf16:
                moe_bf16[j] = (moe_w1[j].astype(BF16), moe_w3[j].astype(BF16), moe_w2[j].astype(BF16))
            xall = _moe_layer(x1, mods, norm2_g[l][None, :], moe_router_w[j], moe_router_b[j], *moe_bf16[j])
    return xall
```

```python
import functools

import jax
import jax.numpy as jnp
from jax import lax
from jax.experimental import pallas as pl
from jax.experimental.pallas import tpu as pltpu

F32 = jnp.float32
BF16 = jnp.bfloat16
I32 = jnp.int32

D_MODEL = 1024
SEQ = 2048
CTX_LEN = 256
T_ALL = SEQ + CTX_LEN
DEPTH = 2
GRID_W = 64
EPS = 1e-6
MLSTM_HEADS = 4
MLSTM_HEAD_DIM = 64
MLSTM_WIDTH = 256
CHUNK = 128
MLA_HEADS = 8
MLA_Q_RANK = 256
MLA_KV_RANK = 128
MLA_NOPE = 64
MLA_ROPE = 32
MLA_V = 64
MLA_QK = 96
MLA_WIDTH = 512
ROPE_BASE = 10000.0
GMLP_GROUPS = 4
GMLP_GROUP_DIM = 64
GMLP_WIDTH = 256
D_FF = 2816
N_EXPERTS = 8
TOP_K = 2
D_FF_EXPERT = 3584
MOE_BLOCK = 512
OFF_GA = 4 * MLSTM_WIDTH
OFF_CQ = OFF_GA + 4 * MLSTM_HEADS
OFF_CKV = OFF_CQ + MLA_Q_RANK
OFF_KR = OFF_CKV + MLA_KV_RANK
OFF_GM = OFF_KR + MLA_ROPE
IN_COLS = OFF_GM + 2 * GMLP_WIDTH

LANES = 128
HEAD_PAD = LANES
ZC_B = 0
ZC_GM = 512
ZC_QV = 1024
ZC_O = 1536
ZC_G = 1792
Z_COLS = ZC_G + 2 * LANES
GATE_ROWS = 8
VMEM_LIMIT = 56 * 1024 * 1024

N_CHUNKS = T_ALL // CHUNK
N_LAT_CHUNKS = SEQ // CHUNK


def _cparams(sem, vmem=VMEM_LIMIT):
    return pltpu.CompilerParams(dimension_semantics=sem, vmem_limit_bytes=vmem)


def _rms(x, g):
    return x * lax.rsqrt(jnp.mean(x * x, axis=-1, keepdims=True) + EPS) * g


def _silu(x):
    return x * jax.nn.sigmoid(x)


def _modulated_norm(x, g, shift, scale):
    return _rms(x, g) * (1.0 + scale) + shift


def _row_mods(ml_ref, mc_ref, first, tile_start, rows):
    row = tile_start + lax.broadcasted_iota(I32, (rows, 1), 0)
    is_ctx = row >= SEQ
    return tuple(jnp.where(is_ctx, mc_ref[0, first + k:first + k + 1, :], ml_ref[0, first + k:first + k + 1, :])
                 for k in range(3))


def _ada_kernel(c_ref, w_ref, b_ref, o_ref):
    s = _silu(c_ref[...]).astype(BF16)
    o_ref[0] = jnp.dot(s, w_ref[0], preferred_element_type=F32) + b_ref[0]


def _ada_call(cvec, ada_w, ada_b):
    rows = cvec.shape[0]
    return pl.pallas_call(
        _ada_kernel,
        grid=(DEPTH, 6),
        in_specs=[pl.BlockSpec((rows, D_MODEL), lambda l, j: (0, 0)),
                  pl.BlockSpec((1, D_MODEL, D_MODEL), lambda l, j: (l, 0, j)),
                  pl.BlockSpec((1, 1, D_MODEL), lambda l, j: (l, 0, j))],
        out_specs=pl.BlockSpec((1, rows, D_MODEL), lambda l, j: (l, 0, j)),
        out_shape=jax.ShapeDtypeStruct((DEPTH, rows, 6 * D_MODEL), F32),
        compiler_params=_cparams(("arbitrary", "arbitrary")),
        name="adaln",
    )(cvec, ada_w, ada_b)


def _stream_specs(src, tr):
    if not isinstance(src, tuple):
        return [pl.BlockSpec((1, tr, D_MODEL), lambda b, i: (b, i, 0))], [src]
    x, ctx = src
    n_full = SEQ // tr
    tail = SEQ - n_full * tr
    assert tail + CTX_LEN == tr and (n_full * tr) % tail == 0
    return ([pl.BlockSpec((1, tr, D_MODEL), lambda b, i: (b, jnp.minimum(i, n_full - 1), 0)),
             pl.BlockSpec((1, tail, D_MODEL), lambda b, i: (b, n_full * tr // tail, 0)),
             pl.BlockSpec((1, CTX_LEN, D_MODEL), lambda b, i: (b, 0, 0))], [x, x, ctx])


def _stream_tile(src_refs, i, tr):
    if len(src_refs) == 1:
        return src_refs[0][0]
    full, tail, ctx = src_refs
    mixed = jnp.concatenate([tail[0], ctx[0]], axis=0)
    return jnp.where(i == SEQ // tr, mixed, full[0])


def _inproj_kernel(*refs, tr, n_src):
    src_refs = refs[:n_src]
    ml_ref, mc_ref, g_ref, w_ref, wk_ref, wg_ref, z_ref, zgc_ref, zk_ref, zg_ref = refs[n_src:]
    shift, scale, _ = _row_mods(ml_ref, mc_ref, 0, pl.program_id(1) * tr, tr)
    x = _stream_tile(src_refs, pl.program_id(1), tr)
    xn = _modulated_norm(x, g_ref[...], shift, scale).astype(BF16)
    z = jnp.dot(xn, w_ref[...], preferred_element_type=F32)
    z_ref[0] = z[:, 0:ZC_G].astype(z_ref.dtype)
    zgc_ref[0] = z[:, ZC_G:Z_COLS]
    nt = (((1,), (1,)), ((), ()))
    zk_ref[0] = lax.dot_general(wk_ref[...], xn, nt, preferred_element_type=F32).astype(zk_ref.dtype)
    zg_ref[0] = lax.dot_general(wg_ref[...], xn, nt, preferred_element_type=F32)


def _inproj_call(src, mods, norm_g, w_main, w_kt, w_gt, n_ctx_row):
    tr = 768
    src_specs, src_ops = _stream_specs(src, tr)
    bsz = src_ops[0].shape[0]
    const = lambda b, i: (0, 0)
    return pl.pallas_call(
        functools.partial(_inproj_kernel, tr=tr, n_src=len(src_ops)),
        grid=(bsz, T_ALL // tr),
        in_specs=src_specs + [
                  pl.BlockSpec((1, 6, D_MODEL), lambda b, i: (b, 0, 0)),
                  pl.BlockSpec((1, 6, D_MODEL), lambda b, i: (n_ctx_row, 0, 0)),
                  pl.BlockSpec((1, D_MODEL), const),
                  pl.BlockSpec((D_MODEL, Z_COLS), const),
                  pl.BlockSpec((MLSTM_WIDTH, D_MODEL), const),
                  pl.BlockSpec((2 * GATE_ROWS, D_MODEL), const)],
        out_specs=[pl.BlockSpec((1, tr, ZC_G), lambda b, i: (b, i, 0)),
                   pl.BlockSpec((1, tr, Z_COLS - ZC_G), lambda b, i: (b, i, 0)),
                   pl.BlockSpec((1, MLSTM_WIDTH, tr), lambda b, i: (b, 0, i)),
                   pl.BlockSpec((1, 2 * GATE_ROWS, tr), lambda b, i: (b, 0, i))],
        out_shape=[jax.ShapeDtypeStruct((bsz, T_ALL, ZC_G), BF16),
                   jax.ShapeDtypeStruct((bsz, T_ALL, Z_COLS - ZC_G), F32),
                   jax.ShapeDtypeStruct((bsz, MLSTM_WIDTH, T_ALL), BF16),
                   jax.ShapeDtypeStruct((bsz, 2 * GATE_ROWS, T_ALL), F32)],
        compiler_params=_cparams(("parallel", "parallel")),
        name="in_proj",
    )(*src_ops, mods, mods, norm_g, w_main, w_kt, w_gt)


def _scan_chunk(d, j):
    fwd_chunk = (j + N_LAT_CHUNKS) % N_CHUNKS
    return jnp.where(d == 0, fwd_chunk, N_CHUNKS - 1 - j)


MLSTM_ROWS_PER_STEP = 8


def _log_sigmoid(x):
    return jnp.minimum(x, 0.0) - jnp.log1p(jnp.exp(-jnp.abs(x)))


def _bf16_head(v):
    bits = lax.bitcast_convert_type(v, jnp.uint32) & jnp.uint32(0xFFFF0000)
    return lax.bitcast_convert_type(bits, F32)


def _bf16_terms(x, axis):
    hi = _bf16_head(x)
    rest = x - hi
    mid = _bf16_head(rest)
    lo = rest - mid
    return jnp.concatenate([hi, mid, lo], axis=axis).astype(BF16)


def _mlstm_kernel(*refs, n_cast):
    ins, rest = refs[:8], refs[8:]
    cast_in, o_ref, cast_out, scratch = rest[:n_cast], rest[n_cast], rest[n_cast + 1:2 * n_cast + 1], rest[2 * n_cast + 1:]
    for src, dst in zip(cast_in, cast_out):
        dst[...] = src[...].astype(dst.dtype)
    _mlstm_step(*ins, o_ref, *scratch)


def _mlstm_step(qv_ref, og_ref, zg_ref, kt_ref, gt_ref, gb_ref, gbt_ref, ng_ref, o_ref, c_ref, m_ref, hf_ref):
    L = CHUNK
    E = MLSTM_HEAD_DIM
    d = pl.program_id(1)
    j = pl.program_id(2)
    fwd = d == 0
    row0 = pl.multiple_of(_scan_chunk(d, j) * L, L)

    @pl.when(j == 0)
    def _():
        c_ref[...] = jnp.zeros_like(c_ref)
        m_ref[...] = jnp.zeros_like(m_ref)

    r_i = lax.broadcasted_iota(I32, (L, L), 0)
    c_i = lax.broadcasted_iota(I32, (L, L), 1)
    prec = jnp.logical_or(jnp.logical_and(fwd, c_i <= r_i), jnp.logical_and(jnp.logical_not(fwd), c_i >= r_i))
    succ = jnp.logical_or(jnp.logical_and(fwd, r_i <= c_i), jnp.logical_and(jnp.logical_not(fwd), r_i >= c_i))
    lane = lax.broadcasted_iota(I32, (L, LANES), 1)
    sub = lax.broadcasted_iota(I32, (LANES, L), 0)
    low_lane = lane < E
    low_sub = sub < E

    row_idx = lax.broadcasted_iota(I32, (L, LANES), 0)
    rows = range(qv_ref.shape[0])
    pairs = range(MLSTM_HEADS // 2)
    h_pairs = {}
    for bb in rows:
        gates_c = zg_ref[bb] + gb_ref[0]
        gates_r = gt_ref[bb] + gbt_ref[0]
        logf_r = _log_sigmoid(gates_r)
        logf_c = jnp.concatenate([logf_r, jnp.zeros((LANES - GATE_ROWS, L), F32)], axis=0).T
        parts_c = jnp.dot(prec.astype(BF16), _bf16_terms(logf_c, 1), preferred_element_type=F32)
        b_c = parts_c[:, 0:LANES] + parts_c[:, LANES:2 * LANES] + parts_c[:, 2 * LANES:3 * LANES]
        c_c = pltpu.roll(gates_c, LANES - MLSTM_HEADS, 1) - b_c
        m_row = m_ref[bb]
        cm = c_c
        for sh in (1, 2, 4, 8, 16, 32, 64):
            prev = jnp.where(fwd, jnp.where(row_idx >= sh, pltpu.roll(cm, sh, 0), -jnp.inf),
                             jnp.where(row_idx < L - sh, pltpu.roll(cm, L - sh, 0), -jnp.inf))
            cm = jnp.maximum(cm, prev)
        m_c = jnp.maximum(cm, m_row)
        a_c = jnp.exp(m_row - m_c)
        floor_c = jnp.exp(-(b_c + m_c))
        m_last = jnp.maximum(jnp.max(c_c, axis=0, keepdims=True), m_row)
        b_last = jnp.where(fwd, b_c[L - 1:L, :], b_c[0:1, :])
        m_ref[bb] = b_last + m_last
        decay_row = jnp.exp(m_row - m_last)
        parts_r = jnp.dot(_bf16_terms(logf_r, 0), succ.astype(BF16), preferred_element_type=F32)
        b_r = parts_r[0:GATE_ROWS] + parts_r[GATE_ROWS:2 * GATE_ROWS] + parts_r[2 * GATE_ROWS:3 * GATE_ROWS]
        c_r = gates_r[MLSTM_HEADS:2 * MLSTM_HEADS, :] - b_r[0:MLSTM_HEADS, :]
        w_r = [jnp.exp(c_r[h:h + 1, :] - m_last[:, h:h + 1]) for h in range(MLSTM_HEADS)]

        for pj in pairs:
            tile = slice(pj * LANES, (pj + 1) * LANES)
            q_t = qv_ref[bb, :, tile]
            v_t = qv_ref[bb, :, MLSTM_WIDTH + pj * LANES:MLSTM_WIDTH + (pj + 1) * LANES]
            kt_t = kt_ref[bb, tile, :]
            c_old = c_ref[bb, pj]
            q_both = jnp.concatenate([jnp.where(low_lane, q_t, 0.0), jnp.where(low_lane, 0.0, q_t)],
                                     axis=0).astype(BF16)
            kc = jnp.concatenate([kt_t.astype(BF16), c_old.astype(BF16)], axis=1)
            qk_qc = jnp.dot(q_both, kc, preferred_element_type=F32)
            hx = []
            upd = []
            for par in range(2):
                h = 2 * pj + par
                own_lane = low_lane if par == 0 else jnp.logical_not(low_lane)
                own_sub = low_sub if par == 0 else jnp.logical_not(low_sub)
                den_lane = E if par == 0 else 0
                v_ext = jnp.where(own_lane, v_t, jnp.where(lane == den_lane, 1.0, 0.0)).astype(BF16)
                sqk = qk_qc[par * L:(par + 1) * L, 0:L]
                q_c = qk_qc[par * L:(par + 1) * L, L:L + LANES]
                dmat = jnp.where(prec, jnp.exp(c_r[h:h + 1, :] - m_c[:, h:h + 1]), 0.0)
                kw_t = jnp.where(own_sub, kt_t, 0.0) * w_r[h]
                pv_upd = jnp.dot(jnp.concatenate([sqk * dmat, kw_t], axis=0).astype(BF16), v_ext,
                                 preferred_element_type=F32)
                nd = pv_upd[0:L] + a_c[:, h:h + 1] * q_c
                den = nd[:, den_lane:den_lane + 1]
                hx.append(nd / jnp.maximum(jnp.abs(den), floor_c[:, h:h + 1]))
                upd.append(pv_upd[L:L + LANES])
            decay = jnp.where(low_sub, decay_row[:, 2 * pj:2 * pj + 1], decay_row[:, 2 * pj + 1:2 * pj + 2])
            c_ref[bb, pj] = decay * c_old + upd[0] + upd[1]
            h_pairs[bb, pj] = jnp.where(low_lane, hx[0], hx[1])

    @pl.when(fwd)
    def _():
        for (bb, pj), h_pair in h_pairs.items():
            hf_ref[bb, pl.ds(row0, L), pj * LANES:(pj + 1) * LANES] = h_pair

    @pl.when(jnp.logical_not(fwd))
    def _():
        for (bb, pj), h_pair in h_pairs.items():
            tile = slice(pj * LANES, (pj + 1) * LANES)
            hs = hf_ref[bb, pl.ds(row0, L), tile] + h_pair
            sq = hs * hs
            ss = jnp.where(low_lane,
                           jnp.sum(jnp.where(low_lane, sq, 0.0), axis=-1, keepdims=True),
                           jnp.sum(jnp.where(low_lane, 0.0, sq), axis=-1, keepdims=True))
            y = hs * lax.rsqrt(ss * (1.0 / E) + EPS) * ng_ref[:, tile]
            o_ref[bb, :, tile] = (jax.nn.sigmoid(og_ref[bb, :, tile].astype(F32)) * y).astype(o_ref.dtype)


def _mlstm_call(z, zgc, zkt, zgt, gate_b, gate_bt, norm_g, cast_ws=()):
    bsz = z.shape[0]
    nb = max(r for r in range(1, MLSTM_ROWS_PER_STEP + 1) if bsz % r == 0)
    last = N_CHUNKS - 1
    chunk = lambda b, d, j: _scan_chunk(d, j)
    steps = (bsz // nb) * 2 * N_CHUNKS
    n_blk = 1 << (steps.bit_length() - 1)
    cast_idx = lambda b, d, j: (jnp.minimum((b * 2 + d) * N_CHUNKS + j, n_blk - 1), 0)
    cast_specs = [pl.BlockSpec((w.shape[0] // n_blk, w.shape[1]), cast_idx) for w in cast_ws]
    assert all(w.shape[0] % (16 * n_blk) == 0 for w in cast_ws)
    return pl.pallas_call(
        functools.partial(_mlstm_kernel, n_cast=len(cast_ws)),
        grid=(bsz // nb, 2, N_CHUNKS),
        in_specs=[pl.BlockSpec((nb, CHUNK, 2 * MLSTM_WIDTH), lambda b, d, j: (b, chunk(b, d, j), ZC_QV // 512)),
                  pl.BlockSpec((nb, CHUNK, MLSTM_WIDTH), lambda b, d, j: (b, chunk(b, d, j), ZC_O // MLSTM_WIDTH)),
                  pl.BlockSpec((nb, CHUNK, LANES), lambda b, d, j: (b, chunk(b, d, j), d)),
                  pl.BlockSpec((nb, MLSTM_WIDTH, CHUNK), lambda b, d, j: (b, 0, chunk(b, d, j))),
                  pl.BlockSpec((nb, GATE_ROWS, CHUNK), lambda b, d, j: (b, d, chunk(b, d, j))),
                  pl.BlockSpec((1, 1, LANES), lambda b, d, j: (d, 0, 0)),
                  pl.BlockSpec((1, GATE_ROWS, LANES), lambda b, d, j: (d, 0, 0)),
                  pl.BlockSpec((1, MLSTM_WIDTH), lambda b, d, j: (0, 0))] + cast_specs,
        out_specs=[pl.BlockSpec((nb, CHUNK, MLSTM_WIDTH),
                                lambda b, d, j: (b, jnp.where(d == 0, last, last - j), 0))] + cast_specs,
        out_shape=[jax.ShapeDtypeStruct((bsz, T_ALL, MLSTM_WIDTH), BF16)]
        + [jax.ShapeDtypeStruct(w.shape, BF16) for w in cast_ws],
        scratch_shapes=[pltpu.VMEM((nb, MLSTM_HEADS // 2, LANES, LANES), F32),
                        pltpu.VMEM((nb, 1, LANES), F32),
                        pltpu.VMEM((nb, T_ALL, MLSTM_WIDTH), F32)],
        compiler_params=_cparams(("arbitrary", "arbitrary", "arbitrary")),
        name="mlstm",
    )(z, z, zgc, zkt, zgt, gate_b, gate_bt, norm_g, *cast_ws)


ROPE_HALF = MLA_ROPE // 2


def _rope(t, cos_t, sin_t):
    return t * cos_t + pltpu.roll(t, LANES - ROPE_HALF, 1) * sin_t


HEAD_PAIR = 2 * HEAD_PAD
LOG2_E = 1.4426950408889634
ATT_Q_SCALE = MLA_QK ** -0.5 * LOG2_E
ATT_DEN_LANE = 0
assert MLA_NOPE + MLA_V == HEAD_PAD


def _head_rms_scale(raw, ones_ref):
    sq = (raw * raw).astype(BF16)
    ss = jnp.concatenate([jnp.dot(sq[:, j * HEAD_PAIR:(j + 1) * HEAD_PAIR], ones_ref[...],
                                  preferred_element_type=F32) for j in range(MLA_HEADS // 2)], axis=-1)
    return lax.rsqrt(ss * (1.0 / MLA_QK) + EPS)


def _qkv_kernel(z_ref, cqg_ref, ckvg_ref, wq_ref, wkv_ref, ones_ref, qc_ref, qs_ref, kc_ref, ks_ref,
                q_ref, k_ref, v_ref, *, tr):
    cqn = _rms(z_ref[0, :, 0:MLA_Q_RANK].astype(F32), cqg_ref[...]).astype(BF16)
    ckvn = _rms(z_ref[0, :, MLA_Q_RANK:MLA_Q_RANK + MLA_KV_RANK].astype(F32), ckvg_ref[...]).astype(BF16)
    k_rope = z_ref[0, :, MLA_Q_RANK + MLA_KV_RANK:MLA_Q_RANK + MLA_KV_RANK + LANES].astype(F32)
    q_raw = jnp.dot(cqn, wq_ref[...], preferred_element_type=F32)
    kv_raw = jnp.dot(ckvn, wkv_ref[...], preferred_element_type=F32)
    lane_all = lax.broadcasted_iota(I32, (tr, MLA_HEADS * HEAD_PAD), 1) % HEAD_PAD
    k_raw = jnp.where(lane_all < MLA_NOPE, kv_raw, jnp.concatenate([k_rope] * MLA_HEADS, axis=-1))
    q_n = q_raw * _head_rms_scale(q_raw, ones_ref)
    k_n = k_raw * _head_rms_scale(k_raw, ones_ref)
    q_cos, q_sin, k_cos, k_sin = qc_ref[...], qs_ref[...], kc_ref[...], ks_ref[...]
    lane = lax.broadcasted_iota(I32, (tr, LANES), 1)
    for h in range(MLA_HEADS):
        sl = slice(h * HEAD_PAD, (h + 1) * HEAD_PAD)
        q_ref[0, :, sl] = _rope(q_n[:, sl], q_cos, q_sin).astype(BF16)
        k_ref[0, :, sl] = _rope(k_n[:, sl], k_cos, k_sin).astype(BF16)
        v_ref[0, :, sl] = jnp.where(lane >= MLA_NOPE, kv_raw[:, sl],
                                    jnp.where(lane == ATT_DEN_LANE, 1.0, 0.0)).astype(BF16)


def _qkv_call(z, p, rope_tabs):
    bsz = z.shape[0]
    tr = 768
    hw = MLA_HEADS * HEAD_PAD
    const = lambda b, i: (0, 0)
    out = jax.ShapeDtypeStruct((bsz, T_ALL, hw), BF16)
    tab = pl.BlockSpec((tr, LANES), lambda b, i: (i, 0))
    ospec = pl.BlockSpec((1, tr, hw), lambda b, i: (b, i, 0))
    return pl.pallas_call(
        functools.partial(_qkv_kernel, tr=tr),
        grid=(bsz, T_ALL // tr),
        in_specs=[pl.BlockSpec((1, tr, 512), lambda b, i: (b, i, ZC_B // 512)),
                  pl.BlockSpec((1, MLA_Q_RANK), const), pl.BlockSpec((1, MLA_KV_RANK), const),
                  pl.BlockSpec((MLA_Q_RANK, hw), const), pl.BlockSpec((MLA_KV_RANK, hw), const),
                  pl.BlockSpec((HEAD_PAIR, HEAD_PAIR), const),
                  tab, tab, tab, tab],
        out_specs=[ospec, ospec, ospec],
        out_shape=[out, out, out],
        compiler_params=_cparams(("parallel", "parallel")),
        name="mla_qkv",
    )(z, p["cq_g"], p["ckv_g"], p["w_uq"], p["w_kv"], _head_ones(),
      *_gained_rope_tables(rope_tabs, p["q_g"], ATT_Q_SCALE), *_gained_rope_tables(rope_tabs, p["k_g"], 1.0))


def _gained_rope_tables(rope_tabs, gain, scale):
    cos_t, sin_t = rope_tabs
    partner_gain = jnp.roll(gain, -ROPE_HALF, axis=1)
    return cos_t * (gain * scale), sin_t * (partner_gain * scale)


def _head_ones():
    r = jnp.arange(HEAD_PAIR)
    same_head = (r[:, None] // HEAD_PAD) == (r[None, :] // HEAD_PAD)
    real_row = (r[:, None] % HEAD_PAD) < MLA_QK
    return jnp.logical_and(same_head, real_row).astype(BF16)


ATT_TQ = 1024
assert SEQ % ATT_TQ == 0 and CTX_LEN <= ATT_TQ


def _attn_kernel(q_ref, k_ref, v_ref, o_ref):
    i = pl.program_id(1)

    def run(rows, k0):
        def scores(hh):
            sl = slice(hh * HEAD_PAD, (hh + 1) * HEAD_PAD)
            return lax.dot_general(q_ref[0, 0:rows, sl], k_ref[0, k0:T_ALL, sl], (((1,), (1,)), ((), ())),
                                   preferred_element_type=F32)

        s = scores(0)
        for hh in range(MLA_HEADS):
            s_next = scores(hh + 1) if hh + 1 < MLA_HEADS else None
            p = jnp.exp2(s - jnp.max(s, axis=-1, keepdims=True)).astype(BF16)
            oe = jnp.dot(p, v_ref[0, k0:T_ALL, hh * HEAD_PAD:(hh + 1) * HEAD_PAD], preferred_element_type=F32)
            o_ref[0, 0:rows, hh * MLA_V:(hh + 1) * MLA_V] = (
                oe[:, MLA_NOPE:MLA_NOPE + MLA_V] / oe[:, ATT_DEN_LANE:ATT_DEN_LANE + 1]).astype(o_ref.dtype)
            s = s_next

    @pl.when(i < SEQ // ATT_TQ)
    def _():
        run(ATT_TQ, 0)

    @pl.when(i >= SEQ // ATT_TQ)
    def _():
        run(CTX_LEN, SEQ)
        o_ref[0, CTX_LEN:ATT_TQ, :] = jnp.zeros((ATT_TQ - CTX_LEN, MLA_WIDTH), o_ref.dtype)


def _attn_call(q, k, v, t_out):
    bsz = q.shape[0]
    w = MLA_HEADS * HEAD_PAD
    return pl.pallas_call(
        _attn_kernel,
        grid=(bsz, pl.cdiv(t_out, ATT_TQ)),
        in_specs=[pl.BlockSpec((1, ATT_TQ, w), lambda b, i: (b, i, 0)),
                  pl.BlockSpec((1, T_ALL, w), lambda b, i: (b, 0, 0)),
                  pl.BlockSpec((1, T_ALL, w), lambda b, i: (b, 0, 0))],
        out_specs=pl.BlockSpec((1, ATT_TQ, MLA_WIDTH), lambda b, i: (b, i, 0)),
        out_shape=jax.ShapeDtypeStruct((bsz, t_out, MLA_WIDTH), BF16),
        compiler_params=_cparams(("parallel", "arbitrary")),
        name="mla_attn",
    )(q, k, v)


def _mixout_kernel(*refs, tr, n_src):
    src_refs = refs[:n_src]
    zg_ref, a_ref, b_ref, ml_ref, mc_ref, lng_ref, lnb_ref, ws_ref, bs_ref, wo_ref, o_ref, cm_ref = refs[n_src:]
    gd = GMLP_GROUP_DIM
    act = jax.nn.gelu(zg_ref[0].astype(F32))
    u = act[:, 0:GMLP_WIDTH]
    vv = act[:, GMLP_WIDTH:2 * GMLP_WIDTH]
    mu = jnp.mean(vv, axis=-1, keepdims=True)
    var = jnp.mean(jnp.square(vv - mu), axis=-1, keepdims=True)
    vn = ((vv - mu) * lax.rsqrt(var + EPS) * lng_ref[...] + lnb_ref[...]).astype(BF16)
    col_group = lax.broadcasted_iota(I32, (CHUNK, GMLP_WIDTH), 1) // gd
    for c in range(tr // CHUNK):
        rows = slice(c * CHUNK, (c + 1) * CHUNK)
        mixed = jnp.dot(ws_ref[...], vn[rows, :], preferred_element_type=F32)
        sv = mixed[0:CHUNK]
        for g in range(1, GMLP_GROUPS):
            sv = jnp.where(col_group == g, mixed[g * CHUNK:(g + 1) * CHUNK], sv)
        cm_ref[rows, :] = u[rows, :] * (sv + bs_ref[...])
    y = jnp.dot(a_ref[0].astype(BF16), wo_ref[0:MLSTM_WIDTH, :], preferred_element_type=F32)
    y += jnp.dot(b_ref[0].astype(BF16), wo_ref[MLSTM_WIDTH:MLSTM_WIDTH + MLA_WIDTH, :],
                 preferred_element_type=F32)
    y += jnp.dot(cm_ref[...].astype(BF16), wo_ref[MLSTM_WIDTH + MLA_WIDTH:, :], preferred_element_type=F32)
    _, _, gate = _row_mods(ml_ref, mc_ref, 0, pl.program_id(1) * tr, tr)
    o_ref[0] = _stream_tile(src_refs, pl.program_id(1), tr) + gate * y


def _mixout_call(src, z, a_mix, b_mix, mods, p, n_ctx_row, t_out, tr):
    src_specs, src_ops = _stream_specs(src, tr)
    bsz = src_ops[0].shape[0]
    const = lambda b, i: (0, 0)
    return pl.pallas_call(
        functools.partial(_mixout_kernel, tr=tr, n_src=len(src_ops)),
        grid=(bsz, t_out // tr),
        in_specs=src_specs + [
                  pl.BlockSpec((1, tr, 2 * GMLP_WIDTH), lambda b, i: (b, i, ZC_GM // (2 * GMLP_WIDTH))),
                  pl.BlockSpec((1, tr, MLSTM_WIDTH), lambda b, i: (b, i, 0)),
                  pl.BlockSpec((1, tr, MLA_WIDTH), lambda b, i: (b, i, 0)),
                  pl.BlockSpec((1, 6, D_MODEL), lambda b, i: (b, 0, 0)),
                  pl.BlockSpec((1, 6, D_MODEL), lambda b, i: (n_ctx_row, 0, 0)),
                  pl.BlockSpec((1, GMLP_WIDTH), const), pl.BlockSpec((1, GMLP_WIDTH), const),
                  pl.BlockSpec((GMLP_GROUPS * CHUNK, CHUNK), const),
                  pl.BlockSpec((CHUNK, GMLP_WIDTH), const),
                  pl.BlockSpec((D_MODEL, D_MODEL), const)],
        out_specs=pl.BlockSpec((1, tr, D_MODEL), lambda b, i: (b, i, 0)),
        out_shape=jax.ShapeDtypeStruct((bsz, t_out, D_MODEL), F32),
        scratch_shapes=[pltpu.VMEM((tr, GMLP_WIDTH), F32)],
        compiler_params=_cparams(("parallel", "parallel")),
        name="mix_out",
    )(*src_ops, z, a_mix, b_mix, mods, mods, p["ln_g"], p["ln_b"], p["w_s"], p["b_s"], p["w_out"])


FFN_SPLIT = 2


def _ffn_kernel(x_ref, ml_ref, mc_ref, g_ref, w1_ref, w3_ref, w2_ref, o_ref, *, tr):
    shift, scale, gate = _row_mods(ml_ref, mc_ref, 3, pl.program_id(1) * tr, tr)
    x = x_ref[0]
    h = _modulated_norm(x, g_ref[...], shift, scale).astype(BF16)
    fc = D_FF // FFN_SPLIT
    y = jnp.zeros((tr, D_MODEL), F32)
    for f in range(FFN_SPLIT):
        cols = slice(f * fc, (f + 1) * fc)
        h1 = jnp.dot(h, w1_ref[:, cols], preferred_element_type=F32)
        h3 = jnp.dot(h, w3_ref[:, cols], preferred_element_type=F32)
        y += jnp.dot((_silu(h1) * h3).astype(BF16), w2_ref[cols, :], preferred_element_type=F32)
    o_ref[0] = x + gate * y


def _ffn_call(x1, mods, norm_g, w1, w3, w2, n_ctx_row):
    bsz = x1.shape[0]
    tr = 768
    const = lambda b, i: (0, 0)
    resident = pl.Buffered(1)
    return pl.pallas_call(
        functools.partial(_ffn_kernel, tr=tr),
        grid=(bsz, T_ALL // tr),
        in_specs=[pl.BlockSpec((1, tr, D_MODEL), lambda b, i: (b, i, 0)),
                  pl.BlockSpec((1, 6, D_MODEL), lambda b, i: (b, 0, 0)),
                  pl.BlockSpec((1, 6, D_MODEL), lambda b, i: (n_ctx_row, 0, 0)),
                  pl.BlockSpec((1, D_MODEL), const),
                  pl.BlockSpec((D_MODEL, D_FF), const, pipeline_mode=resident),
                  pl.BlockSpec((D_MODEL, D_FF), const, pipeline_mode=resident),
                  pl.BlockSpec((D_FF, D_MODEL), const, pipeline_mode=resident)],
        out_specs=pl.BlockSpec((1, tr, D_MODEL), lambda b, i: (b, i, 0)),
        out_shape=jax.ShapeDtypeStruct((bsz, T_ALL, D_MODEL), F32),
        compiler_params=_cparams(("parallel", "parallel")),
        name="dense_ffn",
    )(x1, mods, mods, norm_g, w1, w3, w2)


ROUTE_TR = 1024


def _router_kernel(x_ref, ml_ref, g_ref, rw_ref, rb_ref, h_ref, e_ref, r_ref, gt_ref, cnt_ref, run_ref, *, tr):
    step = pl.program_id(0) * pl.num_programs(1) + pl.program_id(1)

    @pl.when(step == 0)
    def _():
        run_ref[...] = jnp.zeros_like(run_ref)

    h = _modulated_norm(x_ref[0], g_ref[...], ml_ref[0, 3:4, :], ml_ref[0, 4:5, :])
    h_ref[...] = h
    lane = lax.broadcasted_iota(I32, (tr, LANES), 1)
    h_hi = _bf16_head(h)
    h_lo = (h - h_hi).astype(BF16)
    w = rw_ref[...]
    w_hi = _bf16_head(w)
    w_lo = (w - w_hi).astype(BF16)
    first = jnp.dot(h_hi.astype(BF16), jnp.concatenate([w_hi, w_lo.astype(F32)], axis=1).astype(BF16),
                    preferred_element_type=F32)
    logits = (first[:, 0:LANES] + first[:, LANES:2 * LANES]
              + jnp.dot(h_lo, w_hi.astype(BF16), preferred_element_type=F32)) + rb_ref[...]
    logits = jnp.where(lane < N_EXPERTS, logits, -jnp.inf)
    m1 = jnp.max(logits, axis=-1, keepdims=True)
    e1 = jnp.min(jnp.where(logits == m1, lane, LANES), axis=-1, keepdims=True)
    rest = jnp.where(lane == e1, -jnp.inf, logits)
    m2 = jnp.max(rest, axis=-1, keepdims=True)
    e2 = jnp.min(jnp.where(rest == m2, lane, LANES), axis=-1, keepdims=True)
    ex = jnp.exp(m2 - m1)
    g1 = 1.0 / (1.0 + ex)
    g2 = ex / (1.0 + ex)
    onehot = jnp.logical_or(lane == e1, lane == e2 + N_EXPERTS)
    oh = onehot.astype(F32)
    r_i = lax.broadcasted_iota(I32, (tr, tr), 0)
    c_i = lax.broadcasted_iota(I32, (tr, tr), 1)
    before = jnp.dot((c_i < r_i).astype(BF16), oh.astype(BF16), preferred_element_type=F32)
    tot = jnp.sum(oh, axis=0, keepdims=True)
    tot0_shift = pltpu.roll(tot, N_EXPERTS, 1)
    run = run_ref[...]
    first_half = lax.broadcasted_iota(I32, (1, LANES), 1) < N_EXPERTS
    offs = run + jnp.where(first_half, 0.0, tot0_shift)
    ranks = oh * (before + offs)
    rank1 = jnp.sum(jnp.where(lane < N_EXPERTS, ranks, 0.0), axis=-1, keepdims=True)
    rank2 = jnp.sum(jnp.where(lane >= N_EXPERTS, ranks, 0.0), axis=-1, keepdims=True)
    col = lax.broadcasted_iota(I32, (tr, TOP_K), 1)
    e_ref[...] = jnp.where(col == 0, e1, e2)
    r_ref[...] = jnp.where(col == 0, rank1, rank2).astype(I32)
    gt_ref[...] = jnp.where(col == 0, g1, g2)
    both = tot + jnp.where(first_half, pltpu.roll(tot, LANES - N_EXPERTS, 1), tot0_shift)
    new_run = run + both
    run_ref[...] = new_run
    cnt_ref[...] = new_run.astype(I32)


def _router_call(x1, mods, norm_g, rw_p, rb_p):
    bsz = x1.shape[0]
    tr = ROUTE_TR
    n = bsz * SEQ
    nt = SEQ // tr
    const = lambda b, i: (0, 0)
    tok = lambda b, i: (b * nt + i, 0)
    return pl.pallas_call(
        functools.partial(_router_kernel, tr=tr),
        grid=(bsz, nt),
        in_specs=[pl.BlockSpec((1, tr, D_MODEL), lambda b, i: (b, i, 0)),
                  pl.BlockSpec((1, 6, D_MODEL), lambda b, i: (b, 0, 0)),
                  pl.BlockSpec((1, D_MODEL), const),
                  pl.BlockSpec((D_MODEL, LANES), const),
                  pl.BlockSpec((1, LANES), const)],
        out_specs=[pl.BlockSpec((tr, D_MODEL), tok),
                   pl.BlockSpec((tr, TOP_K), tok), pl.BlockSpec((tr, TOP_K), tok), pl.BlockSpec((tr, TOP_K), tok),
                   pl.BlockSpec((1, LANES), const)],
        out_shape=[jax.ShapeDtypeStruct((n, D_MODEL), F32),
                   jax.ShapeDtypeStruct((n, TOP_K), I32), jax.ShapeDtypeStruct((n, TOP_K), I32),
                   jax.ShapeDtypeStruct((n, TOP_K), F32),
                   jax.ShapeDtypeStruct((1, LANES), I32)],
        scratch_shapes=[pltpu.VMEM((1, LANES), F32)],
        compiler_params=_cparams(("arbitrary", "arbitrary")),
        name="moe_router",
    )(x1, mods, norm_g, rw_p, rb_p)


DISPATCH_TD = 1024
SUBLANES = 8
ZERO_BURST = MOE_BLOCK + SUBLANES
N_ZERO_BURSTS = 2 * N_EXPERTS
DMA_ISSUE_UNROLL = 8


def _row_copy(src, src_row, dst, dst_row, sem):
    return pltpu.make_async_copy(src.at[pl.ds(src_row, 1)], dst.at[pl.ds(dst_row, 1)], sem)


def _dispatch_kernel(pad_ref, dest_ref, h_ref, xs_out, zero_ref, sem):
    @pl.when(pl.program_id(0) == 0)
    def _():
        zero_ref[...] = jnp.zeros_like(zero_ref)
        for e in range(N_ZERO_BURSTS):
            start = pl.multiple_of(pad_ref[e], SUBLANES)
            burst = pltpu.make_async_copy(zero_ref, xs_out.at[pl.ds(start, ZERO_BURST)], sem)
            burst.start()
            burst.wait()

    def issue(r, carry):
        for kk in range(TOP_K):
            _row_copy(h_ref, r, xs_out, dest_ref[kk, r], sem).start()
        return carry

    lax.fori_loop(0, DISPATCH_TD, issue, 0, unroll=DMA_ISSUE_UNROLL)

    for kk in range(TOP_K):
        pltpu.make_async_copy(h_ref, xs_out.at[pl.ds(0, DISPATCH_TD)], sem).wait()


def _dispatch_call(pad_start, dest_t, h2, cap):
    n = h2.shape[0]
    grid_spec = pltpu.PrefetchScalarGridSpec(
        num_scalar_prefetch=1,
        grid=(n // DISPATCH_TD,),
        in_specs=[pl.BlockSpec((TOP_K, DISPATCH_TD), lambda i, pad: (0, i), memory_space=pltpu.SMEM),
                  pl.BlockSpec((DISPATCH_TD, D_MODEL), lambda i, pad: (i, 0))],
        out_specs=pl.BlockSpec(memory_space=pl.ANY),
        scratch_shapes=[pltpu.VMEM((ZERO_BURST, D_MODEL), F32), pltpu.SemaphoreType.DMA(())],
    )
    return pl.pallas_call(
        _dispatch_kernel,
        grid_spec=grid_spec,
        out_shape=jax.ShapeDtypeStruct((cap, D_MODEL), F32),
        compiler_params=_cparams(("arbitrary",)),
        name="moe_dispatch",
    )(pad_start, dest_t, h2)


EXPERT_FSPLIT = 2


def _expert_kernel(be_ref, na_ref, xs_ref, w1_ref, w3_ref, w2_ref, ys_ref):
    i = pl.program_id(0)
    f = pl.program_id(1)
    active = i < na_ref[0]

    @pl.when(jnp.logical_and(active, f == 0))
    def _():
        ys_ref[...] = jnp.zeros_like(ys_ref)

    @pl.when(active)
    def _():
        x = xs_ref[...].astype(BF16)
        h1 = jnp.dot(x, w1_ref[0], preferred_element_type=F32)
        h3 = jnp.dot(x, w3_ref[0], preferred_element_type=F32)
        ys_ref[...] += jnp.dot((_silu(h1) * h3).astype(BF16), w2_ref[0], preferred_element_type=F32)

    @pl.when(jnp.logical_not(active))
    def _():
        ys_ref[...] = jnp.zeros_like(ys_ref)


def _expert_call(block_expert, n_active, xs, w1, w3, w2):
    cap = xs.shape[0]
    nb = cap // MOE_BLOCK
    fc = D_FF_EXPERT // EXPERT_FSPLIT
    last_f = EXPERT_FSPLIT - 1

    def blk(i, na):
        return jnp.minimum(i, jnp.maximum(na[0] - 1, 0))

    def fidx(i, f, na):
        return jnp.where(i < na[0], f, last_f)

    grid_spec = pltpu.PrefetchScalarGridSpec(
        num_scalar_prefetch=2,
        grid=(nb, EXPERT_FSPLIT),
        in_specs=[pl.BlockSpec((MOE_BLOCK, D_MODEL), lambda i, f, be, na: (blk(i, na), 0)),
                  pl.BlockSpec((1, D_MODEL, fc), lambda i, f, be, na: (be[blk(i, na)], 0, fidx(i, f, na))),
                  pl.BlockSpec((1, D_MODEL, fc), lambda i, f, be, na: (be[blk(i, na)], 0, fidx(i, f, na))),
                  pl.BlockSpec((1, fc, D_MODEL), lambda i, f, be, na: (be[blk(i, na)], fidx(i, f, na), 0))],
        out_specs=pl.BlockSpec((MOE_BLOCK, D_MODEL), lambda i, f, be, na: (i, 0)),
    )
    return pl.pallas_call(
        _expert_kernel,
        grid_spec=grid_spec,
        out_shape=jax.ShapeDtypeStruct((cap, D_MODEL), F32),
        compiler_params=_cparams(("arbitrary", "arbitrary")),
        name="moe_experts",
    )(block_expert, n_active, xs, w1, w3, w2)


COMBINE_TC = 512


def _combine_kernel(dest_ref, dest_next_ref, x_ref, gt_ref, ml_ref, ys_hbm, o_ref, buf_ref, sem):
    i = pl.program_id(0)
    slot = i % 2

    def gather(d_ref, s):
        def issue(r, carry):
            for kk in range(TOP_K):
                _row_copy(ys_hbm, d_ref[kk, r], buf_ref.at[s, kk], r, sem.at[s]).start()
            return carry

        lax.fori_loop(0, COMBINE_TC, issue, 0, unroll=DMA_ISSUE_UNROLL)

    @pl.when(i == 0)
    def _():
        gather(dest_ref, 0)

    @pl.when(i + 1 < pl.num_programs(0))
    def _():
        gather(dest_next_ref, 1 - slot)

    for kk in range(TOP_K):
        pltpu.make_async_copy(ys_hbm.at[pl.ds(0, COMBINE_TC)], buf_ref.at[slot, kk], sem.at[slot]).wait()
    g = gt_ref[...]
    y = buf_ref[slot, 0] * g[:, 0:1] + buf_ref[slot, 1] * g[:, 1:2]
    o_ref[...] = x_ref[...] + ml_ref[0, 5:6, :] * y


def _combine_call(dest_t, x1_flat, gates, mods, ys):
    n = x1_flat.shape[0]
    tc = COMBINE_TC
    per_batch = SEQ // tc
    last = n // tc - 1
    return pl.pallas_call(
        _combine_kernel,
        grid=(n // tc,),
        in_specs=[pl.BlockSpec((TOP_K, tc), lambda i: (0, i), memory_space=pltpu.SMEM),
                  pl.BlockSpec((TOP_K, tc), lambda i: (0, jnp.minimum(i + 1, last)), memory_space=pltpu.SMEM),
                  pl.BlockSpec((tc, D_MODEL), lambda i: (i, 0)),
                  pl.BlockSpec((tc, TOP_K), lambda i: (i, 0)),
                  pl.BlockSpec((1, 6, D_MODEL), lambda i: (i // per_batch, 0, 0)),
                  pl.BlockSpec(memory_space=pl.ANY)],
        out_specs=pl.BlockSpec((tc, D_MODEL), lambda i: (i, 0)),
        out_shape=jax.ShapeDtypeStruct((n, D_MODEL), F32),
        scratch_shapes=[pltpu.VMEM((2, TOP_K, tc, D_MODEL), F32), pltpu.SemaphoreType.DMA((2,))],
        compiler_params=_cparams(("arbitrary",)),
        name="moe_combine",
    )(dest_t, dest_t, x1_flat, gates, mods, ys)


def _moe_layer(x1, mods, norm_g, router_w, router_b, w1, w3, w2):
    bsz = x1.shape[0]
    n = bsz * SEQ
    cap = (n * TOP_K + MOE_BLOCK - 1) // MOE_BLOCK * MOE_BLOCK + N_EXPERTS * MOE_BLOCK
    rw_p = jnp.zeros((D_MODEL, LANES), F32).at[:, :N_EXPERTS].set(router_w)
    rb_p = jnp.zeros((1, LANES), F32).at[0, :N_EXPERTS].set(router_b)
    h2, e_idx, rank, gates, counts = _router_call(x1, mods, norm_g, rw_p, rb_p)
    counts = counts[0, :N_EXPERTS]
    padded = (counts + MOE_BLOCK - 1) // MOE_BLOCK * MOE_BLOCK
    padded_end = jnp.cumsum(padded)
    base = padded_end - padded
    dest_t = (base[e_idx] + rank).T.astype(I32)
    n_blocks = cap // MOE_BLOCK
    block_start = jnp.arange(n_blocks, dtype=I32) * MOE_BLOCK
    block_expert = jnp.minimum(jnp.sum(padded_end[None, :] <= block_start[:, None], axis=1),
                               N_EXPERTS - 1).astype(I32)
    n_active = (padded_end[-1:] // MOE_BLOCK).astype(I32)
    tail = padded_end[-1] + jnp.arange(N_EXPERTS, dtype=I32) * MOE_BLOCK
    pad_start = jnp.concatenate([base + counts, tail])
    pad_start = (jnp.minimum(pad_start, cap - ZERO_BURST) // SUBLANES * SUBLANES).astype(I32)
    xs = _dispatch_call(pad_start, dest_t, h2, cap)
    ys = _expert_call(block_expert, n_active, xs, w1, w3, w2)
    out = _combine_call(dest_t, x1.reshape(n, D_MODEL), gates, mods, ys)
    return out.reshape(bsz, SEQ, D_MODEL)


def _rope_tables():
    rows = SEQ // GRID_W
    row = jnp.repeat(jnp.arange(rows), GRID_W).astype(F32)
    col = jnp.tile(jnp.arange(GRID_W), rows).astype(F32)
    n_freq = MLA_ROPE // 4
    inv = ROPE_BASE ** (-jnp.arange(n_freq, dtype=F32) / n_freq)
    ang = jnp.concatenate([row[:, None] * inv, col[:, None] * inv], axis=-1)
    cos, sin = jnp.cos(ang), jnp.sin(ang)
    ones = jnp.ones((SEQ, MLA_NOPE), F32)
    pad = jnp.zeros((SEQ, LANES - MLA_QK), F32)
    zn = jnp.zeros((SEQ, MLA_NOPE), F32)
    cos_t = jnp.concatenate([ones, cos, cos, pad], axis=-1)
    sin_t = jnp.concatenate([zn, -sin, sin, pad], axis=-1)
    ident = jnp.concatenate([jnp.ones((CTX_LEN, MLA_QK), F32), jnp.zeros((CTX_LEN, LANES - MLA_QK), F32)], axis=-1)
    zero = jnp.zeros((CTX_LEN, LANES), F32)
    return jnp.concatenate([cos_t, ident], axis=0), jnp.concatenate([sin_t, zero], axis=0)


def _rope_copy_pad(a):
    first_half = a[..., MLA_NOPE:MLA_NOPE + ROPE_HALF]
    zeros = jnp.zeros(a.shape[:-1] + (LANES - MLA_QK - ROPE_HALF,), a.dtype)
    return jnp.concatenate([a, first_half, zeros], axis=-1)


def _gate_cols(ga, d):
    h = MLSTM_HEADS
    return jnp.concatenate([ga[..., (2 + d) * h:(3 + d) * h], ga[..., d * h:(d + 1) * h]], axis=-1)


def _relayout_w_in(w_in):
    zeros = lambda n: jnp.zeros((D_MODEL, n), F32)
    w = MLSTM_WIDTH
    ga = w_in[:, OFF_GA:OFF_CQ]
    gate_tile = lambda d: jnp.concatenate([_gate_cols(ga, d), zeros(LANES - GATE_ROWS)], axis=-1)
    main = jnp.concatenate([
        w_in[:, OFF_CQ:OFF_KR],
        _rope_copy_pad(jnp.concatenate([zeros(MLA_NOPE), w_in[:, OFF_KR:OFF_GM]], axis=-1)),
        w_in[:, OFF_GM:IN_COLS],
        w_in[:, 0:w], w_in[:, 2 * w:3 * w],
        w_in[:, 3 * w:4 * w],
        gate_tile(0), gate_tile(1)], axis=-1)
    keys_t = (w_in[:, w:2 * w] * (MLSTM_HEAD_DIM ** -0.5)).T
    gates_t = jnp.concatenate([_gate_cols(ga, 0), _gate_cols(ga, 1)], axis=-1).T
    return main.astype(BF16), keys_t.astype(BF16), gates_t.astype(BF16)


def _relayout_gate_b(gb):
    per_dir = jnp.stack([_gate_cols(gb, 0), _gate_cols(gb, 1)])
    col_form = jnp.pad(per_dir, ((0, 0), (0, LANES - GATE_ROWS)))[:, None, :]
    row_form = jnp.broadcast_to(per_dir[:, :, None], (2, GATE_ROWS, LANES))
    return col_form, row_form


def _layer_params(l, w_in, w_out, mlstm_gate_b, mlstm_norm_g, mla_cq_g, mla_ckv_g, mla_w_uq, mla_w_ukv,
                  mla_q_g, mla_k_g, gmlp_ln_g, gmlp_ln_b, gmlp_w_s, gmlp_b_s):
    pad1 = lambda g: _rope_copy_pad(g)[None, :]
    w_uq = _rope_copy_pad(mla_w_uq[l].reshape(MLA_Q_RANK, MLA_HEADS, MLA_QK)).reshape(MLA_Q_RANK, -1)
    return dict(
        w_in=_relayout_w_in(w_in[l]),
        gate_b=_relayout_gate_b(mlstm_gate_b[l]),
        mlstm_g=mlstm_norm_g[l][None, :],
        cq_g=mla_cq_g[l][None, :], ckv_g=mla_ckv_g[l][None, :],
        w_uq=w_uq.astype(BF16),
        w_kv=mla_w_ukv[l].astype(BF16),
        q_g=pad1(mla_q_g[l]), k_g=pad1(mla_k_g[l]),
        ln_g=gmlp_ln_g[l][None, :], ln_b=gmlp_ln_b[l][None, :],
        w_s=gmlp_w_s[l].reshape(GMLP_GROUPS * CHUNK, CHUNK).astype(BF16),
        b_s=jnp.repeat(gmlp_b_s[l].T, GMLP_GROUP_DIM, axis=1),
        w_out=w_out[l].astype(BF16),
    )


def kernel(x, c, ctx, c_ctx, ada_w, ada_b, norm1_g, norm2_g, w_in, w_out, mlstm_gate_b, mlstm_norm_g, mla_cq_g, mla_ckv_g, mla_w_uq, mla_w_ukv, mla_q_g, mla_k_g, gmlp_ln_g, gmlp_ln_b, gmlp_w_s, gmlp_b_s, ffn_w1, ffn_w3, ffn_w2, moe_router_w, moe_router_b, moe_w1, moe_w3, moe_w2):
    bsz = x.shape[0]
    assert x.shape[1:] == (SEQ, D_MODEL) and ctx.shape[1:] == (CTX_LEN, D_MODEL)
    mod_rows = -(-(bsz + 1) // 8) * 8
    cvec = jnp.zeros((mod_rows, D_MODEL), F32).at[:bsz].set(c).at[bsz].set(c_ctx)
    mods_all = _ada_call(cvec, ada_w.astype(BF16), ada_b[:, None, :]).reshape(DEPTH, mod_rows, 6, D_MODEL)
    rope_tabs = _rope_tables()
    xall = (x, ctx) if DEPTH > 1 else jnp.concatenate([x, ctx], axis=1)
    moe_bf16 = {}
    for l in range(DEPTH):
        last = l == DEPTH - 1
        p = _layer_params(l, w_in, w_out, mlstm_gate_b, mlstm_norm_g, mla_cq_g, mla_ckv_g, mla_w_uq, mla_w_ukv,
                          mla_q_g, mla_k_g, gmlp_ln_g, gmlp_ln_b, gmlp_w_s, gmlp_b_s)
        mods = mods_all[l]
        z, zgc, zkt, zgt = _inproj_call(xall, mods, norm1_g[l][None, :], *p["w_in"], bsz)
        cast_ws = ()
        if (l + 1) % 2 == 1 and l + 1 < DEPTH:
            jn = (l + 1) // 2
            cast_ws = (moe_w1[jn].reshape(-1, D_FF_EXPERT), moe_w3[jn].reshape(-1, D_FF_EXPERT),
                       moe_w2[jn].reshape(-1, D_MODEL))
        a_mix, *cast = _mlstm_call(z, zgc, zkt, zgt, *p["gate_b"], p["mlstm_g"], cast_ws)
        if cast:
            moe_bf16[(l + 1) // 2] = (cast[0].reshape(N_EXPERTS, D_MODEL, D_FF_EXPERT),
                                      cast[1].reshape(N_EXPERTS, D_MODEL, D_FF_EXPERT),
                                      cast[2].reshape(N_EXPERTS, D_FF_EXPERT, D_MODEL))
        q, k, v = _qkv_call(z, p, rope_tabs)
        t_out = SEQ if last else T_ALL
        b_mix = _attn_call(q, k, v, t_out)
        x1 = _mixout_call(xall, z, a_mix, b_mix, mods, p, bsz, t_out, 1024 if last else 768)
        j = l // 2
        if l % 2 == 0:
            assert not last
            xall = _ffn_call(x1, mods, norm2_g[l][None, :], ffn_w1[j].astype(BF16), ffn_w3[j].astype(BF16),
                             ffn_w2[j].astype(BF16), bsz)
        else:
            assert last
            if j not in moe_bf16:
                moe_bf16[j] = (moe_w1[j].astype(BF16), moe_w3[j].astype(BF16), moe_w2[j].astype(BF16))
            xall = _moe_layer(x1, mods, norm2_g[l][None, :], moe_router_w[j], moe_router_b[j], *moe_bf16[j])
    return xall
```
